```python
import functools
import jax, jax.numpy as jnp
from jax import lax
import numpy as np

D_MODEL = 1024
BATCH = 32
SEQ = 2048
DEPTH = 2
DEC_BATCH = 32
DEC_SEQ = 32
PAST_LEN = 1024

CHUNK = 64
Q_BLOCK = 128
D_PLE = 256
FOX_HEADS = 8
FOX_HEAD_DIM = 64
FOX_W = FOX_HEADS * FOX_HEAD_DIM
MLA_HEADS = 8
MLA_Q_RANK = 256
MLA_KV_RANK = 128
MLA_NOPE = 64
MLA_ROPE = 32
MLA_V = 64
MLA_W = MLA_HEADS * MLA_V
ROPE_THETA = 10000.0
D_FF = 2816
N_EXPERTS = 8
TOP_K = 2
D_FF_EXPERT = 2816
N_DENSE = (DEPTH + 1) // 2
N_MOE = DEPTH // 2
DN_ALPHA = (2 * DEPTH) ** 0.25
DN_BETA = (8 * DEPTH) ** -0.25
LN_EPS = 1e-5
RMS_EPS = 1e-6
NEG_INF = -1e30
IN_SIZES = (FOX_W, FOX_W, FOX_W, FOX_HEADS, MLA_Q_RANK, MLA_KV_RANK, MLA_ROPE, D_MODEL, D_MODEL)
IN_WIDTH = 3 * FOX_W + FOX_HEADS + MLA_Q_RANK + MLA_KV_RANK + MLA_ROPE + 2 * D_MODEL

kernel_name = 'hybrid_fox_mla_streaming_step'


def layer_norm(x, g, b):
    xf = x.astype(jnp.float32)
    mu = jnp.mean(xf, axis=-1, keepdims=True)
    var = jnp.mean(jnp.square(xf - mu), axis=-1, keepdims=True)
    y = (xf - mu) * lax.rsqrt(var + LN_EPS)
    return (y * g.astype(jnp.float32) + b.astype(jnp.float32)).astype(x.dtype)


def rms_norm(x, g):
    xf = x.astype(jnp.float32)
    y = xf * lax.rsqrt(jnp.mean(jnp.square(xf), axis=-1, keepdims=True) + RMS_EPS)
    return (y * g.astype(jnp.float32)).astype(x.dtype)


def rope(x, pos):
    r = x.shape[-1]
    half = r // 2
    inv = ROPE_THETA ** (-jnp.arange(half, dtype=jnp.float32) * 2.0 / r)
    ang = pos.astype(jnp.float32)[:, None] * inv[None, :]
    cos = jnp.cos(ang)[:, None, :]
    sin = jnp.sin(ang)[:, None, :]
    x1 = x[..., :half].astype(jnp.float32)
    x2 = x[..., half:].astype(jnp.float32)
    return jnp.concatenate([x1 * cos - x2 * sin, x2 * cos + x1 * sin], axis=-1).astype(x.dtype)


def split_cols(z):
    out, start = [], 0
    for n in IN_SIZES:
        out.append(z[..., start:start + n])
        start += n
    return out


def attend(q, k, v, q_pos, k_pos, q_cum, k_cum, chunk_causal):
    b, s, h, dk = q.shape
    qb = Q_BLOCK if s % Q_BLOCK == 0 else s
    nb = s // qb
    scale = dk ** -0.5
    k_grp = k_pos // CHUNK if chunk_causal else k_pos
    k_cum_t = None if k_cum is None else jnp.swapaxes(k_cum.astype(jnp.float32), 1, 2)

    def to_blocks(a):
        return jnp.swapaxes(a.reshape((b, nb, qb) + a.shape[2:]), 0, 1)

    def one_block(args):
        q_blk, pos_blk, cum_blk = args
        sc = jnp.einsum('bqhd,bkhd->bhqk', q_blk, k).astype(jnp.float32) * scale
        if cum_blk is not None:
            sc = sc + jnp.swapaxes(cum_blk.astype(jnp.float32), 1, 2)[..., None] - k_cum_t[:, :, None, :]
        q_grp = pos_blk // CHUNK if chunk_causal else pos_blk
        visible = k_grp[None, :] <= q_grp[:, None]
        sc = jnp.where(visible, sc, NEG_INF)
        pr = jax.nn.softmax(sc, axis=-1).astype(v.dtype)
        return jnp.einsum('bhqk,bkhd->bqhd', pr, v)

    xs = (to_blocks(q), q_pos.reshape(nb, qb), None if q_cum is None else to_blocks(q_cum))
    out = lax.map(one_block, xs)
    return jnp.swapaxes(out, 0, 1).reshape(b, s, h, v.shape[-1])


def token_mixers(x, past, w_in, b_fox_f, g_mla_cq, w_mla_qb, g_mla_ckv, w_mla_kvb, w_o_fox, w_o_mla, w_out):
    b, s, _ = x.shape
    n_past = 0 if past is None else past[0].shape[1]
    q_pos = n_past + jnp.arange(s)
    k_pos = jnp.arange(n_past + s)
    t = n_past + s
    fq, fk, fv, ff, cq, ckv, kr, ga, gb = split_cols(x @ w_in)
    fq = fq.reshape(b, s, FOX_HEADS, FOX_HEAD_DIM)
    fk = fk.reshape(b, s, FOX_HEADS, FOX_HEAD_DIM)
    fv = fv.reshape(b, s, FOX_HEADS, FOX_HEAD_DIM)
    logf = jax.nn.log_sigmoid((ff + b_fox_f).astype(jnp.float32))
    ckv = rms_norm(ckv, g_mla_ckv)
    kr = rope(kr[:, :, None, :], q_pos)[:, :, 0, :]
    qm = (rms_norm(cq, g_mla_cq) @ w_mla_qb).reshape(b, s, MLA_HEADS, MLA_NOPE + MLA_ROPE)
    qm = jnp.concatenate([qm[..., :MLA_NOPE], rope(qm[..., MLA_NOPE:], q_pos)], axis=-1)
    new_rows = (fk, fv, logf, ckv, kr)
    if past is None:
        fk_all, fv_all, logf_all, ckv_all, kr_all = new_rows
    else:
        fk_all, fv_all, logf_all, ckv_all, kr_all = [jnp.concatenate([c, n], axis=1) for c, n in zip(past, new_rows)]
    cum = jnp.cumsum(logf_all.astype(jnp.float32), axis=1)
    o_fox = attend(fq, fk_all, fv_all, q_pos, k_pos, cum[:, n_past:], cum, False)
    kv = (ckv_all @ w_mla_kvb).reshape(b, t, MLA_HEADS, MLA_NOPE + MLA_V)
    k_rope_h = jnp.broadcast_to(kr_all[:, :, None, :], (b, t, MLA_HEADS, MLA_ROPE)).astype(kv.dtype)
    km = jnp.concatenate([kv[..., :MLA_NOPE], k_rope_h], axis=-1)
    o_mla = attend(qm, km, kv[..., MLA_NOPE:], q_pos, k_pos, None, None, True)
    merged = (jax.nn.sigmoid(ga) * (o_fox.reshape(b, s, FOX_W) @ w_o_fox)
              + jax.nn.sigmoid(gb) * (o_mla.reshape(b, s, MLA_W) @ w_o_mla))
    return merged @ w_out, new_rows


def swiglu(x, w_gate, w_up, w_down):
    return (jax.nn.silu(x @ w_gate) * (x @ w_up)) @ w_down


def moe_swiglu(x, w_router, b_router, w_gate, w_up, w_down):
    logits = (x @ w_router + b_router).astype(jnp.float32)
    top_val, top_idx = lax.top_k(logits, TOP_K)
    top_w = jax.nn.softmax(top_val, axis=-1)
    gates = jnp.sum(jax.nn.one_hot(top_idx, N_EXPERTS, dtype=jnp.float32) * top_w[..., None], axis=-2)
    gates = gates.astype(x.dtype)
    y = gates[..., 0:1] * swiglu(x, w_gate[0], w_up[0], w_down[0])
    for e in range(1, N_EXPERTS):
        y = y + gates[..., e:e + 1] * swiglu(x, w_gate[e], w_up[e], w_down[e])
    return y


def layer(x, past, p, mix_w, ln_mix_g, ln_mix_b, ffn, w_ple_proj, w_ple_gate, ln_ffn_g, ln_ffn_b):
    m, new_rows = token_mixers(x, past, *mix_w)
    h = layer_norm(DN_ALPHA * x + m, ln_mix_g, ln_mix_b)
    ple = jax.nn.sigmoid(h @ w_ple_gate) * (p @ w_ple_proj)
    y = layer_norm(DN_ALPHA * h + ffn(h) + ple, ln_ffn_g, ln_ffn_b)
    return y, new_rows


def setup_inputs(seed: int = 0) -> dict:
    key = jax.random.key(seed)
    keys = jax.random.split(key, 40)
    counter = [0]

    def nrm(shape, scale):
        k = keys[counter[0]]
        counter[0] += 1
        return jax.random.normal(k, shape, jnp.float32) * scale

    d = D_MODEL
    return {
        'x_prompt': nrm((BATCH, SEQ, d), 1.0),
        'x_sample': nrm((DEC_BATCH, DEC_SEQ, d), 1.0),
        'cache_fox_k': nrm((DEPTH, DEC_BATCH, PAST_LEN, FOX_HEADS, FOX_HEAD_DIM), 1.0),
        'cache_fox_v': nrm((DEPTH, DEC_BATCH, PAST_LEN, FOX_HEADS, FOX_HEAD_DIM), 1.0),
        'cache_fox_logf': jax.nn.log_sigmoid(4.0 + nrm((DEPTH, DEC_BATCH, PAST_LEN, FOX_HEADS), 1.0)),
        'cache_mla_ckv': nrm((DEPTH, DEC_BATCH, PAST_LEN, MLA_KV_RANK), 1.0),
        'cache_mla_krope': nrm((DEPTH, DEC_BATCH, PAST_LEN, MLA_ROPE), 1.0),
        'p_prompt': nrm((DEPTH, BATCH, SEQ, D_PLE), 1.0),
        'p_sample': nrm((DEPTH, DEC_BATCH, DEC_SEQ, D_PLE), 1.0),
        'w_in': nrm((DEPTH, d, IN_WIDTH), d ** -0.5),
        'b_fox_f': 4.0 + nrm((DEPTH, FOX_HEADS), 0.5),
        'g_mla_cq': 1.0 + nrm((DEPTH, MLA_Q_RANK), 0.05),
        'w_mla_qb': nrm((DEPTH, MLA_Q_RANK, MLA_HEADS * (MLA_NOPE + MLA_ROPE)), MLA_Q_RANK ** -0.5),
        'g_mla_ckv': 1.0 + nrm((DEPTH, MLA_KV_RANK), 0.05),
        'w_mla_kvb': nrm((DEPTH, MLA_KV_RANK, MLA_HEADS * (MLA_NOPE + MLA_V)), MLA_KV_RANK ** -0.5),
        'w_o_fox': nrm((DEPTH, FOX_W, d), FOX_W ** -0.5),
        'w_o_mla': nrm((DEPTH, MLA_W, d), MLA_W ** -0.5),
        'w_out': nrm((DEPTH, d, d), DN_BETA * d ** -0.5),
        'ln_mix_g': 1.0 + nrm((DEPTH, d), 0.05),
        'ln_mix_b': nrm((DEPTH, d), 0.02),
        'w_ffn_gate': nrm((N_DENSE, d, D_FF), d ** -0.5),
        'w_ffn_up': nrm((N_DENSE, d, D_FF), d ** -0.5),
        'w_ffn_down': nrm((N_DENSE, D_FF, d), DN_BETA * D_FF ** -0.5),
        'w_router': nrm((N_MOE, d, N_EXPERTS), d ** -0.5),
        'b_router': nrm((N_MOE, N_EXPERTS), 0.01),
        'w_moe_gate': nrm((N_MOE, N_EXPERTS, d, D_FF_EXPERT), d ** -0.5),
        'w_moe_up': nrm((N_MOE, N_EXPERTS, d, D_FF_EXPERT), d ** -0.5),
        'w_moe_down': nrm((N_MOE, N_EXPERTS, D_FF_EXPERT, d), DN_BETA * D_FF_EXPERT ** -0.5),
        'w_ple_proj': nrm((DEPTH, D_PLE, d), DN_BETA * D_PLE ** -0.5),
        'w_ple_gate': nrm((DEPTH, d, d), d ** -0.5),
        'ln_ffn_g': 1.0 + nrm((DEPTH, d), 0.05),
        'ln_ffn_b': nrm((DEPTH, d), 0.02),
    }


def reference(x_prompt, x_sample, cache_fox_k, cache_fox_v, cache_fox_logf, cache_mla_ckv, cache_mla_krope,
              p_prompt, p_sample, w_in, b_fox_f, g_mla_cq, w_mla_qb, g_mla_ckv, w_mla_kvb, w_o_fox, w_o_mla,
              w_out, ln_mix_g, ln_mix_b, w_ffn_gate, w_ffn_up, w_ffn_down, w_router, b_router, w_moe_gate,
              w_moe_up, w_moe_down, w_ple_proj, w_ple_gate, ln_ffn_g, ln_ffn_b):
    hp, hs = x_prompt, x_sample
    rows_p, rows_s = [], []
    for i in range(DEPTH):
        mix_w = (w_in[i], b_fox_f[i], g_mla_cq[i], w_mla_qb[i], g_mla_ckv[i], w_mla_kvb[i],
                 w_o_fox[i], w_o_mla[i], w_out[i])
        j = i // 2
        if i % 2 == 0:
            ffn = functools.partial(swiglu, w_gate=w_ffn_gate[j], w_up=w_ffn_up[j], w_down=w_ffn_down[j])
        else:
            ffn = functools.partial(moe_swiglu, w_router=w_router[j], b_router=b_router[j],
                                    w_gate=w_moe_gate[j], w_up=w_moe_up[j], w_down=w_moe_down[j])
        past = (cache_fox_k[i], cache_fox_v[i], cache_fox_logf[i], cache_mla_ckv[i], cache_mla_krope[i])
        hp, new_p = layer(hp, None, p_prompt[i], mix_w, ln_mix_g[i], ln_mix_b[i], ffn,
                          w_ple_proj[i], w_ple_gate[i], ln_ffn_g[i], ln_ffn_b[i])
        hs, new_s = layer(hs, past, p_sample[i], mix_w, ln_mix_g[i], ln_mix_b[i], ffn,
                          w_ple_proj[i], w_ple_gate[i], ln_ffn_g[i], ln_ffn_b[i])
        rows_p.append(new_p)
        rows_s.append(new_s)

    def stack(rows, idx):
        return jnp.stack([r[idx] for r in rows], axis=0)

    return (hp, hs,
            stack(rows_p, 0), stack(rows_p, 1), stack(rows_p, 2), stack(rows_p, 3), stack(rows_p, 4),
            stack(rows_s, 0), stack(rows_s, 1), stack(rows_s, 2), stack(rows_s, 3), stack(rows_s, 4))
```

```python
import functools

import jax
import jax.numpy as jnp
from jax import lax
from jax.experimental import pallas as pl
from jax.experimental.pallas import tpu as pltpu

CHUNK = 64
FOX_HEADS = 8
FOX_HEAD_DIM = 64
FOX_W = FOX_HEADS * FOX_HEAD_DIM
MLA_HEADS = 8
MLA_Q_RANK = 256
MLA_KV_RANK = 128
MLA_NOPE = 64
MLA_ROPE = 32
MLA_V = 64
MLA_W = MLA_HEADS * MLA_V
ROPE_THETA = 10000.0
N_EXPERTS = 8
LN_EPS = 1e-5
RMS_EPS = 1e-6
NEG_INF = -1e30

LANES = 128
PAIR_W = 2 * MLA_V
MLA_QK_W = 256
N_PAIRS = FOX_HEADS // 2
V7X_SCOPED_VMEM_BYTES = 60000 * 1024

BF16 = jnp.bfloat16
F32 = jnp.float32


def _dot(a, b):
    return jnp.dot(a, b, preferred_element_type=F32)


def _dot_nt(a, b):
    return lax.dot_general(a, b, (((1,), (1,)), ((), ())), preferred_element_type=F32)


def _sigmoid(x):
    return 1.0 / (1.0 + jnp.exp(-x))


def _log_sigmoid(x):
    return jnp.minimum(x, 0.0) - jnp.log1p(jnp.exp(-jnp.abs(x)))


def _rms_norm(x, g):
    return x * lax.rsqrt(jnp.mean(jnp.square(x), axis=-1, keepdims=True) + RMS_EPS) * g


def _layer_norm(x, g, b):
    mu = jnp.mean(x, axis=-1, keepdims=True)
    xc = x - mu
    var = jnp.mean(jnp.square(xc), axis=-1, keepdims=True)
    return xc * lax.rsqrt(var + LN_EPS) * g + b


def _resident(shape):
    nd = len(shape)
    return pl.BlockSpec(shape, lambda *_: (0,) * nd, pipeline_mode=pl.Buffered(1))


def _params(semantics, vmem_bytes):
    return pltpu.CompilerParams(dimension_semantics=semantics,
                                vmem_limit_bytes=min(int(vmem_bytes), V7X_SCOPED_VMEM_BYTES))


def _pick_tile(n, candidates):
    for c in candidates:
        if n % c == 0:
            return c
    return n


def _proj_kernel(x_ref, w1_ref, w2_ref, wfft_ref, bf_ref, bft_ref, gcq_ref, gckv_ref, wqa_ref, wqb_ref,
                 ctab_ref, stab_ref, cos_ref, sin_ref,
                 fq_ref, fk_ref, fv_ref, fkb_ref, fvb_ref, logf_ref, logft_ref, ckv_ref, kr_ref, qp_ref):
    xb = x_ref[0].astype(BF16)
    z1 = _dot(xb, w1_ref[...])
    fq_ref[0] = z1[:, :FOX_W].astype(BF16)
    fk = z1[:, FOX_W:2 * FOX_W]
    fk_ref[0] = fk
    fkb_ref[0] = fk.astype(BF16)
    fv = z1[:, 2 * FOX_W:]
    fv_ref[0] = fv
    fvb_ref[0] = fv.astype(BF16)

    z2 = _dot(xb, w2_ref[...])
    cq = z2[:, :MLA_Q_RANK]
    o = MLA_Q_RANK
    ckv = z2[:, o:o + MLA_KV_RANK]
    o += MLA_KV_RANK
    kr = z2[:, o:o + MLA_ROPE]
    krr = z2[:, o + LANES:o + LANES + MLA_ROPE]
    ff = z2[:, o + 2 * LANES:o + 2 * LANES + FOX_HEADS]
    logf_ref[0] = _log_sigmoid(ff + bf_ref[...])
    fft = _dot_nt(wfft_ref[...], xb)
    logft_ref[0] = _log_sigmoid(fft[:FOX_HEADS] + bft_ref[...])
    ckv_ref[0] = _rms_norm(ckv, gckv_ref[...])
    kr_ref[0] = kr * cos_ref[...] + krr * sin_ref[...]
    cqn = _rms_norm(cq, gcq_ref[...]).astype(BF16)
    qp = _dot(cqn, wqa_ref[...]) * ctab_ref[...] + _dot(cqn, wqb_ref[...]) * stab_ref[...]
    qp_ref[0] = qp.astype(BF16)


def _proj(x, w, tabs):
    b, s, d = x.shape
    tm = _pick_tile(s, (512, 256, 128))
    ns = s // tm
    w1, w2, wfft, bf, bft, gcq, gckv, wqa, wqb = w
    ctab, stab, cos2, sin2 = tabs
    qw = N_PAIRS * MLA_QK_W

    def tok(width):
        return pl.BlockSpec((1, tm, width), lambda si, bi: (bi, si, 0))

    def tab(width):
        return pl.BlockSpec((tm, width), lambda si, bi: (si, 0))

    in_specs = [tok(d), _resident(w1.shape), _resident(w2.shape), _resident(wfft.shape), _resident(bf.shape),
                _resident(bft.shape), _resident(gcq.shape), _resident(gckv.shape), _resident(wqa.shape),
                _resident(wqb.shape), tab(qw), tab(qw), tab(MLA_ROPE), tab(MLA_ROPE)]
    out_shape = (
        jax.ShapeDtypeStruct((b, s, FOX_W), BF16),
        jax.ShapeDtypeStruct((b, s, FOX_W), F32),
        jax.ShapeDtypeStruct((b, s, FOX_W), F32),
        jax.ShapeDtypeStruct((b, s, FOX_W), BF16),
        jax.ShapeDtypeStruct((b, s, FOX_W), BF16),
        jax.ShapeDtypeStruct((b, s, FOX_HEADS), F32),
        jax.ShapeDtypeStruct((b, FOX_HEADS, s), F32),
        jax.ShapeDtypeStruct((b, s, MLA_KV_RANK), F32),
        jax.ShapeDtypeStruct((b, s, MLA_ROPE), F32),
        jax.ShapeDtypeStruct((b, s, qw), BF16),
    )
    out_specs = (tok(FOX_W), tok(FOX_W), tok(FOX_W), tok(FOX_W), tok(FOX_W), tok(FOX_HEADS),
                 pl.BlockSpec((1, FOX_HEADS, tm), lambda si, bi: (bi, 0, si)),
                 tok(MLA_KV_RANK), tok(MLA_ROPE), tok(qw))
    return pl.pallas_call(
        _proj_kernel, grid=(ns, b), in_specs=in_specs, out_specs=out_specs, out_shape=out_shape,
        compiler_params=_params(("parallel", "parallel"), 48 << 20), name="proj",
    )(x, w1, w2, wfft, bf, bft, gcq, gckv, wqa, wqb, ctab, stab, cos2, sin2)


def _cumsum_kernel(x_ref, o_ref, *, ch):
    t = x_ref.shape[2]
    r = lax.broadcasted_iota(jnp.int32, (ch, ch), 0)
    c = lax.broadcasted_iota(jnp.int32, (ch, ch), 1)
    upper = jnp.where(r <= c, 1.0, 0.0).astype(BF16)
    carry = jnp.zeros((FOX_HEADS, 1), F32)
    zeros = jnp.zeros((FOX_HEADS, ch), F32)
    for ci in range(t // ch):
        xc = x_ref[0, :, ci * ch:(ci + 1) * ch]
        hi = xc.astype(BF16).astype(F32)
        r1 = xc - hi
        mid = r1.astype(BF16).astype(F32)
        lo = (r1 - mid).astype(BF16).astype(F32)
        pieces = jnp.concatenate([hi, mid, lo, zeros], axis=0).astype(BF16)
        pc = _dot(pieces, upper)
        cum = pc[0:8] + pc[8:16] + pc[16:24] + carry
        o_ref[0, :, ci * ch:(ci + 1) * ch] = -cum
        carry = cum[:, ch - 1:ch]


def _neg_cumsum(logft):
    b, h, t = logft.shape
    ch = 256 if t % 256 == 0 else LANES
    spec = pl.BlockSpec((1, h, t), lambda bi: (bi, 0, 0))
    return pl.pallas_call(
        functools.partial(_cumsum_kernel, ch=ch), grid=(b,), in_specs=[spec], out_specs=spec,
        out_shape=jax.ShapeDtypeStruct((b, h, t), F32),
        compiler_params=_params(("parallel",), 16 << 20), name="cumsum",
    )(logft)


def _kvx_kernel(ckv_ref, kr_ref, wk_ref, place_ref, wv_ref, kp_ref, vm_ref):
    cb = ckv_ref[0].astype(BF16)
    kp = _dot(cb, wk_ref[...]) + _dot(kr_ref[0].astype(BF16), place_ref[...])
    kp_ref[0] = kp.astype(BF16)
    vm_ref[0] = _dot(cb, wv_ref[...]).astype(BF16)


def _kv_expand(ckv, kr, wk, place, wv):
    b, t, _ = ckv.shape
    tm = _pick_tile(t, (512, 384, 256, 128))
    kw = N_PAIRS * MLA_QK_W

    def tok(width):
        return pl.BlockSpec((1, tm, width), lambda bi, ti: (bi, ti, 0))

    return pl.pallas_call(
        _kvx_kernel, grid=(b, t // tm),
        in_specs=[tok(MLA_KV_RANK), tok(MLA_ROPE), _resident(wk.shape), _resident(place.shape), _resident(wv.shape)],
        out_specs=(tok(kw), tok(MLA_W)),
        out_shape=(jax.ShapeDtypeStruct((b, t, kw), BF16), jax.ShapeDtypeStruct((b, t, MLA_W), BF16)),
        compiler_params=_params(("parallel", "parallel"), 32 << 20), name="kvexpand",
    )(ckv, kr, wk, place, wv)


def _attn_kernel(*refs, tq, tk, width, q_off, t_valid, chunked, has_bias):
    if has_bias:
        q_ref, k_ref, v_ref, nb_ref, o_ref = refs
    else:
        q_ref, k_ref, v_ref, o_ref = refs
        nb_ref = None
    nkb = k_ref.shape[1] // tk
    q_min = q_off + pl.program_id(2) * tq
    q_max = q_min + tq - 1
    if chunked:
        lim_min = (q_min // CHUNK + 1) * CHUNK
        lim_max = (q_max // CHUNK + 1) * CHUNK
    else:
        lim_min = q_min + 1
        lim_max = q_max + 1
    n_full = jnp.minimum(lim_min, t_valid) // tk
    n_vis = jnp.minimum((jnp.minimum(lim_max, t_valid) + tk - 1) // tk, nkb)

    qpos = q_min + lax.broadcasted_iota(jnp.int32, (tq, 1), 0)
    if chunked:
        row_lim = lax.shift_left(lax.shift_right_logical(qpos, 6) + 1, 6)
    else:
        row_lim = qpos + 1
    row_lim = jnp.minimum(row_lim, t_valid)

    q = q_ref[0]
    lane = lax.broadcasted_iota(jnp.int32, (1, width), 1)

    def head_out(hh):
        if width == LANES:
            keep = (lane // FOX_HEAD_DIM) == hh
        else:
            nope = (lane < 2 * MLA_NOPE) & ((lane // MLA_NOPE) == hh)
            rope = (lane >= 2 * MLA_NOPE) & (lane < 2 * MLA_NOPE + 2 * MLA_ROPE) & (
                ((lane - 2 * MLA_NOPE) // MLA_ROPE) == hh)
            keep = nope | rope
        qh = jnp.where(keep, q, jnp.zeros_like(q))

        def step(kb, carry, masked):
            m, l, acc = carry
            k0 = pl.multiple_of(kb * tk, tk)
            s = _dot_nt(qh, k_ref[0, pl.ds(k0, tk), :])
            if has_bias:
                s = s + nb_ref[0, 0, hh:hh + 1, pl.ds(k0, tk)]
            if masked:
                kpos = k0 + lax.broadcasted_iota(jnp.int32, (1, tk), 1)
                s = jnp.where(kpos < row_lim, s, NEG_INF)
            m_new = jnp.maximum(m, jnp.max(s, axis=-1, keepdims=True))
            p = jnp.exp(s - m_new)
            alpha = jnp.exp(m - m_new)
            l = alpha * l + jnp.sum(p, axis=-1, keepdims=True)
            acc = alpha * acc + _dot(p.astype(BF16), v_ref[0, pl.ds(k0, tk), :])
            return m_new, l, acc

        carry = (jnp.full((tq, 1), NEG_INF, F32), jnp.zeros((tq, 1), F32), jnp.zeros((tq, PAIR_W), F32))
        carry = lax.fori_loop(0, n_full, functools.partial(step, masked=False), carry)
        _, l, acc = lax.fori_loop(n_full, n_vis, functools.partial(step, masked=True), carry)
        return acc * (1.0 / l)

    o0 = head_out(0)
    o1 = head_out(1)
    olane = lax.broadcasted_iota(jnp.int32, (1, PAIR_W), 1)
    o_ref[0] = jnp.where(olane < MLA_V, o0, o1).astype(BF16)


def _attention(q, k, v, nb, *, q_off, t_valid, chunked):
    assert CHUNK == 64
    b, sq, qw = q.shape
    width = qw // N_PAIRS
    tp = k.shape[1]
    tq = _pick_tile(sq, (512, 256, 128))
    tk = _pick_tile(tp, (512, 384, 256, 128))
    in_specs = [pl.BlockSpec((1, tq, width), lambda bi, pi, qi: (bi, qi, pi)),
                pl.BlockSpec((1, tp, width), lambda bi, pi, qi: (bi, 0, pi)),
                pl.BlockSpec((1, tp, PAIR_W), lambda bi, pi, qi: (bi, 0, pi))]
    args = [q, k, v]
    if nb is not None:
        in_specs.append(pl.BlockSpec((1, 1, 2, tp), lambda bi, pi, qi: (bi, pi, 0, 0)))
        args.append(nb)
    kern = functools.partial(_attn_kernel, tq=tq, tk=tk, width=width, q_off=q_off, t_valid=t_valid,
                             chunked=chunked, has_bias=nb is not None)
    return pl.pallas_call(
        kern, grid=(b, N_PAIRS, sq // tq), in_specs=in_specs,
        out_specs=pl.BlockSpec((1, tq, PAIR_W), lambda bi, pi, qi: (bi, qi, pi)),
        out_shape=jax.ShapeDtypeStruct((b, sq, N_PAIRS * PAIR_W), BF16),
        compiler_params=_params(("parallel", "parallel", "parallel"), 40 << 20),
        name="attn_mla" if chunked else "attn_fox",
    )(*args)


def _mix_kernel(x_ref, of_ref, om_ref, wga_ref, wgb_ref, wof_ref, wom_ref, wout_ref, g_ref, b_ref, h_ref, *, alpha):
    x = x_ref[...]
    xb = x.astype(BF16)
    a = _sigmoid(_dot(xb, wga_ref[...])) * _dot(of_ref[...], wof_ref[...])
    bm = _sigmoid(_dot(xb, wgb_ref[...])) * _dot(om_ref[...], wom_ref[...])
    m = _dot((a + bm).astype(BF16), wout_ref[...])
    h_ref[...] = _layer_norm(alpha * x + m, g_ref[...], b_ref[...])


def _mix(x, of, om, w, alpha):
    r, d = x.shape
    tm = _pick_tile(r, (512, 256, 128))
    wga, wgb, wof, wom, wout, g, bb = w

    def row(width):
        return pl.BlockSpec((tm, width), lambda i: (i, 0))

    return pl.pallas_call(
        functools.partial(_mix_kernel, alpha=alpha), grid=(r // tm,),
        in_specs=[row(d), row(FOX_W), row(MLA_W)] + [_resident(a.shape) for a in w],
        out_specs=row(d), out_shape=jax.ShapeDtypeStruct((r, d), F32),
        compiler_params=_params(("parallel",), 48 << 20), name="mix",
    )(x, of, om, wga, wgb, wof, wom, wout, g, bb)


def _ple(hb, p_ref, wpg_ref, wpp_ref):
    return _sigmoid(_dot(hb, wpg_ref[...])) * _dot(p_ref[...].astype(BF16), wpp_ref[...])


def _ffn_chunk(n_ff):
    return _pick_tile(n_ff, (1408, 1024, 512, 256, 128))


def _dense_ffn_kernel(h_ref, p_ref, wg_ref, wu_ref, wd_ref, wpg_ref, wpp_ref, g_ref, b_ref, y_ref, *, alpha, fc):
    h = h_ref[...]
    hb = h.astype(BF16)
    n_ff = wg_ref.shape[1]
    acc = jnp.zeros(h.shape, F32)
    for c in range(n_ff // fc):
        sl = slice(c * fc, (c + 1) * fc)
        gt = _dot(hb, wg_ref[:, sl])
        act = (gt * _sigmoid(gt) * _dot(hb, wu_ref[:, sl])).astype(BF16)
        acc = acc + _dot(act, wd_ref[sl, :])
    y = alpha * h + acc + _ple(hb, p_ref, wpg_ref, wpp_ref)
    y_ref[...] = _layer_norm(y, g_ref[...], b_ref[...])


def _dense_ffn(h, p, w, alpha):
    r, d = h.shape
    tm = _pick_tile(r, (512, 256, 128))
    wg, wu, wd, wpg, wpp, g, bb = w

    def row(width):
        return pl.BlockSpec((tm, width), lambda i: (i, 0))

    return pl.pallas_call(
        functools.partial(_dense_ffn_kernel, alpha=alpha, fc=_ffn_chunk(wg.shape[1])), grid=(r // tm,),
        in_specs=[row(d), row(p.shape[1])] + [_resident(a.shape) for a in w],
        out_specs=row(d), out_shape=jax.ShapeDtypeStruct((r, d), F32),
        compiler_params=_params(("parallel",), 56 << 20), name="ffn_dense",
    )(h, p, wg, wu, wd, wpg, wpp, g, bb)


def _moe_ffn_kernel(h_ref, p_ref, wr_ref, br_ref, wg_ref, wu_ref, wd_ref, wpg_ref, wpp_ref, g_ref, b_ref,
                    y_ref, gates_ref, acc_ref, *, alpha):
    e = pl.program_id(1)
    c = pl.program_id(2)
    last = (e == pl.num_programs(1) - 1) & (c == pl.num_programs(2) - 1)
    h = h_ref[...]
    hb = h.astype(BF16)
    lane = lax.broadcasted_iota(jnp.int32, (1, LANES), 1)

    @pl.when((e == 0) & (c == 0))
    def _route():
        logits = _dot(hb, wr_ref[...]) + br_ref[...]
        lg = jnp.where(lane < N_EXPERTS, logits, -jnp.inf)
        m1 = jnp.max(lg, axis=-1, keepdims=True)
        i1 = jnp.min(jnp.where(lg == m1, lane, LANES), axis=-1, keepdims=True)
        lg2 = jnp.where(lane == i1, -jnp.inf, lg)
        m2 = jnp.max(lg2, axis=-1, keepdims=True)
        i2 = jnp.min(jnp.where(lg2 == m2, lane, LANES), axis=-1, keepdims=True)
        e2 = jnp.exp(m2 - m1)
        den = 1.0 + e2
        gates_ref[...] = jnp.where(lane == i1, 1.0 / den, 0.0) + jnp.where(lane == i2, e2 / den, 0.0)
        acc_ref[...] = jnp.zeros(acc_ref.shape, F32)

    gt = _dot(hb, wg_ref[0])
    act = (gt * _sigmoid(gt) * _dot(hb, wu_ref[0])).astype(BF16)
    gate = jnp.sum(jnp.where(lane == e, gates_ref[...], 0.0), axis=-1, keepdims=True)
    acc_ref[...] += gate * _dot(act, wd_ref[0])

    @pl.when(last)
    def _finish():
        y = alpha * h + acc_ref[...] + _ple(hb, p_ref, wpg_ref, wpp_ref)
        y_ref[...] = _layer_norm(y, g_ref[...], b_ref[...])


def _moe_ffn(h, p, w, alpha):
    r, d = h.shape
    tm = _pick_tile(r, (512, 256, 128))
    wr, br, wg, wu, wd, wpg, wpp, g, bb = w
    n_e, _, n_ff = wg.shape
    fc = _ffn_chunk(n_ff)

    def row(width):
        return pl.BlockSpec((tm, width), lambda i, e, c: (i, 0))

    in_specs = [row(d), row(p.shape[1]), _resident(wr.shape), _resident(br.shape),
                pl.BlockSpec((1, d, fc), lambda i, e, c: (e, 0, c)),
                pl.BlockSpec((1, d, fc), lambda i, e, c: (e, 0, c)),
                pl.BlockSpec((1, fc, d), lambda i, e, c: (e, c, 0)),
                _resident(wpg.shape), _resident(wpp.shape), _resident(g.shape), _resident(bb.shape)]
    return pl.pallas_call(
        functools.partial(_moe_ffn_kernel, alpha=alpha), grid=(r // tm, n_e, n_ff // fc),
        in_specs=in_specs, out_specs=row(d), out_shape=jax.ShapeDtypeStruct((r, d), F32),
        scratch_shapes=[pltpu.VMEM((tm, LANES), F32), pltpu.VMEM((tm, d), F32)],
        compiler_params=_params(("parallel", "arbitrary", "arbitrary"), 56 << 20), name="ffn_moe",
    )(h, p, wr, br, wg, wu, wd, wpg, wpp, g, bb)


def _pad_lanes(a, width=LANES):
    return jnp.pad(a, ((0, 0), (0, width - a.shape[1])))


def _rot_half_cols(a):
    half = a.shape[-1] // 2
    return jnp.concatenate([-a[..., half:], a[..., :half]], axis=-1)


def _mixer_weights(w_in, b_fox_f, g_cq, w_qb, g_ckv, w_kvb):
    d = w_in.shape[0]
    sizes = (FOX_W, FOX_W, FOX_W, FOX_HEADS, MLA_Q_RANK, MLA_KV_RANK, MLA_ROPE, d, d)
    cols, start = [], 0
    for n in sizes:
        cols.append(w_in[:, start:start + n])
        start += n
    wfq, wfk, wfv, wff, wcq, wckv, wkr, wga, wgb = cols
    w1 = jnp.concatenate([wfq * (FOX_HEAD_DIM ** -0.5), wfk, wfv], axis=1).astype(BF16)
    w2 = jnp.concatenate([wcq, wckv, _pad_lanes(wkr), _pad_lanes(_rot_half_cols(wkr)), _pad_lanes(wff)],
                         axis=1).astype(BF16)
    wfft = jnp.pad(wff.T, ((0, 16 - FOX_HEADS), (0, 0))).astype(BF16)
    bf = b_fox_f.reshape(1, FOX_HEADS)
    bft = b_fox_f.reshape(FOX_HEADS, 1)

    qb = w_qb.reshape(MLA_Q_RANK, MLA_HEADS, MLA_NOPE + MLA_ROPE)
    q_nope, q_rope = qb[..., :MLA_NOPE], qb[..., MLA_NOPE:]
    q_rot = _rot_half_cols(q_rope)
    z_pad = jnp.zeros((MLA_Q_RANK, MLA_QK_W - 2 * MLA_NOPE - 2 * MLA_ROPE), F32)
    z_nope = jnp.zeros((MLA_Q_RANK, 2 * MLA_NOPE), F32)
    wqa = jnp.concatenate([jnp.concatenate([q_nope[:, 2 * j], q_nope[:, 2 * j + 1], q_rope[:, 2 * j],
                                            q_rope[:, 2 * j + 1], z_pad], axis=1) for j in range(N_PAIRS)], axis=1)
    wqb = jnp.concatenate([jnp.concatenate([z_nope, q_rot[:, 2 * j], q_rot[:, 2 * j + 1], z_pad], axis=1)
                           for j in range(N_PAIRS)], axis=1)

    kvb = w_kvb.reshape(MLA_KV_RANK, MLA_HEADS, MLA_NOPE + MLA_V)
    k_nope, v_up = kvb[..., :MLA_NOPE], kvb[..., MLA_NOPE:]
    zk = jnp.zeros((MLA_KV_RANK, MLA_QK_W - 2 * MLA_NOPE), F32)
    wk = jnp.concatenate([jnp.concatenate([k_nope[:, 2 * j], k_nope[:, 2 * j + 1], zk], axis=1)
                          for j in range(N_PAIRS)], axis=1).astype(BF16)
    eye = jnp.eye(MLA_ROPE, dtype=F32)
    place = jnp.concatenate([jnp.zeros((MLA_ROPE, 2 * MLA_NOPE), F32), eye, eye,
                             jnp.zeros((MLA_ROPE, MLA_QK_W - 2 * MLA_NOPE - 2 * MLA_ROPE), F32)], axis=1)
    place = jnp.tile(place, (1, N_PAIRS)).astype(BF16)
    wv = v_up.reshape(MLA_KV_RANK, MLA_W).astype(BF16)
    proj_w = (w1, w2, wfft, bf, bft, g_cq.reshape(1, -1), g_ckv.reshape(1, -1), wqa.astype(BF16), wqb.astype(BF16))
    return proj_w, (wk, place, wv), (wga.astype(BF16), wgb.astype(BF16))


def _rope_tables(pos):
    half = MLA_ROPE // 2
    inv = ROPE_THETA ** (-jnp.arange(half, dtype=F32) * 2.0 / MLA_ROPE)
    ang = pos.astype(F32)[:, None] * inv[None, :]
    cos2 = jnp.concatenate([jnp.cos(ang)] * 2, axis=1)
    sin2 = jnp.concatenate([jnp.sin(ang)] * 2, axis=1)
    n = pos.shape[0]
    scale = (MLA_NOPE + MLA_ROPE) ** -0.5
    pad = jnp.zeros((n, MLA_QK_W - 2 * MLA_NOPE - 2 * MLA_ROPE), F32)
    ctab = jnp.concatenate([jnp.full((n, 2 * MLA_NOPE), scale, F32), scale * cos2, scale * cos2, pad], axis=1)
    stab = jnp.concatenate([jnp.zeros((n, 2 * MLA_NOPE), F32), scale * sin2, scale * sin2, pad], axis=1)
    return jnp.tile(ctab, (1, N_PAIRS)), jnp.tile(stab, (1, N_PAIRS)), cos2, sin2


def _pad_time(a, tp, axis):
    pad = [(0, 0)] * a.ndim
    pad[axis] = (0, tp - a.shape[axis])
    return jnp.pad(a, pad)


def _layer(x, past, p, proj_w, kvx_w, mix_w, ffn_w, is_moe, alpha):
    b, s, d = x.shape
    n_past = 0 if past is None else past[0].shape[1]
    t = n_past + s
    tabs = _rope_tables(n_past + jnp.arange(s))
    fq, fk, fv, fkb, fvb, logf, logft, ckv, kr, qp = _proj(x, proj_w, tabs)
    if past is None:
        tp = t
        k_fox, v_fox, logft_all, ckv_all, kr_all = fkb, fvb, logft, ckv, kr
    else:
        tp = -(-t // LANES) * LANES
        pk, pv, plogf, pckv, pkr = past
        k_fox = _pad_time(jnp.concatenate([pk.reshape(b, n_past, FOX_W).astype(BF16), fkb], axis=1), tp, 1)
        v_fox = _pad_time(jnp.concatenate([pv.reshape(b, n_past, FOX_W).astype(BF16), fvb], axis=1), tp, 1)
        logft_all = _pad_time(jnp.concatenate([jnp.swapaxes(plogf, 1, 2), logft], axis=2), tp, 2)
        ckv_all = _pad_time(jnp.concatenate([pckv, ckv], axis=1), tp, 1)
        kr_all = _pad_time(jnp.concatenate([pkr, kr], axis=1), tp, 1)
    nb = _neg_cumsum(logft_all).reshape(b, N_PAIRS, 2, tp)
    k_mla, v_mla = _kv_expand(ckv_all, kr_all, *kvx_w)
    o_fox = _attention(fq, k_fox, v_fox, nb, q_off=n_past, t_valid=t, chunked=False)
    o_mla = _attention(qp, k_mla, v_mla, None, q_off=n_past, t_valid=t, chunked=True)
    r = b * s
    h = _mix(x.reshape(r, d), o_fox.reshape(r, FOX_W), o_mla.reshape(r, MLA_W), mix_w, alpha)
    ffn = _moe_ffn if is_moe else _dense_ffn
    y = ffn(h, p.reshape(r, -1), ffn_w, alpha).reshape(b, s, d)
    new_rows = (fk.reshape(b, s, FOX_HEADS, FOX_HEAD_DIM), fv.reshape(b, s, FOX_HEADS, FOX_HEAD_DIM), logf, ckv, kr)
    return y, new_rows


def kernel(x_prompt, x_sample, cache_fox_k, cache_fox_v, cache_fox_logf, cache_mla_ckv, cache_mla_krope,
           p_prompt, p_sample, w_in, b_fox_f, g_mla_cq, w_mla_qb, g_mla_ckv, w_mla_kvb, w_o_fox, w_o_mla,
           w_out, ln_mix_g, ln_mix_b, w_ffn_gate, w_ffn_up, w_ffn_down, w_router, b_router, w_moe_gate,
           w_moe_up, w_moe_down, w_ple_proj, w_ple_gate, ln_ffn_g, ln_ffn_b):
    depth = w_in.shape[0]
    alpha = (2 * depth) ** 0.25
    hp, hs = x_prompt, x_sample
    rows_p, rows_s = [], []
    for i in range(depth):
        proj_w, kvx_w, (wga, wgb) = _mixer_weights(w_in[i], b_fox_f[i], g_mla_cq[i], w_mla_qb[i], g_mla_ckv[i],
                                                   w_mla_kvb[i])
        mix_w = (wga, wgb, w_o_fox[i].astype(BF16), w_o_mla[i].astype(BF16), w_out[i].astype(BF16),
                 ln_mix_g[i].reshape(1, -1), ln_mix_b[i].reshape(1, -1))
        tail = (w_ple_gate[i].astype(BF16), w_ple_proj[i].astype(BF16),
                ln_ffn_g[i].reshape(1, -1), ln_ffn_b[i].reshape(1, -1))
        j = i // 2
        is_moe = i % 2 == 1
        if is_moe:
            ffn_w = (_pad_lanes(w_router[j]).astype(BF16), _pad_lanes(b_router[j].reshape(1, -1)),
                     w_moe_gate[j].astype(BF16), w_moe_up[j].astype(BF16), w_moe_down[j].astype(BF16)) + tail
        else:
            ffn_w = (w_ffn_gate[j].astype(BF16), w_ffn_up[j].astype(BF16), w_ffn_down[j].astype(BF16)) + tail
        past = (cache_fox_k[i], cache_fox_v[i], cache_fox_logf[i], cache_mla_ckv[i], cache_mla_krope[i])
        hp, new_p = _layer(hp, None, p_prompt[i], proj_w, kvx_w, mix_w, ffn_w, is_moe, alpha)
        hs, new_s = _layer(hs, past, p_sample[i], proj_w, kvx_w, mix_w, ffn_w, is_moe, alpha)
        rows_p.append(new_p)
        rows_s.append(new_s)

    def stack(rows, idx):
        return jnp.stack([r[idx] for r in rows], axis=0)

    return (hp, hs) + tuple(stack(rows_p, k) for k in range(5)) + tuple(stack(rows_s, k) for k in range(5))
```

```python
import functools

import jax
import jax.numpy as jnp
from jax import lax
from jax.experimental import pallas as pl
from jax.experimental.pallas import tpu as pltpu

CHUNK = 64
FOX_HEADS = 8
FOX_HEAD_DIM = 64
FOX_W = FOX_HEADS * FOX_HEAD_DIM
MLA_HEADS = 8
MLA_Q_RANK = 256
MLA_KV_RANK = 128
MLA_NOPE = 64
MLA_ROPE = 32
MLA_V = 64
MLA_W = MLA_HEADS * MLA_V
ROPE_THETA = 10000.0
N_EXPERTS = 8
LN_EPS = 1e-5
RMS_EPS = 1e-6
NEG_INF = -1e30
LOG2E = 1.4426950408889634

LANES = 128
PAIR_W = 2 * MLA_V
MLA_QK_W = 256
N_PAIRS = FOX_HEADS // 2
V7X_SCOPED_VMEM_BYTES = 60000 * 1024

BF16 = jnp.bfloat16
F32 = jnp.float32


def _dot(a, b):
    return jnp.dot(a, b, preferred_element_type=F32)


def _dot_nt(a, b):
    return lax.dot_general(a, b, (((1,), (1,)), ((), ())), preferred_element_type=F32)


def _sigmoid(x):
    return 1.0 / (1.0 + jnp.exp(-x))


def _log_sigmoid(x):
    return jnp.minimum(x, 0.0) - jnp.log1p(jnp.exp(-jnp.abs(x)))


def _rms_norm(x, g):
    return x * lax.rsqrt(jnp.mean(jnp.square(x), axis=-1, keepdims=True) + RMS_EPS) * g


def _layer_norm(x, g, b):
    mu = jnp.mean(x, axis=-1, keepdims=True)
    xc = x - mu
    var = jnp.mean(jnp.square(xc), axis=-1, keepdims=True)
    return xc * lax.rsqrt(var + LN_EPS) * g + b


def _resident(shape):
    nd = len(shape)
    return pl.BlockSpec(shape, lambda *_: (0,) * nd, pipeline_mode=pl.Buffered(1))


def _params(semantics, vmem_bytes):
    return pltpu.CompilerParams(dimension_semantics=semantics,
                                vmem_limit_bytes=min(int(vmem_bytes), V7X_SCOPED_VMEM_BYTES))


def _pick_tile(n, candidates):
    for c in candidates:
        if n % c == 0:
            return c
    return n


def _proj_kernel(x_ref, w1_ref, w2_ref, wfft_ref, bf_ref, bft_ref, gcq_ref, gckv_ref, wqa_ref, wqb_ref,
                 ctab_ref, stab_ref, cos_ref, sin_ref,
                 fq_ref, fk_ref, fv_ref, fkb_ref, fvb_ref, logf_ref, logft_ref, ckv_ref, kr_ref, qp_ref):
    xb = x_ref[0].astype(BF16)
    z1 = _dot(xb, w1_ref[...])
    fq_ref[0] = z1[:, :FOX_W].astype(BF16)
    fk = z1[:, FOX_W:2 * FOX_W]
    fk_ref[0] = fk
    fkb_ref[0] = fk.astype(BF16)
    fv = z1[:, 2 * FOX_W:]
    fv_ref[0] = fv
    fvb_ref[0] = fv.astype(BF16)

    z2 = _dot(xb, w2_ref[...])
    cq = z2[:, :MLA_Q_RANK]
    o = MLA_Q_RANK
    ckv = z2[:, o:o + MLA_KV_RANK]
    o += MLA_KV_RANK
    kr = z2[:, o:o + MLA_ROPE]
    krr = z2[:, o + LANES:o + LANES + MLA_ROPE]
    ff = z2[:, o + 2 * LANES:o + 2 * LANES + FOX_HEADS]
    logf_ref[0] = _log_sigmoid(ff + bf_ref[...])
    fft = _dot_nt(wfft_ref[...], xb)
    logft_ref[0] = _log_sigmoid(fft[:FOX_HEADS] + bft_ref[...])
    ckv_ref[0] = _rms_norm(ckv, gckv_ref[...])
    kr_ref[0] = kr * cos_ref[...] + krr * sin_ref[...]
    cqn = _rms_norm(cq, gcq_ref[...]).astype(BF16)
    qp = _dot(cqn, wqa_ref[...]) * ctab_ref[...] + _dot(cqn, wqb_ref[...]) * stab_ref[...]
    qp_ref[0] = qp.astype(BF16)


def _proj(x, w, tabs):
    b, s, d = x.shape
    tm = _pick_tile(s, (512, 256, 128))
    ns = s // tm
    w1, w2, wfft, bf, bft, gcq, gckv, wqa, wqb = w
    ctab, stab, cos2, sin2 = tabs
    qw = N_PAIRS * MLA_QK_W

    def tok(width):
        return pl.BlockSpec((1, tm, width), lambda si, bi: (bi, si, 0))

    def tab(width):
        return pl.BlockSpec((tm, width), lambda si, bi: (si, 0))

    in_specs = [tok(d), _resident(w1.shape), _resident(w2.shape), _resident(wfft.shape), _resident(bf.shape),
                _resident(bft.shape), _resident(gcq.shape), _resident(gckv.shape), _resident(wqa.shape),
                _resident(wqb.shape), tab(qw), tab(qw), tab(MLA_ROPE), tab(MLA_ROPE)]
    out_shape = (
        jax.ShapeDtypeStruct((b, s, FOX_W), BF16),
        jax.ShapeDtypeStruct((b, s, FOX_W), F32),
        jax.ShapeDtypeStruct((b, s, FOX_W), F32),
        jax.ShapeDtypeStruct((b, s, FOX_W), BF16),
        jax.ShapeDtypeStruct((b, s, FOX_W), BF16),
        jax.ShapeDtypeStruct((b, s, FOX_HEADS), F32),
        jax.ShapeDtypeStruct((b, FOX_HEADS, s), F32),
        jax.ShapeDtypeStruct((b, s, MLA_KV_RANK), F32),
        jax.ShapeDtypeStruct((b, s, MLA_ROPE), F32),
        jax.ShapeDtypeStruct((b, s, qw), BF16),
    )
    out_specs = (tok(FOX_W), tok(FOX_W), tok(FOX_W), tok(FOX_W), tok(FOX_W), tok(FOX_HEADS),
                 pl.BlockSpec((1, FOX_HEADS, tm), lambda si, bi: (bi, 0, si)),
                 tok(MLA_KV_RANK), tok(MLA_ROPE), tok(qw))
    return pl.pallas_call(
        _proj_kernel, grid=(ns, b), in_specs=in_specs, out_specs=out_specs, out_shape=out_shape,
        compiler_params=_params(("arbitrary", "arbitrary"), 48 << 20), name="proj",
    )(x, w1, w2, wfft, bf, bft, gcq, gckv, wqa, wqb, ctab, stab, cos2, sin2)


def _cumsum_kernel(x_ref, o_ref, *, ch):
    t = x_ref.shape[2]
    r = lax.broadcasted_iota(jnp.int32, (ch, ch), 0)
    c = lax.broadcasted_iota(jnp.int32, (ch, ch), 1)
    upper = jnp.where(r <= c, 1.0, 0.0).astype(BF16)
    carry = jnp.zeros((FOX_HEADS, 1), F32)
    zeros = jnp.zeros((FOX_HEADS, ch), F32)
    for ci in range(t // ch):
        xc = x_ref[0, :, ci * ch:(ci + 1) * ch]
        hi = xc.astype(BF16).astype(F32)
        r1 = xc - hi
        mid = r1.astype(BF16).astype(F32)
        lo = (r1 - mid).astype(BF16).astype(F32)
        pieces = jnp.concatenate([hi, mid, lo, zeros], axis=0).astype(BF16)
        pc = _dot(pieces, upper)
        cum = pc[0:8] + pc[8:16] + pc[16:24] + carry
        o_ref[0, :, ci * ch:(ci + 1) * ch] = cum * (-LOG2E)
        carry = cum[:, ch - 1:ch]


def _neg_cumsum(logft):
    b, h, t = logft.shape
    ch = 256 if t % 256 == 0 else LANES
    spec = pl.BlockSpec((1, h, t), lambda bi: (bi, 0, 0))
    return pl.pallas_call(
        functools.partial(_cumsum_kernel, ch=ch), grid=(b,), in_specs=[spec], out_specs=spec,
        out_shape=jax.ShapeDtypeStruct((b, h, t), F32),
        compiler_params=_params(("arbitrary",), 16 << 20), name="cumsum",
    )(logft)


def _kvx_kernel(ckv_ref, kr_ref, wk_ref, place_ref, wv_ref, kp_ref, vm_ref):
    cb = ckv_ref[0].astype(BF16)
    kp = _dot(cb, wk_ref[...]) + _dot(kr_ref[0].astype(BF16), place_ref[...])
    kp_ref[0] = kp.astype(BF16)
    vm_ref[0] = _dot(cb, wv_ref[...]).astype(BF16)


def _kv_expand(ckv, kr, wk, place, wv):
    b, t, _ = ckv.shape
    tm = _pick_tile(t, (512, 384, 256, 128))
    kw = N_PAIRS * MLA_QK_W

    def tok(width):
        return pl.BlockSpec((1, tm, width), lambda bi, ti: (bi, ti, 0))

    return pl.pallas_call(
        _kvx_kernel, grid=(b, t // tm),
        in_specs=[tok(MLA_KV_RANK), tok(MLA_ROPE), _resident(wk.shape), _resident(place.shape), _resident(wv.shape)],
        out_specs=(tok(kw), tok(MLA_W)),
        out_shape=(jax.ShapeDtypeStruct((b, t, kw), BF16), jax.ShapeDtypeStruct((b, t, MLA_W), BF16)),
        compiler_params=_params(("arbitrary", "arbitrary"), 32 << 20), name="kvexpand",
    )(ckv, kr, wk, place, wv)


def _head_lane_mask(lane, hh, width):
    if width == LANES:
        return (lane // FOX_HEAD_DIM) == hh
    nope = (lane < 2 * MLA_NOPE) & ((lane // MLA_NOPE) == hh)
    rope = (lane >= 2 * MLA_NOPE) & (lane < 2 * MLA_NOPE + 2 * MLA_ROPE) & (
        ((lane - 2 * MLA_NOPE) // MLA_ROPE) == hh)
    return nope | rope


def _attn_kernel(*refs, tq, tk, width, q_off, t_valid, chunked, has_bias):
    if has_bias:
        q_ref, k_ref, v_ref, nb_ref, o_ref = refs
    else:
        q_ref, k_ref, v_ref, o_ref = refs
        nb_ref = None
    nkb = k_ref.shape[1] // tk
    q_min = q_off + pl.program_id(2) * tq
    q_max = q_min + tq - 1
    if chunked:
        lim_min = (q_min // CHUNK + 1) * CHUNK
        lim_max = (q_max // CHUNK + 1) * CHUNK
    else:
        lim_min = q_min + 1
        lim_max = q_max + 1
    n_full = jnp.minimum(lim_min, t_valid) // tk
    n_vis = jnp.minimum((jnp.minimum(lim_max, t_valid) + tk - 1) // tk, nkb)

    qpos = q_min + lax.broadcasted_iota(jnp.int32, (tq, 1), 0)
    if chunked:
        row_lim = lax.shift_left(lax.shift_right_logical(qpos, 6) + 1, 6)
    else:
        row_lim = qpos + 1
    row_lim = jnp.minimum(row_lim, t_valid)

    q = q_ref[0]
    lane = lax.broadcasted_iota(jnp.int32, (1, width), 1)
    qh = [jnp.where(_head_lane_mask(lane, hh, width), q, jnp.zeros_like(q)) for hh in range(2)]

    def step(kb, carry, masked):
        k0 = pl.multiple_of(kb * tk, tk)
        kblk = k_ref[0, pl.ds(k0, tk), :]
        vblk = v_ref[0, pl.ds(k0, tk), :]
        out = []
        for hh in range(2):
            m, l, acc = carry[3 * hh:3 * hh + 3]
            s = _dot_nt(qh[hh], kblk)
            if has_bias:
                s = s + nb_ref[0, 0, hh:hh + 1, pl.ds(k0, tk)]
            if masked:
                kpos = k0 + lax.broadcasted_iota(jnp.int32, (1, tk), 1)
                s = jnp.where(kpos < row_lim, s, NEG_INF)
            m_new = jnp.maximum(m, jnp.max(s, axis=-1, keepdims=True))
            p = jnp.exp2(s - m_new)
            alpha = jnp.exp2(m - m_new)
            l = alpha * l + jnp.sum(p, axis=-1, keepdims=True)
            acc = alpha * acc + _dot(p.astype(BF16), vblk)
            out += [m_new, l, acc]
        return tuple(out)

    init = (jnp.full((tq, 1), NEG_INF, F32), jnp.zeros((tq, 1), F32), jnp.zeros((tq, PAIR_W), F32)) * 2
    carry = lax.fori_loop(0, n_full, functools.partial(step, masked=False), init)
    carry = lax.fori_loop(n_full, n_vis, functools.partial(step, masked=True), carry)
    o0 = carry[2] * (1.0 / carry[1])
    o1 = carry[5] * (1.0 / carry[4])
    olane = lax.broadcasted_iota(jnp.int32, (1, PAIR_W), 1)
    o_ref[0] = jnp.where(olane < MLA_V, o0, o1).astype(BF16)


def _attention(q, k, v, nb, *, q_off, t_valid, chunked):
    assert CHUNK == 64
    b, sq, qw = q.shape
    width = qw // N_PAIRS
    tp = k.shape[1]
    tq = _pick_tile(sq, (512, 256, 128))
    tk = _pick_tile(tp, (512, 384, 256, 128))
    in_specs = [pl.BlockSpec((1, tq, width), lambda bi, pi, qi: (bi, qi, pi)),
                pl.BlockSpec((1, tp, width), lambda bi, pi, qi: (bi, 0, pi)),
                pl.BlockSpec((1, tp, PAIR_W), lambda bi, pi, qi: (bi, 0, pi))]
    args = [q, k, v]
    if nb is not None:
        in_specs.append(pl.BlockSpec((1, 1, 2, tp), lambda bi, pi, qi: (bi, pi, 0, 0)))
        args.append(nb)
    kern = functools.partial(_attn_kernel, tq=tq, tk=tk, width=width, q_off=q_off, t_valid=t_valid,
                             chunked=chunked, has_bias=nb is not None)
    return pl.pallas_call(
        kern, grid=(b, N_PAIRS, sq // tq), in_specs=in_specs,
        out_specs=pl.BlockSpec((1, tq, PAIR_W), lambda bi, pi, qi: (bi, qi, pi)),
        out_shape=jax.ShapeDtypeStruct((b, sq, N_PAIRS * PAIR_W), BF16),
        compiler_params=_params(("arbitrary", "arbitrary", "arbitrary"), 40 << 20),
        name="attn_mla" if chunked else "attn_fox",
    )(*args)


def _mix_kernel(x_ref, of_ref, om_ref, wga_ref, wgb_ref, wof_ref, wom_ref, wout_ref, g_ref, b_ref, h_ref, *, alpha):
    x = x_ref[...]
    xb = x.astype(BF16)
    a = _sigmoid(_dot(xb, wga_ref[...])) * _dot(of_ref[...], wof_ref[...])
    bm = _sigmoid(_dot(xb, wgb_ref[...])) * _dot(om_ref[...], wom_ref[...])
    m = _dot((a + bm).astype(BF16), wout_ref[...])
    h_ref[...] = _layer_norm(alpha * x + m, g_ref[...], b_ref[...])


def _mix(x, of, om, w, alpha):
    r, d = x.shape
    tm = _pick_tile(r, (512, 256, 128))
    wga, wgb, wof, wom, wout, g, bb = w

    def row(width):
        return pl.BlockSpec((tm, width), lambda i: (i, 0))

    return pl.pallas_call(
        functools.partial(_mix_kernel, alpha=alpha), grid=(r // tm,),
        in_specs=[row(d), row(FOX_W), row(MLA_W)] + [_resident(a.shape) for a in w],
        out_specs=row(d), out_shape=jax.ShapeDtypeStruct((r, d), F32),
        compiler_params=_params(("arbitrary",), 48 << 20), name="mix",
    )(x, of, om, wga, wgb, wof, wom, wout, g, bb)


def _ple(hb, p_ref, wpg_ref, wpp_ref):
    return _sigmoid(_dot(hb, wpg_ref[...])) * _dot(p_ref[...].astype(BF16), wpp_ref[...])


def _ffn_chunk(n_ff):
    return _pick_tile(n_ff, (1408, 1024, 512, 256, 128))


def _dense_ffn_kernel(h_ref, p_ref, wg_ref, wu_ref, wd_ref, wpg_ref, wpp_ref, g_ref, b_ref, y_ref, *, alpha, fc):
    h = h_ref[...]
    hb = h.astype(BF16)
    n_ff = wg_ref.shape[1]
    acc = jnp.zeros(h.shape, F32)
    for c in range(n_ff // fc):
        sl = slice(c * fc, (c + 1) * fc)
        gt = _dot(hb, wg_ref[:, sl])
        act = (gt * _sigmoid(gt) * _dot(hb, wu_ref[:, sl])).astype(BF16)
        acc = acc + _dot(act, wd_ref[sl, :])
    y = alpha * h + acc + _ple(hb, p_ref, wpg_ref, wpp_ref)
    y_ref[...] = _layer_norm(y, g_ref[...], b_ref[...])


def _dense_ffn(h, p, w, alpha):
    r, d = h.shape
    tm = _pick_tile(r, (512, 256, 128))
    wg, wu, wd, wpg, wpp, g, bb = w

    def row(width):
        return pl.BlockSpec((tm, width), lambda i: (i, 0))

    return pl.pallas_call(
        functools.partial(_dense_ffn_kernel, alpha=alpha, fc=_ffn_chunk(wg.shape[1])), grid=(r // tm,),
        in_specs=[row(d), row(p.shape[1])] + [_resident(a.shape) for a in w],
        out_specs=row(d), out_shape=jax.ShapeDtypeStruct((r, d), F32),
        compiler_params=_params(("arbitrary",), 56 << 20), name="ffn_dense",
    )(h, p, wg, wu, wd, wpg, wpp, g, bb)


MOE_ROW_TILE = 512
ROUTE_LANES = 6


def _route_kernel(h_ref, wr_ref, br_ref, route_ref, hp_ref, counts_ref, carry_ref):
    @pl.when(pl.program_id(0) == 0)
    def _init():
        carry_ref[...] = jnp.zeros(carry_ref.shape, F32)

    hb = h_ref[...].astype(BF16)
    tm, d = hb.shape
    lane = lax.broadcasted_iota(jnp.int32, (1, LANES), 1)
    logits = _dot(hb, wr_ref[...]) + br_ref[...]
    lg = jnp.where(lane < N_EXPERTS, logits, -jnp.inf)
    m1 = jnp.max(lg, axis=-1, keepdims=True)
    i1 = jnp.min(jnp.where(lg == m1, lane, LANES), axis=-1, keepdims=True)
    lg2 = jnp.where(lane == i1, -jnp.inf, lg)
    m2 = jnp.max(lg2, axis=-1, keepdims=True)
    i2 = jnp.min(jnp.where(lg2 == m2, lane, LANES), axis=-1, keepdims=True)
    e2 = jnp.exp(m2 - m1)
    den = 1.0 + e2
    hit1 = lane == i1
    hit2 = lane == i2
    onehot = jnp.where(hit1, 1.0, 0.0) + jnp.where(hit2, 1.0, 0.0)
    rr = lax.broadcasted_iota(jnp.int32, (tm, tm), 0)
    cc = lax.broadcasted_iota(jnp.int32, (tm, tm), 1)
    lower = jnp.where(cc < rr, 1.0, 0.0).astype(BF16)
    prefix = _dot(lower, onehot.astype(BF16)) + carry_ref[...]
    rank1 = jnp.sum(jnp.where(hit1, prefix, 0.0), axis=-1, keepdims=True)
    rank2 = jnp.sum(jnp.where(hit2, prefix, 0.0), axis=-1, keepdims=True)
    fields = (i1.astype(F32), i2.astype(F32), rank1, rank2, 1.0 / den, e2 / den)
    route = jnp.zeros((tm, LANES), F32)
    for j, f in enumerate(fields):
        route = jnp.where(lane == j, f, route)
    route_ref[...] = route
    carry = carry_ref[...] + jnp.sum(onehot, axis=0, keepdims=True)
    carry_ref[...] = carry
    counts_ref[...] = carry
    bits = lax.bitcast_convert_type(hb.astype(F32), jnp.uint32)
    hp_ref[...] = (bits[:, d // 2:] & jnp.uint32(0xFFFF0000)) | lax.shift_right_logical(bits[:, :d // 2], jnp.uint32(16))


def _route(h, wr, br):
    r, d = h.shape
    tm = _pick_tile(r, (512, 256, 128))
    return pl.pallas_call(
        _route_kernel, grid=(r // tm,),
        in_specs=[pl.BlockSpec((tm, d), lambda i: (i, 0)), _resident(wr.shape), _resident(br.shape)],
        out_specs=(pl.BlockSpec((tm, LANES), lambda i: (i, 0)), pl.BlockSpec((tm, d // 2), lambda i: (i, 0)),
                   pl.BlockSpec((1, LANES), lambda i: (0, 0))),
        out_shape=(jax.ShapeDtypeStruct((r, LANES), F32), jax.ShapeDtypeStruct((r, d // 2), jnp.uint32),
                   jax.ShapeDtypeStruct((1, LANES), F32)),
        scratch_shapes=[pltpu.VMEM((1, LANES), F32)],
        compiler_params=_params(("arbitrary",), 32 << 20), name="moe_route",
    )(h, wr, br)


def _row_copy(src_ref, dst_ref, sem, s_row, d_row):
    return pltpu.make_async_copy(src_ref.at[pl.ds(s_row, 1)], dst_ref.at[pl.ds(d_row, 1)], sem)


def _row_scatter_kernel(idx_ref, src_ref, init_ref, dst_ref, sem, *, n):
    del init_ref
    base = pl.program_id(0) * n

    def issue(j, c):
        _row_copy(src_ref, dst_ref, sem, lax.shift_right_logical(base + j, 1), idx_ref[0, 0, j]).start()
        return c

    lax.fori_loop(0, n, issue, 0)

    def drain(j, c):
        _row_copy(src_ref, dst_ref, sem, 0, 0).wait()
        return c

    lax.fori_loop(0, n, drain, 0)


def _row_gather_kernel(idx_ref, src_ref, dst_ref, sem, *, n):
    base = pl.program_id(0) * n

    def issue(j, c):
        _row_copy(src_ref, dst_ref, sem, idx_ref[0, 0, j], base + j).start()
        return c

    lax.fori_loop(0, n, issue, 0)

    def drain(j, c):
        _row_copy(src_ref, dst_ref, sem, 0, 0).wait()
        return c

    lax.fori_loop(0, n, drain, 0)


def _row_scatter(idx3, src, init):
    steps, _, n = idx3.shape
    any_spec = pl.BlockSpec(memory_space=pl.ANY)
    return pl.pallas_call(
        functools.partial(_row_scatter_kernel, n=n), grid=(steps,),
        in_specs=[pl.BlockSpec((1, 1, n), lambda i: (i, 0, 0), memory_space=pltpu.SMEM), any_spec, any_spec],
        out_specs=any_spec, out_shape=jax.ShapeDtypeStruct(init.shape, init.dtype),
        scratch_shapes=[pltpu.SemaphoreType.DMA(())], input_output_aliases={2: 0},
        compiler_params=_params(("arbitrary",), 16 << 20), name="moe_scatter",
    )(idx3, src, init)


def _row_gather(idx3, src):
    steps, _, n = idx3.shape
    any_spec = pl.BlockSpec(memory_space=pl.ANY)
    return pl.pallas_call(
        functools.partial(_row_gather_kernel, n=n), grid=(steps,),
        in_specs=[pl.BlockSpec((1, 1, n), lambda i: (i, 0, 0), memory_space=pltpu.SMEM), any_spec],
        out_specs=any_spec, out_shape=jax.ShapeDtypeStruct((steps * n, src.shape[1]), src.dtype),
        scratch_shapes=[pltpu.SemaphoreType.DMA(())],
        compiler_params=_params(("arbitrary",), 16 << 20), name="moe_gather",
    )(idx3, src)


def _grouped_ffn_kernel(te_ref, tv_ref, x_ref, wg_ref, wu_ref, wd_ref, y_ref, *, fc):
    del te_ref
    valid = tv_ref[pl.program_id(0)] != 0

    @pl.when(valid)
    def _compute():
        pk = x_ref[...]
        lo = lax.bitcast_convert_type(lax.shift_left(pk, jnp.uint32(16)), F32)
        hi = lax.bitcast_convert_type(pk & jnp.uint32(0xFFFF0000), F32)
        xb = jnp.concatenate([lo, hi], axis=1).astype(BF16)
        n_ff = wg_ref.shape[2]
        acc = jnp.zeros(y_ref.shape, F32)
        for c in range(n_ff // fc):
            sl = slice(c * fc, (c + 1) * fc)
            gt = _dot(xb, wg_ref[0, :, sl])
            act = (gt * _sigmoid(gt) * _dot(xb, wu_ref[0, :, sl])).astype(BF16)
            acc = acc + _dot(act, wd_ref[0, sl, :])
        y_ref[...] = acc

    @pl.when(jnp.logical_not(valid))
    def _skip():
        y_ref[...] = jnp.zeros(y_ref.shape, F32)


def _grouped_ffn(tile_expert, tile_valid, xs, wg, wu, wd):
    rp, half = xs.shape
    n_e, d, n_ff = wg.shape
    tmx = MOE_ROW_TILE
    grid_spec = pltpu.PrefetchScalarGridSpec(
        num_scalar_prefetch=2, grid=(rp // tmx,),
        in_specs=[pl.BlockSpec((tmx, half), lambda i, te, tv: (i, 0)),
                  pl.BlockSpec((1, d, n_ff), lambda i, te, tv: (te[i], 0, 0)),
                  pl.BlockSpec((1, d, n_ff), lambda i, te, tv: (te[i], 0, 0)),
                  pl.BlockSpec((1, n_ff, d), lambda i, te, tv: (te[i], 0, 0))],
        out_specs=pl.BlockSpec((tmx, d), lambda i, te, tv: (i, 0)))
    return pl.pallas_call(
        functools.partial(_grouped_ffn_kernel, fc=_ffn_chunk(n_ff)), grid_spec=grid_spec,
        out_shape=jax.ShapeDtypeStruct((rp, d), F32),
        compiler_params=_params(("arbitrary",), V7X_SCOPED_VMEM_BYTES), name="moe_ffn",
    )(tile_expert, tile_valid, xs, wg, wu, wd)


def _moe_out_kernel(h_ref, p_ref, yt_ref, route_ref, wpg_ref, wpp_ref, g_ref, b_ref, y_ref, *, alpha):
    h = h_ref[...]
    d = h.shape[1]
    hb = h.astype(BF16)
    rt = route_ref[...]
    moe = rt[:, 4:5] * yt_ref[:, :d] + rt[:, 5:6] * yt_ref[:, d:]
    y = alpha * h + moe + _ple(hb, p_ref, wpg_ref, wpp_ref)
    y_ref[...] = _layer_norm(y, g_ref[...], b_ref[...])


def _moe_out(h, p, yt, route, w, alpha):
    r, d = h.shape
    tm = _pick_tile(r, (512, 256, 128))

    def row(width):
        return pl.BlockSpec((tm, width), lambda i: (i, 0))

    return pl.pallas_call(
        functools.partial(_moe_out_kernel, alpha=alpha), grid=(r // tm,),
        in_specs=[row(d), row(p.shape[1]), row(2 * d), row(LANES)] + [_resident(a.shape) for a in w],
        out_specs=row(d), out_shape=jax.ShapeDtypeStruct((r, d), F32),
        compiler_params=_params(("arbitrary",), 48 << 20), name="moe_out",
    )(h, p, yt, route, *w)


def _moe_ffn(h, p, w, alpha):
    r, d = h.shape
    wr, br, wg, wu, wd, wpg, wpp, g, bb = w
    tmx = MOE_ROW_TILE
    route, hp, counts = _route(h, wr, br)
    cnt = counts[0, :N_EXPERTS].astype(jnp.int32)
    padded = (cnt + tmx - 1) // tmx * tmx
    ends = jnp.cumsum(padded)
    dest = jnp.take(ends - padded, route[:, 0:2].astype(jnp.int32)) + route[:, 2:4].astype(jnp.int32)
    n_tiles = -(-2 * r // tmx) + N_EXPERTS
    starts = jnp.arange(n_tiles, dtype=jnp.int32) * tmx
    tile_expert = jnp.minimum(jnp.sum(ends[None, :] <= starts[:, None], axis=1), N_EXPERTS - 1).astype(jnp.int32)
    tile_valid = (starts < ends[-1]).astype(jnp.int32)
    n = _pick_tile(2 * r, (2048, 1024, 512, 256))
    dest3 = dest.reshape(2 * r // n, 1, n)
    xs = _row_scatter(dest3, hp, jnp.zeros((n_tiles * tmx, d // 2), jnp.uint32))
    ys = _grouped_ffn(tile_expert, tile_valid, xs, wg, wu, wd)
    yt = _row_gather(dest3, ys).reshape(r, 2 * d)
    return _moe_out(h, p, yt, route, (wpg, wpp, g, bb), alpha)


def _pad_lanes(a, width=LANES):
    return jnp.pad(a, ((0, 0), (0, width - a.shape[1])))


def _rot_half_cols(a):
    half = a.shape[-1] // 2
    return jnp.concatenate([-a[..., half:], a[..., :half]], axis=-1)


def _mixer_weights(w_in, b_fox_f, g_cq, w_qb, g_ckv, w_kvb):
    d = w_in.shape[0]
    sizes = (FOX_W, FOX_W, FOX_W, FOX_HEADS, MLA_Q_RANK, MLA_KV_RANK, MLA_ROPE, d, d)
    cols, start = [], 0
    for n in sizes:
        cols.append(w_in[:, start:start + n])
        start += n
    wfq, wfk, wfv, wff, wcq, wckv, wkr, wga, wgb = cols
    w1 = jnp.concatenate([wfq * (FOX_HEAD_DIM ** -0.5 * LOG2E), wfk, wfv], axis=1).astype(BF16)
    w2 = jnp.concatenate([wcq, wckv, _pad_lanes(wkr), _pad_lanes(_rot_half_cols(wkr)), _pad_lanes(wff)],
                         axis=1).astype(BF16)
    wfft = jnp.pad(wff.T, ((0, 16 - FOX_HEADS), (0, 0))).astype(BF16)
    bf = b_fox_f.reshape(1, FOX_HEADS)
    bft = b_fox_f.reshape(FOX_HEADS, 1)

    qb = w_qb.reshape(MLA_Q_RANK, MLA_HEADS, MLA_NOPE + MLA_ROPE)
    q_nope, q_rope = qb[..., :MLA_NOPE], qb[..., MLA_NOPE:]
    q_rot = _rot_half_cols(q_rope)
    z_pad = jnp.zeros((MLA_Q_RANK, MLA_QK_W - 2 * MLA_NOPE - 2 * MLA_ROPE), F32)
    z_nope = jnp.zeros((MLA_Q_RANK, 2 * MLA_NOPE), F32)
    wqa = jnp.concatenate([jnp.concatenate([q_nope[:, 2 * j], q_nope[:, 2 * j + 1], q_rope[:, 2 * j],
                                            q_rope[:, 2 * j + 1], z_pad], axis=1) for j in range(N_PAIRS)], axis=1)
    wqb = jnp.concatenate([jnp.concatenate([z_nope, q_rot[:, 2 * j], q_rot[:, 2 * j + 1], z_pad], axis=1)
                           for j in range(N_PAIRS)], axis=1)

    kvb = w_kvb.reshape(MLA_KV_RANK, MLA_HEADS, MLA_NOPE + MLA_V)
    k_nope, v_up = kvb[..., :MLA_NOPE], kvb[..., MLA_NOPE:]
    zk = jnp.zeros((MLA_KV_RANK, MLA_QK_W - 2 * MLA_NOPE), F32)
    wk = jnp.concatenate([jnp.concatenate([k_nope[:, 2 * j], k_nope[:, 2 * j + 1], zk], axis=1)
                          for j in range(N_PAIRS)], axis=1).astype(BF16)
    eye = jnp.eye(MLA_ROPE, dtype=F32)
    place = jnp.concatenate([jnp.zeros((MLA_ROPE, 2 * MLA_NOPE), F32), eye, eye,
                             jnp.zeros((MLA_ROPE, MLA_QK_W - 2 * MLA_NOPE - 2 * MLA_ROPE), F32)], axis=1)
    place = jnp.tile(place, (1, N_PAIRS)).astype(BF16)
    wv = v_up.reshape(MLA_KV_RANK, MLA_W).astype(BF16)
    proj_w = (w1, w2, wfft, bf, bft, g_cq.reshape(1, -1), g_ckv.reshape(1, -1), wqa.astype(BF16), wqb.astype(BF16))
    return proj_w, (wk, place, wv), (wga.astype(BF16), wgb.astype(BF16))


def _rope_tables(pos):
    half = MLA_ROPE // 2
    inv = ROPE_THETA ** (-jnp.arange(half, dtype=F32) * 2.0 / MLA_ROPE)
    ang = pos.astype(F32)[:, None] * inv[None, :]
    cos2 = jnp.concatenate([jnp.cos(ang)] * 2, axis=1)
    sin2 = jnp.concatenate([jnp.sin(ang)] * 2, axis=1)
    n = pos.shape[0]
    scale = (MLA_NOPE + MLA_ROPE) ** -0.5 * LOG2E
    pad = jnp.zeros((n, MLA_QK_W - 2 * MLA_NOPE - 2 * MLA_ROPE), F32)
    ctab = jnp.concatenate([jnp.full((n, 2 * MLA_NOPE), scale, F32), scale * cos2, scale * cos2, pad], axis=1)
    stab = jnp.concatenate([jnp.zeros((n, 2 * MLA_NOPE), F32), scale * sin2, scale * sin2, pad], axis=1)
    return jnp.tile(ctab, (1, N_PAIRS)), jnp.tile(stab, (1, N_PAIRS)), cos2, sin2


def _pad_time(a, tp, axis):
    pad = [(0, 0)] * a.ndim
    pad[axis] = (0, tp - a.shape[axis])
    return jnp.pad(a, pad)


def _layer(x, past, p, proj_w, kvx_w, mix_w, ffn_w, is_moe, alpha):
    b, s, d = x.shape
    n_past = 0 if past is None else past[0].shape[1]
    t = n_past + s
    tabs = _rope_tables(n_past + jnp.arange(s))
    fq, fk, fv, fkb, fvb, logf, logft, ckv, kr, qp = _proj(x, proj_w, tabs)
    if past is None:
        tp = t
        k_fox, v_fox, logft_all, ckv_all, kr_all = fkb, fvb, logft, ckv, kr
    else:
        tp = -(-t // LANES) * LANES
        pk, pv, plogf, pckv, pkr = past
        k_fox = _pad_time(jnp.concatenate([pk.reshape(b, n_past, FOX_W).astype(BF16), fkb], axis=1), tp, 1)
        v_fox = _pad_time(jnp.concatenate([pv.reshape(b, n_past, FOX_W).astype(BF16), fvb], axis=1), tp, 1)
        logft_all = _pad_time(jnp.concatenate([jnp.swapaxes(plogf, 1, 2), logft], axis=2), tp, 2)
        ckv_all = _pad_time(jnp.concatenate([pckv, ckv], axis=1), tp, 1)
        kr_all = _pad_time(jnp.concatenate([pkr, kr], axis=1), tp, 1)
    nb = _neg_cumsum(logft_all).reshape(b, N_PAIRS, 2, tp)
    k_mla, v_mla = _kv_expand(ckv_all, kr_all, *kvx_w)
    o_fox = _attention(fq, k_fox, v_fox, nb, q_off=n_past, t_valid=t, chunked=False)
    o_mla = _attention(qp, k_mla, v_mla, None, q_off=n_past, t_valid=t, chunked=True)
    r = b * s
    h = _mix(x.reshape(r, d), o_fox.reshape(r, FOX_W), o_mla.reshape(r, MLA_W), mix_w, alpha)
    ffn = _moe_ffn if is_moe else _dense_ffn
    y = ffn(h, p.reshape(r, -1), ffn_w, alpha).reshape(b, s, d)
    new_rows = (fk.reshape(b, s, FOX_HEADS, FOX_HEAD_DIM), fv.reshape(b, s, FOX_HEADS, FOX_HEAD_DIM), logf, ckv, kr)
    return y, new_rows


def kernel(x_prompt, x_sample, cache_fox_k, cache_fox_v, cache_fox_logf, cache_mla_ckv, cache_mla_krope,
           p_prompt, p_sample, w_in, b_fox_f, g_mla_cq, w_mla_qb, g_mla_ckv, w_mla_kvb, w_o_fox, w_o_mla,
           w_out, ln_mix_g, ln_mix_b, w_ffn_gate, w_ffn_up, w_ffn_down, w_router, b_router, w_moe_gate,
           w_moe_up, w_moe_down, w_ple_proj, w_ple_gate, ln_ffn_g, ln_ffn_b):
    depth = w_in.shape[0]
    alpha = (2 * depth) ** 0.25
    hp, hs = x_prompt, x_sample
    rows_p, rows_s = [], []
    for i in range(depth):
        proj_w, kvx_w, (wga, wgb) = _mixer_weights(w_in[i], b_fox_f[i], g_mla_cq[i], w_mla_qb[i], g_mla_ckv[i],
                                                   w_mla_kvb[i])
        mix_w = (wga, wgb, w_o_fox[i].astype(BF16), w_o_mla[i].astype(BF16), w_out[i].astype(BF16),
                 ln_mix_g[i].reshape(1, -1), ln_mix_b[i].reshape(1, -1))
        tail = (w_ple_gate[i].astype(BF16), w_ple_proj[i].astype(BF16),
                ln_ffn_g[i].reshape(1, -1), ln_ffn_b[i].reshape(1, -1))
        j = i // 2
        is_moe = i % 2 == 1
        if is_moe:
            ffn_w = (_pad_lanes(w_router[j]).astype(BF16), _pad_lanes(b_router[j].reshape(1, -1)),
                     w_moe_gate[j].astype(BF16), w_moe_up[j].astype(BF16), w_moe_down[j].astype(BF16)) + tail
        else:
            ffn_w = (w_ffn_gate[j].astype(BF16), w_ffn_up[j].astype(BF16), w_ffn_down[j].astype(BF16)) + tail
        past = (cache_fox_k[i], cache_fox_v[i], cache_fox_logf[i], cache_mla_ckv[i], cache_mla_krope[i])
        hp, new_p = _layer(hp, None, p_prompt[i], proj_w, kvx_w, mix_w, ffn_w, is_moe, alpha)
        hs, new_s = _layer(hs, past, p_sample[i], proj_w, kvx_w, mix_w, ffn_w, is_moe, alpha)
        rows_p.append(new_p)
        rows_s.append(new_s)

    def stack(rows, idx):
        return jnp.stack([r[idx] for r in rows], axis=0)

    return (hp, hs) + tuple(stack(rows_p, k) for k in range(5)) + tuple(stack(rows_s, k) for k in range(5))
```

```python
import functools

import jax
import jax.numpy as jnp
from jax import lax
from jax.experimental import pallas as pl
from jax.experimental.pallas import tpu as pltpu

CHUNK = 64
FOX_HEADS = 8
FOX_HEAD_DIM = 64
FOX_W = FOX_HEADS * FOX_HEAD_DIM
MLA_HEADS = 8
MLA_Q_RANK = 256
MLA_KV_RANK = 128
MLA_NOPE = 64
MLA_ROPE = 32
MLA_V = 64
MLA_W = MLA_HEADS * MLA_V
ROPE_THETA = 10000.0
N_EXPERTS = 8
LN_EPS = 1e-5
RMS_EPS = 1e-6
NEG_INF = -1e30
LOG2E = 1.4426950408889634

LANES = 128
PAIR_W = 2 * MLA_V
MLA_QK_W = 256
N_PAIRS = FOX_HEADS // 2
V7X_SCOPED_VMEM_BYTES = 60000 * 1024

BF16 = jnp.bfloat16
F32 = jnp.float32


def _dot(a, b):
    return jnp.dot(a, b, preferred_element_type=F32)


def _dot_nt(a, b):
    return lax.dot_general(a, b, (((1,), (1,)), ((), ())), preferred_element_type=F32)


def _sigmoid(x):
    return 1.0 / (1.0 + jnp.exp(-x))


def _log_sigmoid(x):
    return jnp.minimum(x, 0.0) - jnp.log1p(jnp.exp(-jnp.abs(x)))


def _rms_norm(x, g):
    return x * lax.rsqrt(jnp.mean(jnp.square(x), axis=-1, keepdims=True) + RMS_EPS) * g


def _layer_norm(x, g, b):
    mu = jnp.mean(x, axis=-1, keepdims=True)
    xc = x - mu
    var = jnp.mean(jnp.square(xc), axis=-1, keepdims=True)
    return xc * lax.rsqrt(var + LN_EPS) * g + b


def _resident(shape):
    nd = len(shape)
    return pl.BlockSpec(shape, lambda *_: (0,) * nd, pipeline_mode=pl.Buffered(1))


def _params(semantics, vmem_bytes):
    return pltpu.CompilerParams(dimension_semantics=semantics,
                                vmem_limit_bytes=min(int(vmem_bytes), V7X_SCOPED_VMEM_BYTES))


def _pick_tile(n, candidates):
    for c in candidates:
        if n % c == 0:
            return c
    return n


def _proj_kernel(x_ref, w1_ref, w2_ref, wfft_ref, wfvt_ref, bf_ref, bft_ref, gcq_ref, gckv_ref, wqa_ref, wqb_ref,
                 ctab_ref, stab_ref, cos_ref, sin_ref,
                 fq_ref, fk_ref, fv_ref, fkb_ref, fvb_ref, fvt_ref, logf_ref, logft_ref, ckv_ref, kr_ref, qp_ref):
    xb = x_ref[0].astype(BF16)
    z1 = _dot(xb, w1_ref[...])
    fq_ref[0] = z1[:, :FOX_W].astype(BF16)
    fk = z1[:, FOX_W:2 * FOX_W]
    fk_ref[0] = fk
    fkb_ref[0] = fk.astype(BF16)
    fv = z1[:, 2 * FOX_W:]
    fv_ref[0] = fv
    fvb_ref[0] = fv.astype(BF16)
    fvt_ref[0] = _dot_nt(wfvt_ref[...], xb).astype(BF16)

    z2 = _dot(xb, w2_ref[...])
    cq = z2[:, :MLA_Q_RANK]
    o = MLA_Q_RANK
    ckv = z2[:, o:o + MLA_KV_RANK]
    o += MLA_KV_RANK
    kr = z2[:, o:o + MLA_ROPE]
    krr = z2[:, o + LANES:o + LANES + MLA_ROPE]
    ff = z2[:, o + 2 * LANES:o + 2 * LANES + FOX_HEADS]
    logf_ref[0] = _log_sigmoid(ff + bf_ref[...])
    fft = _dot_nt(wfft_ref[...], xb)
    logft_ref[0] = _log_sigmoid(fft[:FOX_HEADS] + bft_ref[...])
    ckv_ref[0] = _rms_norm(ckv, gckv_ref[...])
    kr_ref[0] = kr * cos_ref[...] + krr * sin_ref[...]
    cqn = _rms_norm(cq, gcq_ref[...]).astype(BF16)
    qp = _dot(cqn, wqa_ref[...]) * ctab_ref[...] + _dot(cqn, wqb_ref[...]) * stab_ref[...]
    qp_ref[0] = qp.astype(BF16)


def _proj(x, w, tabs):
    b, s, d = x.shape
    tm = _pick_tile(s, (512, 256, 128))
    ns = s // tm
    w1, w2, wfft, wfvt, bf, bft, gcq, gckv, wqa, wqb = w
    ctab, stab, cos2, sin2 = tabs
    qw = N_PAIRS * MLA_QK_W

    def tok(width):
        return pl.BlockSpec((1, tm, width), lambda si, bi: (bi, si, 0))

    def tok_t(height):
        return pl.BlockSpec((1, height, tm), lambda si, bi: (bi, 0, si))

    def tab(width):
        return pl.BlockSpec((tm, width), lambda si, bi: (si, 0))

    in_specs = [tok(d), _resident(w1.shape), _resident(w2.shape), _resident(wfft.shape), _resident(wfvt.shape),
                _resident(bf.shape), _resident(bft.shape), _resident(gcq.shape), _resident(gckv.shape),
                _resident(wqa.shape), _resident(wqb.shape), tab(qw), tab(qw), tab(MLA_ROPE), tab(MLA_ROPE)]
    out_shape = (
        jax.ShapeDtypeStruct((b, s, FOX_W), BF16),
        jax.ShapeDtypeStruct((b, s, FOX_W), F32),
        jax.ShapeDtypeStruct((b, s, FOX_W), F32),
        jax.ShapeDtypeStruct((b, s, FOX_W), BF16),
        jax.ShapeDtypeStruct((b, s, FOX_W), BF16),
        jax.ShapeDtypeStruct((b, FOX_W, s), BF16),
        jax.ShapeDtypeStruct((b, s, FOX_HEADS), F32),
        jax.ShapeDtypeStruct((b, FOX_HEADS, s), F32),
        jax.ShapeDtypeStruct((b, s, MLA_KV_RANK), F32),
        jax.ShapeDtypeStruct((b, s, MLA_ROPE), F32),
        jax.ShapeDtypeStruct((b, s, qw), BF16),
    )
    out_specs = (tok(FOX_W), tok(FOX_W), tok(FOX_W), tok(FOX_W), tok(FOX_W), tok_t(FOX_W), tok(FOX_HEADS),
                 tok_t(FOX_HEADS), tok(MLA_KV_RANK), tok(MLA_ROPE), tok(qw))
    return pl.pallas_call(
        _proj_kernel, grid=(ns, b), in_specs=in_specs, out_specs=out_specs, out_shape=out_shape,
        compiler_params=_params(("arbitrary", "arbitrary"), 48 << 20), name="proj",
    )(x, w1, w2, wfft, wfvt, bf, bft, gcq, gckv, wqa, wqb, ctab, stab, cos2, sin2)


def _cumsum_kernel(x_ref, o_ref, rep_ref, *, ch):
    t = x_ref.shape[2]
    r = lax.broadcasted_iota(jnp.int32, (ch, ch), 0)
    c = lax.broadcasted_iota(jnp.int32, (ch, ch), 1)
    upper = jnp.where(r <= c, 1.0, 0.0).astype(BF16)
    carry = jnp.zeros((FOX_HEADS, 1), F32)
    zeros = jnp.zeros((FOX_HEADS, ch), F32)
    for ci in range(t // ch):
        xc = x_ref[0, :, ci * ch:(ci + 1) * ch]
        hi = xc.astype(BF16).astype(F32)
        r1 = xc - hi
        mid = r1.astype(BF16).astype(F32)
        lo = (r1 - mid).astype(BF16).astype(F32)
        pieces = jnp.concatenate([hi, mid, lo, zeros], axis=0).astype(BF16)
        pc = _dot(pieces, upper)
        cum = pc[0:8] + pc[8:16] + pc[16:24] + carry
        nb = cum * (-LOG2E)
        o_ref[0, :, ci * ch:(ci + 1) * ch] = nb
        for hh in range(FOX_HEADS):
            rep_ref[0, hh, ci * ch:(ci + 1) * ch, :] = jnp.broadcast_to(nb[hh:hh + 1, :], (LANES, ch)).T
        carry = cum[:, ch - 1:ch]


def _neg_cumsum(logft):
    b, h, t = logft.shape
    ch = 256 if t % 256 == 0 else LANES
    spec = pl.BlockSpec((1, h, t), lambda bi: (bi, 0, 0))
    return pl.pallas_call(
        functools.partial(_cumsum_kernel, ch=ch), grid=(b,), in_specs=[spec],
        out_specs=(spec, pl.BlockSpec((1, h, t, LANES), lambda bi: (bi, 0, 0, 0))),
        out_shape=(jax.ShapeDtypeStruct((b, h, t), F32), jax.ShapeDtypeStruct((b, h, t, LANES), F32)),
        compiler_params=_params(("arbitrary",), 32 << 20), name="cumsum",
    )(logft)


def _kvx_kernel(ckv_ref, kr_ref, wk_ref, place_ref, wv_ref, wvt_ref, kp_ref, vm_ref, vmt_ref):
    cb = ckv_ref[0].astype(BF16)
    kp = _dot(cb, wk_ref[...]) + _dot(kr_ref[0].astype(BF16), place_ref[...])
    kp_ref[0] = kp.astype(BF16)
    vm_ref[0] = _dot(cb, wv_ref[...]).astype(BF16)
    vmt_ref[0] = _dot_nt(wvt_ref[...], cb).astype(BF16)


def _kv_expand(ckv, kr, wk, place, wv, wvt):
    b, t, _ = ckv.shape
    tm = _pick_tile(t, (512, 384, 256, 128))
    kw = N_PAIRS * MLA_QK_W

    def tok(width):
        return pl.BlockSpec((1, tm, width), lambda bi, ti: (bi, ti, 0))

    return pl.pallas_call(
        _kvx_kernel, grid=(b, t // tm),
        in_specs=[tok(MLA_KV_RANK), tok(MLA_ROPE), _resident(wk.shape), _resident(place.shape), _resident(wv.shape),
                  _resident(wvt.shape)],
        out_specs=(tok(kw), tok(MLA_W), pl.BlockSpec((1, MLA_W, tm), lambda bi, ti: (bi, 0, ti))),
        out_shape=(jax.ShapeDtypeStruct((b, t, kw), BF16), jax.ShapeDtypeStruct((b, t, MLA_W), BF16),
                   jax.ShapeDtypeStruct((b, MLA_W, t), BF16)),
        compiler_params=_params(("arbitrary", "arbitrary"), 32 << 20), name="kvexpand",
    )(ckv, kr, wk, place, wv, wvt)


def _head_lane_mask(lane, hh, width):
    if width == LANES:
        return (lane // FOX_HEAD_DIM) == hh
    nope = (lane < 2 * MLA_NOPE) & ((lane // MLA_NOPE) == hh)
    rope = (lane >= 2 * MLA_NOPE) & (lane < 2 * MLA_NOPE + 2 * MLA_ROPE) & (
        ((lane - 2 * MLA_NOPE) // MLA_ROPE) == hh)
    return nope | rope


def _attn_kernel(*refs, tq, tk, width, q_off, t_valid, chunked, has_bias):
    if has_bias:
        q_ref, k_ref, v_ref, nb_ref, o_ref = refs
    else:
        q_ref, k_ref, v_ref, o_ref = refs
        nb_ref = None
    nkb = k_ref.shape[1] // tk
    q_min = q_off + pl.program_id(2) * tq
    q_max = q_min + tq - 1
    if chunked:
        lim_min = (q_min // CHUNK + 1) * CHUNK
        lim_max = (q_max // CHUNK + 1) * CHUNK
    else:
        lim_min = q_min + 1
        lim_max = q_max + 1
    n_full = jnp.minimum(lim_min, t_valid) // tk
    n_vis = jnp.minimum((jnp.minimum(lim_max, t_valid) + tk - 1) // tk, nkb)

    qpos = q_min + lax.broadcasted_iota(jnp.int32, (tq, 1), 0)
    if chunked:
        row_lim = lax.shift_left(lax.shift_right_logical(qpos, 6) + 1, 6)
    else:
        row_lim = qpos + 1
    row_lim = jnp.minimum(row_lim, t_valid)

    q = q_ref[0]
    lane = lax.broadcasted_iota(jnp.int32, (1, width), 1)
    qh = [jnp.where(_head_lane_mask(lane, hh, width), q, jnp.zeros_like(q)) for hh in range(2)]

    def step(kb, carry, masked):
        k0 = pl.multiple_of(kb * tk, tk)
        kblk = k_ref[0, pl.ds(k0, tk), :]
        vblk = v_ref[0, pl.ds(k0, tk), :]
        out = []
        for hh in range(2):
            m, l, acc = carry[3 * hh:3 * hh + 3]
            s = _dot_nt(qh[hh], kblk)
            if has_bias:
                s = s + nb_ref[0, 0, hh:hh + 1, pl.ds(k0, tk)]
            if masked:
                kpos = k0 + lax.broadcasted_iota(jnp.int32, (1, tk), 1)
                s = jnp.where(kpos < row_lim, s, NEG_INF)
            m_new = jnp.maximum(m, jnp.max(s, axis=-1, keepdims=True))
            p = jnp.exp2(s - m_new)
            alpha = jnp.exp2(m - m_new)
            l = alpha * l + jnp.sum(p, axis=-1, keepdims=True)
            acc = alpha * acc + _dot(p.astype(BF16), vblk)
            out += [m_new, l, acc]
        return tuple(out)

    init = (jnp.full((tq, 1), NEG_INF, F32), jnp.zeros((tq, 1), F32), jnp.zeros((tq, PAIR_W), F32)) * 2
    carry = lax.fori_loop(0, n_full, functools.partial(step, masked=False), init)
    carry = lax.fori_loop(n_full, n_vis, functools.partial(step, masked=True), carry)
    o0 = carry[2] * (1.0 / carry[1])
    o1 = carry[5] * (1.0 / carry[4])
    olane = lax.broadcasted_iota(jnp.int32, (1, PAIR_W), 1)
    o_ref[0] = jnp.where(olane < MLA_V, o0, o1).astype(BF16)


def _attention(q, k, v, nb, *, q_off, t_valid, chunked):
    assert CHUNK == 64
    b, sq, qw = q.shape
    width = qw // N_PAIRS
    tp = k.shape[1]
    tq = _pick_tile(sq, (512, 256, 128))
    tk = _pick_tile(tp, (512, 384, 256, 128))
    in_specs = [pl.BlockSpec((1, tq, width), lambda bi, pi, qi: (bi, qi, pi)),
                pl.BlockSpec((1, tp, width), lambda bi, pi, qi: (bi, 0, pi)),
                pl.BlockSpec((1, tp, PAIR_W), lambda bi, pi, qi: (bi, 0, pi))]
    args = [q, k, v]
    if nb is not None:
        in_specs.append(pl.BlockSpec((1, 1, 2, tp), lambda bi, pi, qi: (bi, pi, 0, 0)))
        args.append(nb)
    kern = functools.partial(_attn_kernel, tq=tq, tk=tk, width=width, q_off=q_off, t_valid=t_valid,
                             chunked=chunked, has_bias=nb is not None)
    return pl.pallas_call(
        kern, grid=(b, N_PAIRS, sq // tq), in_specs=in_specs,
        out_specs=pl.BlockSpec((1, tq, PAIR_W), lambda bi, pi, qi: (bi, qi, pi)),
        out_shape=jax.ShapeDtypeStruct((b, sq, N_PAIRS * PAIR_W), BF16),
        compiler_params=_params(("arbitrary", "arbitrary", "arbitrary"), 40 << 20),
        name="attn_mla" if chunked else "attn_fox",
    )(*args)


def _attn_t_kernel(*refs, tq, tk, width, q_off, t_valid, chunked, has_bias):
    if has_bias:
        q_ref, k_ref, vt_ref, nb_ref, o_ref, sa_ref, sb_ref = refs
    else:
        q_ref, k_ref, vt_ref, o_ref, sa_ref, sb_ref = refs
        nb_ref = None
    nkb = k_ref.shape[1] // tk
    q_min = q_off + pl.program_id(2) * tq
    q_max = q_min + tq - 1
    if chunked:
        lim_min = (q_min // CHUNK + 1) * CHUNK
        lim_max = (q_max // CHUNK + 1) * CHUNK
    else:
        lim_min = q_min + 1
        lim_max = q_max + 1
    n_full = jnp.minimum(lim_min, t_valid) // tk
    n_vis = jnp.minimum((jnp.minimum(lim_max, t_valid) + tk - 1) // tk, nkb)

    qpos = q_min + lax.broadcasted_iota(jnp.int32, (1, tq), 1)
    if chunked:
        col_lim = lax.shift_left(lax.shift_right_logical(qpos, 6) + 1, 6)
    else:
        col_lim = qpos + 1
    col_lim = jnp.minimum(col_lim, t_valid)

    q = q_ref[0]
    lane = lax.broadcasted_iota(jnp.int32, (1, width), 1)
    qh = [jnp.where(_head_lane_mask(lane, hh, width), q, jnp.zeros_like(q)) for hh in range(2)]

    def block_start(kb):
        return pl.multiple_of(jnp.minimum(kb, nkb - 1) * tk, tk)

    def scores_into(s_ref, kb):
        kblk = k_ref[0, pl.ds(block_start(kb), tk), :]
        for hh in range(2):
            s_ref[hh] = _dot_nt(kblk, qh[hh])

    def consume(s_ref, kb, stats, masked):
        k0 = block_start(kb)
        vtb = vt_ref[0, :, pl.ds(k0, tk)]
        out = []
        for hh in range(2):
            m, l, acc = stats[3 * hh:3 * hh + 3]
            s = s_ref[hh]
            if has_bias:
                nb = nb_ref[0, hh, pl.ds(k0, tk), :]
                s = s + jnp.concatenate([nb] * (tq // LANES), axis=1)
            if masked:
                kpos = kb * tk + lax.broadcasted_iota(jnp.int32, (tk, 1), 0)
                s = jnp.where(kpos < col_lim, s, NEG_INF)
            m_new = jnp.maximum(m, jnp.max(s, axis=0, keepdims=True))
            p = jnp.exp2(s - m_new)
            alpha = jnp.exp2(m - m_new)
            l = alpha * l + jnp.sum(p, axis=0, keepdims=True)
            acc = alpha * acc + _dot(vtb, p.astype(BF16))
            out += [m_new, l, acc]
        return tuple(out)

    def pair(j, stats, masked):
        kb = 2 * j
        scores_into(sb_ref, kb + 1)
        stats = consume(sa_ref, kb, stats, masked)
        scores_into(sa_ref, kb + 2)
        return consume(sb_ref, kb + 1, stats, masked)

    scores_into(sa_ref, 0)
    init = (jnp.full((1, tq), NEG_INF, F32), jnp.zeros((1, tq), F32), jnp.zeros((PAIR_W, tq), F32)) * 2
    n_pairs = n_vis // 2
    carry = lax.fori_loop(0, n_full // 2, functools.partial(pair, masked=False), init)
    carry = lax.fori_loop(n_full // 2, n_pairs, functools.partial(pair, masked=True), carry)
    carry = lax.cond(n_vis % 2 == 1, lambda st: consume(sa_ref, 2 * n_pairs, st, True), lambda st: st, carry)
    o0 = carry[2] * (1.0 / carry[1])
    o1 = carry[5] * (1.0 / carry[4])
    orow = lax.broadcasted_iota(jnp.int32, (PAIR_W, 1), 0)
    o_ref[0] = jnp.where(orow < MLA_V, o0, o1).T.astype(BF16)


def _attention_t(q, k, vt, nbrep, *, q_off, t_valid, chunked):
    assert CHUNK == 64
    b, sq, qw = q.shape
    width = qw // N_PAIRS
    tp = k.shape[1]
    tq = _pick_tile(sq, (512, 256, 128))
    tk = _pick_tile(tp, (512, 384, 256, 128))
    in_specs = [pl.BlockSpec((1, tq, width), lambda bi, pi, qi: (bi, qi, pi)),
                pl.BlockSpec((1, tp, width), lambda bi, pi, qi: (bi, 0, pi)),
                pl.BlockSpec((1, PAIR_W, tp), lambda bi, pi, qi: (bi, pi, 0))]
    args = [q, k, vt]
    if nbrep is not None:
        in_specs.append(pl.BlockSpec((1, 2, tp, LANES), lambda bi, pi, qi: (bi, pi, 0, 0)))
        args.append(nbrep)
    kern = functools.partial(_attn_t_kernel, tq=tq, tk=tk, width=width, q_off=q_off, t_valid=t_valid,
                             chunked=chunked, has_bias=nbrep is not None)
    return pl.pallas_call(
        kern, grid=(b, N_PAIRS, sq // tq), in_specs=in_specs,
        out_specs=pl.BlockSpec((1, tq, PAIR_W), lambda bi, pi, qi: (bi, qi, pi)),
        out_shape=jax.ShapeDtypeStruct((b, sq, N_PAIRS * PAIR_W), BF16),
        scratch_shapes=[pltpu.VMEM((2, tk, tq), F32), pltpu.VMEM((2, tk, tq), F32)],
        compiler_params=_params(("arbitrary", "arbitrary", "arbitrary"), 40 << 20),
        name="attn_t_mla" if chunked else "attn_t_fox",
    )(*args)


def _mix_kernel(x_ref, of_ref, om_ref, wga_ref, wgb_ref, wof_ref, wom_ref, wout_ref, g_ref, b_ref, h_ref, *, alpha):
    x = x_ref[...]
    xb = x.astype(BF16)
    a = _sigmoid(_dot(xb, wga_ref[...])) * _dot(of_ref[...], wof_ref[...])
    bm = _sigmoid(_dot(xb, wgb_ref[...])) * _dot(om_ref[...], wom_ref[...])
    m = _dot((a + bm).astype(BF16), wout_ref[...])
    h_ref[...] = _layer_norm(alpha * x + m, g_ref[...], b_ref[...])


def _mix(x, of, om, w, alpha):
    r, d = x.shape
    tm = _pick_tile(r, (512, 256, 128))
    wga, wgb, wof, wom, wout, g, bb = w

    def row(width):
        return pl.BlockSpec((tm, width), lambda i: (i, 0))

    return pl.pallas_call(
        functools.partial(_mix_kernel, alpha=alpha), grid=(r // tm,),
        in_specs=[row(d), row(FOX_W), row(MLA_W)] + [_resident(a.shape) for a in w],
        out_specs=row(d), out_shape=jax.ShapeDtypeStruct((r, d), F32),
        compiler_params=_params(("arbitrary",), 48 << 20), name="mix",
    )(x, of, om, wga, wgb, wof, wom, wout, g, bb)


def _ple(hb, p_ref, wpg_ref, wpp_ref):
    return _sigmoid(_dot(hb, wpg_ref[...])) * _dot(p_ref[...].astype(BF16), wpp_ref[...])


def _ffn_chunk(n_ff):
    return _pick_tile(n_ff, (1408, 1024, 512, 256, 128))


def _dense_ffn_kernel(h_ref, p_ref, wg_ref, wu_ref, wd_ref, wpg_ref, wpp_ref, g_ref, b_ref, y_ref, *, alpha, fc):
    h = h_ref[...]
    hb = h.astype(BF16)
    n_ff = wg_ref.shape[1]
    acc = jnp.zeros(h.shape, F32)
    for c in range(n_ff // fc):
        sl = slice(c * fc, (c + 1) * fc)
        gt = _dot(hb, wg_ref[:, sl])
        act = (gt * _sigmoid(gt) * _dot(hb, wu_ref[:, sl])).astype(BF16)
        acc = acc + _dot(act, wd_ref[sl, :])
    y = alpha * h + acc + _ple(hb, p_ref, wpg_ref, wpp_ref)
    y_ref[...] = _layer_norm(y, g_ref[...], b_ref[...])


def _dense_ffn(h, p, w, alpha):
    r, d = h.shape
    tm = _pick_tile(r, (512, 256, 128))
    wg, wu, wd, wpg, wpp, g, bb = w

    def row(width):
        return pl.BlockSpec((tm, width), lambda i: (i, 0))

    return pl.pallas_call(
        functools.partial(_dense_ffn_kernel, alpha=alpha, fc=_ffn_chunk(wg.shape[1])), grid=(r // tm,),
        in_specs=[row(d), row(p.shape[1])] + [_resident(a.shape) for a in w],
        out_specs=row(d), out_shape=jax.ShapeDtypeStruct((r, d), F32),
        compiler_params=_params(("arbitrary",), 56 << 20), name="ffn_dense",
    )(h, p, wg, wu, wd, wpg, wpp, g, bb)


MOE_ROW_TILE = 512
ROUTE_LANES = 6


def _route_kernel(h_ref, wr_ref, br_ref, route_ref, hp_ref, counts_ref, carry_ref):
    @pl.when(pl.program_id(0) == 0)
    def _init():
        carry_ref[...] = jnp.zeros(carry_ref.shape, F32)

    hb = h_ref[...].astype(BF16)
    tm, d = hb.shape
    lane = lax.broadcasted_iota(jnp.int32, (1, LANES), 1)
    logits = _dot(hb, wr_ref[...]) + br_ref[...]
    lg = jnp.where(lane < N_EXPERTS, logits, -jnp.inf)
    m1 = jnp.max(lg, axis=-1, keepdims=True)
    i1 = jnp.min(jnp.where(lg == m1, lane, LANES), axis=-1, keepdims=True)
    lg2 = jnp.where(lane == i1, -jnp.inf, lg)
    m2 = jnp.max(lg2, axis=-1, keepdims=True)
    i2 = jnp.min(jnp.where(lg2 == m2, lane, LANES), axis=-1, keepdims=True)
    e2 = jnp.exp(m2 - m1)
    den = 1.0 + e2
    hit1 = lane == i1
    hit2 = lane == i2
    onehot = jnp.where(hit1, 1.0, 0.0) + jnp.where(hit2, 1.0, 0.0)
    rr = lax.broadcasted_iota(jnp.int32, (tm, tm), 0)
    cc = lax.broadcasted_iota(jnp.int32, (tm, tm), 1)
    lower = jnp.where(cc < rr, 1.0, 0.0).astype(BF16)
    prefix = _dot(lower, onehot.astype(BF16)) + carry_ref[...]
    rank1 = jnp.sum(jnp.where(hit1, prefix, 0.0), axis=-1, keepdims=True)
    rank2 = jnp.sum(jnp.where(hit2, prefix, 0.0), axis=-1, keepdims=True)
    fields = (i1.astype(F32), i2.astype(F32), rank1, rank2, 1.0 / den, e2 / den)
    route = jnp.zeros((tm, LANES), F32)
    for j, f in enumerate(fields):
        route = jnp.where(lane == j, f, route)
    route_ref[...] = route
    carry = carry_ref[...] + jnp.sum(onehot, axis=0, keepdims=True)
    carry_ref[...] = carry
    counts_ref[...] = carry
    bits = lax.bitcast_convert_type(hb.astype(F32), jnp.uint32)
    hp_ref[...] = (bits[:, d // 2:] & jnp.uint32(0xFFFF0000)) | lax.shift_right_logical(bits[:, :d // 2], jnp.uint32(16))


def _route(h, wr, br):
    r, d = h.shape
    tm = _pick_tile(r, (512, 256, 128))
    return pl.pallas_call(
        _route_kernel, grid=(r // tm,),
        in_specs=[pl.BlockSpec((tm, d), lambda i: (i, 0)), _resident(wr.shape), _resident(br.shape)],
        out_specs=(pl.BlockSpec((tm, LANES), lambda i: (i, 0)), pl.BlockSpec((tm, d // 2), lambda i: (i, 0)),
                   pl.BlockSpec((1, LANES), lambda i: (0, 0))),
        out_shape=(jax.ShapeDtypeStruct((r, LANES), F32), jax.ShapeDtypeStruct((r, d // 2), jnp.uint32),
                   jax.ShapeDtypeStruct((1, LANES), F32)),
        scratch_shapes=[pltpu.VMEM((1, LANES), F32)],
        compiler_params=_params(("arbitrary",), 32 << 20), name="moe_route",
    )(h, wr, br)


ROW_DMA_UNROLL = 8


def _row_scatter_kernel(idx_ref, src_ref, init_ref, dst_ref, sem, *, tm):
    del init_ref

    def copy(t, d_row):
        return pltpu.make_async_copy(src_ref.at[pl.ds(t, 1)], dst_ref.at[pl.ds(d_row, 1)], sem)

    def issue(t, c):
        copy(t, idx_ref[0, 0, 2 * t]).start()
        copy(t, idx_ref[0, 0, 2 * t + 1]).start()
        return c

    lax.fori_loop(0, tm, issue, 0, unroll=ROW_DMA_UNROLL)

    def drain(t, c):
        copy(0, 0).wait()
        copy(0, 0).wait()
        return c

    lax.fori_loop(0, tm, drain, 0, unroll=ROW_DMA_UNROLL)


def _row_scatter(idx3, src, init):
    steps, _, n = idx3.shape
    tm = n // 2
    any_spec = pl.BlockSpec(memory_space=pl.ANY)
    return pl.pallas_call(
        functools.partial(_row_scatter_kernel, tm=tm), grid=(steps,),
        in_specs=[pl.BlockSpec((1, 1, n), lambda i: (i, 0, 0), memory_space=pltpu.SMEM),
                  pl.BlockSpec((tm, src.shape[1]), lambda i: (i, 0)), any_spec],
        out_specs=any_spec, out_shape=jax.ShapeDtypeStruct(init.shape, init.dtype),
        scratch_shapes=[pltpu.SemaphoreType.DMA(())], input_output_aliases={2: 0},
        compiler_params=_params(("arbitrary",), 16 << 20), name="moe_scatter",
    )(idx3, src, init)


def _grouped_ffn_kernel(te_ref, tv_ref, x_ref, wg_ref, wu_ref, wd_ref, y_ref, *, fc):
    del te_ref
    valid = tv_ref[pl.program_id(0)] != 0

    @pl.when(valid)
    def _compute():
        pk = x_ref[...]
        lo = lax.bitcast_convert_type(lax.shift_left(pk, jnp.uint32(16)), F32)
        hi = lax.bitcast_convert_type(pk & jnp.uint32(0xFFFF0000), F32)
        xb = jnp.concatenate([lo, hi], axis=1).astype(BF16)
        n_ff = wg_ref.shape[2]
        acc = jnp.zeros(y_ref.shape, F32)
        for c in range(n_ff // fc):
            sl = slice(c * fc, (c + 1) * fc)
            gt = _dot(xb, wg_ref[0, :, sl])
            act = (gt * _sigmoid(gt) * _dot(xb, wu_ref[0, :, sl])).astype(BF16)
            acc = acc + _dot(act, wd_ref[0, sl, :])
        y_ref[...] = acc

    @pl.when(jnp.logical_not(valid))
    def _skip():
        y_ref[...] = jnp.zeros(y_ref.shape, F32)


def _grouped_ffn(tile_expert, tile_valid, xs, wg, wu, wd):
    rp, half = xs.shape
    n_e, d, n_ff = wg.shape
    tmx = MOE_ROW_TILE
    grid_spec = pltpu.PrefetchScalarGridSpec(
        num_scalar_prefetch=2, grid=(rp // tmx,),
        in_specs=[pl.BlockSpec((tmx, half), lambda i, te, tv: (i, 0)),
                  pl.BlockSpec((1, d, n_ff), lambda i, te, tv: (te[i], 0, 0)),
                  pl.BlockSpec((1, d, n_ff), lambda i, te, tv: (te[i], 0, 0)),
                  pl.BlockSpec((1, n_ff, d), lambda i, te, tv: (te[i], 0, 0))],
        out_specs=pl.BlockSpec((tmx, d), lambda i, te, tv: (i, 0)))
    return pl.pallas_call(
        functools.partial(_grouped_ffn_kernel, fc=_ffn_chunk(n_ff)), grid_spec=grid_spec,
        out_shape=jax.ShapeDtypeStruct((rp, d), F32),
        compiler_params=_params(("arbitrary",), V7X_SCOPED_VMEM_BYTES), name="moe_ffn",
    )(tile_expert, tile_valid, xs, wg, wu, wd)


def _moe_out_kernel(idx_ref, h_ref, p_ref, route_ref, ys_ref, wpg_ref, wpp_ref, g_ref, b_ref, y_ref,
                    buf_ref, sem, *, alpha):
    tm = h_ref.shape[0]

    def copy(k, t, s_row):
        return pltpu.make_async_copy(ys_ref.at[pl.ds(s_row, 1)], buf_ref.at[k, pl.ds(t, 1)], sem)

    def issue(t, c):
        copy(0, t, idx_ref[0, 0, 2 * t]).start()
        copy(1, t, idx_ref[0, 0, 2 * t + 1]).start()
        return c

    lax.fori_loop(0, tm, issue, 0, unroll=ROW_DMA_UNROLL)
    h = h_ref[...]
    hb = h.astype(BF16)
    ple = _ple(hb, p_ref, wpg_ref, wpp_ref)

    def drain(t, c):
        copy(0, 0, 0).wait()
        copy(1, 0, 0).wait()
        return c

    lax.fori_loop(0, tm, drain, 0, unroll=ROW_DMA_UNROLL)
    rt = route_ref[...]
    moe = rt[:, 4:5] * buf_ref[0] + rt[:, 5:6] * buf_ref[1]
    y_ref[...] = _layer_norm(alpha * h + moe + ple, g_ref[...], b_ref[...])


def _moe_out(idx3, h, p, route, ys, w, alpha):
    r, d = h.shape
    steps, _, n = idx3.shape
    tm = n // 2

    def row(width):
        return pl.BlockSpec((tm, width), lambda i: (i, 0))

    return pl.pallas_call(
        functools.partial(_moe_out_kernel, alpha=alpha), grid=(steps,),
        in_specs=[pl.BlockSpec((1, 1, n), lambda i: (i, 0, 0), memory_space=pltpu.SMEM),
                  row(d), row(p.shape[1]), row(LANES), pl.BlockSpec(memory_space=pl.ANY)]
        + [_resident(a.shape) for a in w],
        out_specs=row(d), out_shape=jax.ShapeDtypeStruct((r, d), F32),
        scratch_shapes=[pltpu.VMEM((2, tm, d), F32), pltpu.SemaphoreType.DMA(())],
        compiler_params=_params(("arbitrary",), 48 << 20), name="moe_out",
    )(idx3, h, p, route, ys, *w)


def _moe_ffn(h, p, w, alpha):
    r, d = h.shape
    wr, br, wg, wu, wd, wpg, wpp, g, bb = w
    tmx = MOE_ROW_TILE
    route, hp, counts = _route(h, wr, br)
    cnt = counts[0, :N_EXPERTS].astype(jnp.int32)
    padded = (cnt + tmx - 1) // tmx * tmx
    ends = jnp.cumsum(padded)
    dest = jnp.take(ends - padded, route[:, 0:2].astype(jnp.int32)) + route[:, 2:4].astype(jnp.int32)
    n_tiles = -(-2 * r // tmx) + N_EXPERTS
    starts = jnp.arange(n_tiles, dtype=jnp.int32) * tmx
    tile_expert = jnp.minimum(jnp.sum(ends[None, :] <= starts[:, None], axis=1), N_EXPERTS - 1).astype(jnp.int32)
    tile_valid = (starts < ends[-1]).astype(jnp.int32)
    tm = _pick_tile(r, (512, 256, 128))
    dest3 = dest.reshape(r // tm, 1, 2 * tm)
    xs = _row_scatter(dest3, hp, jnp.zeros((n_tiles * tmx, d // 2), jnp.uint32))
    ys = _grouped_ffn(tile_expert, tile_valid, xs, wg, wu, wd)
    return _moe_out(dest3, h, p, route, ys, (wpg, wpp, g, bb), alpha)


def _pad_lanes(a, width=LANES):
    return jnp.pad(a, ((0, 0), (0, width - a.shape[1])))


def _rot_half_cols(a):
    half = a.shape[-1] // 2
    return jnp.concatenate([-a[..., half:], a[..., :half]], axis=-1)


def _mixer_weights(w_in, b_fox_f, g_cq, w_qb, g_ckv, w_kvb):
    d = w_in.shape[0]
    sizes = (FOX_W, FOX_W, FOX_W, FOX_HEADS, MLA_Q_RANK, MLA_KV_RANK, MLA_ROPE, d, d)
    cols, start = [], 0
    for n in sizes:
        cols.append(w_in[:, start:start + n])
        start += n
    wfq, wfk, wfv, wff, wcq, wckv, wkr, wga, wgb = cols
    w1 = jnp.concatenate([wfq * (FOX_HEAD_DIM ** -0.5 * LOG2E), wfk, wfv], axis=1).astype(BF16)
    w2 = jnp.concatenate([wcq, wckv, _pad_lanes(wkr), _pad_lanes(_rot_half_cols(wkr)), _pad_lanes(wff)],
                         axis=1).astype(BF16)
    wfft = jnp.pad(wff.T, ((0, 16 - FOX_HEADS), (0, 0))).astype(BF16)
    bf = b_fox_f.reshape(1, FOX_HEADS)
    bft = b_fox_f.reshape(FOX_HEADS, 1)

    qb = w_qb.reshape(MLA_Q_RANK, MLA_HEADS, MLA_NOPE + MLA_ROPE)
    q_nope, q_rope = qb[..., :MLA_NOPE], qb[..., MLA_NOPE:]
    q_rot = _rot_half_cols(q_rope)
    z_pad = jnp.zeros((MLA_Q_RANK, MLA_QK_W - 2 * MLA_NOPE - 2 * MLA_ROPE), F32)
    z_nope = jnp.zeros((MLA_Q_RANK, 2 * MLA_NOPE), F32)
    wqa = jnp.concatenate([jnp.concatenate([q_nope[:, 2 * j], q_nope[:, 2 * j + 1], q_rope[:, 2 * j],
                                            q_rope[:, 2 * j + 1], z_pad], axis=1) for j in range(N_PAIRS)], axis=1)
    wqb = jnp.concatenate([jnp.concatenate([z_nope, q_rot[:, 2 * j], q_rot[:, 2 * j + 1], z_pad], axis=1)
                           for j in range(N_PAIRS)], axis=1)

    kvb = w_kvb.reshape(MLA_KV_RANK, MLA_HEADS, MLA_NOPE + MLA_V)
    k_nope, v_up = kvb[..., :MLA_NOPE], kvb[..., MLA_NOPE:]
    zk = jnp.zeros((MLA_KV_RANK, MLA_QK_W - 2 * MLA_NOPE), F32)
    wk = jnp.concatenate([jnp.concatenate([k_nope[:, 2 * j], k_nope[:, 2 * j + 1], zk], axis=1)
                          for j in range(N_PAIRS)], axis=1).astype(BF16)
    eye = jnp.eye(MLA_ROPE, dtype=F32)
    place = jnp.concatenate([jnp.zeros((MLA_ROPE, 2 * MLA_NOPE), F32), eye, eye,
                             jnp.zeros((MLA_ROPE, MLA_QK_W - 2 * MLA_NOPE - 2 * MLA_ROPE), F32)], axis=1)
    place = jnp.tile(place, (1, N_PAIRS)).astype(BF16)
    wv = v_up.reshape(MLA_KV_RANK, MLA_W).astype(BF16)
    proj_w = (w1, w2, wfft, wfv.T.astype(BF16), bf, bft, g_cq.reshape(1, -1), g_ckv.reshape(1, -1),
              wqa.astype(BF16), wqb.astype(BF16))
    return proj_w, (wk, place, wv, wv.T), (wga.astype(BF16), wgb.astype(BF16))


def _rope_tables(pos):
    half = MLA_ROPE // 2
    inv = ROPE_THETA ** (-jnp.arange(half, dtype=F32) * 2.0 / MLA_ROPE)
    ang = pos.astype(F32)[:, None] * inv[None, :]
    cos2 = jnp.concatenate([jnp.cos(ang)] * 2, axis=1)
    sin2 = jnp.concatenate([jnp.sin(ang)] * 2, axis=1)
    n = pos.shape[0]
    scale = (MLA_NOPE + MLA_ROPE) ** -0.5 * LOG2E
    pad = jnp.zeros((n, MLA_QK_W - 2 * MLA_NOPE - 2 * MLA_ROPE), F32)
    ctab = jnp.concatenate([jnp.full((n, 2 * MLA_NOPE), scale, F32), scale * cos2, scale * cos2, pad], axis=1)
    stab = jnp.concatenate([jnp.zeros((n, 2 * MLA_NOPE), F32), scale * sin2, scale * sin2, pad], axis=1)
    return jnp.tile(ctab, (1, N_PAIRS)), jnp.tile(stab, (1, N_PAIRS)), cos2, sin2


def _pad_time(a, tp, axis):
    pad = [(0, 0)] * a.ndim
    pad[axis] = (0, tp - a.shape[axis])
    return jnp.pad(a, pad)


def _layer(x, past, p, proj_w, kvx_w, mix_w, ffn_w, is_moe, alpha):
    b, s, d = x.shape
    n_past = 0 if past is None else past[0].shape[1]
    t = n_past + s
    tabs = _rope_tables(n_past + jnp.arange(s))
    fq, fk, fv, fkb, fvb, fvt, logf, logft, ckv, kr, qp = _proj(x, proj_w, tabs)
    if past is None:
        tp = t
        k_fox, logft_all, ckv_all, kr_all = fkb, logft, ckv, kr
    else:
        tp = -(-t // LANES) * LANES
        pk, pv, plogf, pckv, pkr = past
        k_fox = _pad_time(jnp.concatenate([pk.reshape(b, n_past, FOX_W).astype(BF16), fkb], axis=1), tp, 1)
        v_fox = _pad_time(jnp.concatenate([pv.reshape(b, n_past, FOX_W).astype(BF16), fvb], axis=1), tp, 1)
        logft_all = _pad_time(jnp.concatenate([jnp.swapaxes(plogf, 1, 2), logft], axis=2), tp, 2)
        ckv_all = _pad_time(jnp.concatenate([pckv, ckv], axis=1), tp, 1)
        kr_all = _pad_time(jnp.concatenate([pkr, kr], axis=1), tp, 1)
    nb, nbrep = _neg_cumsum(logft_all)
    k_mla, v_mla, vt_mla = _kv_expand(ckv_all, kr_all, *kvx_w)
    if past is None:
        o_fox = _attention_t(fq, k_fox, fvt, nbrep, q_off=0, t_valid=t, chunked=False)
        o_mla = _attention_t(qp, k_mla, vt_mla, None, q_off=0, t_valid=t, chunked=True)
    else:
        nb = nb.reshape(b, N_PAIRS, 2, tp)
        o_fox = _attention(fq, k_fox, v_fox, nb, q_off=n_past, t_valid=t, chunked=False)
        o_mla = _attention(qp, k_mla, v_mla, None, q_off=n_past, t_valid=t, chunked=True)
    r = b * s
    h = _mix(x.reshape(r, d), o_fox.reshape(r, FOX_W), o_mla.reshape(r, MLA_W), mix_w, alpha)
    ffn = _moe_ffn if is_moe else _dense_ffn
    y = ffn(h, p.reshape(r, -1), ffn_w, alpha).reshape(b, s, d)
    new_rows = (fk.reshape(b, s, FOX_HEADS, FOX_HEAD_DIM), fv.reshape(b, s, FOX_HEADS, FOX_HEAD_DIM), logf, ckv, kr)
    return y, new_rows


def kernel(x_prompt, x_sample, cache_fox_k, cache_fox_v, cache_fox_logf, cache_mla_ckv, cache_mla_krope,
           p_prompt, p_sample, w_in, b_fox_f, g_mla_cq, w_mla_qb, g_mla_ckv, w_mla_kvb, w_o_fox, w_o_mla,
           w_out, ln_mix_g, ln_mix_b, w_ffn_gate, w_ffn_up, w_ffn_down, w_router, b_router, w_moe_gate,
           w_moe_up, w_moe_down, w_ple_proj, w_ple_gate, ln_ffn_g, ln_ffn_b):
    depth = w_in.shape[0]
    alpha = (2 * depth) ** 0.25
    hp, hs = x_prompt, x_sample
    rows_p, rows_s = [], []
    for i in range(depth):
        proj_w, kvx_w, (wga, wgb) = _mixer_weights(w_in[i], b_fox_f[i], g_mla_cq[i], w_mla_qb[i], g_mla_ckv[i],
                                                   w_mla_kvb[i])
        mix_w = (wga, wgb, w_o_fox[i].astype(BF16), w_o_mla[i].astype(BF16), w_out[i].astype(BF16),
                 ln_mix_g[i].reshape(1, -1), ln_mix_b[i].reshape(1, -1))
        tail = (w_ple_gate[i].astype(BF16), w_ple_proj[i].astype(BF16),
                ln_ffn_g[i].reshape(1, -1), ln_ffn_b[i].reshape(1, -1))
        j = i // 2
        is_moe = i % 2 == 1
        if is_moe:
            ffn_w = (_pad_lanes(w_router[j]).astype(BF16), _pad_lanes(b_router[j].reshape(1, -1)),
                     w_moe_gate[j].astype(BF16), w_moe_up[j].astype(BF16), w_moe_down[j].astype(BF16)) + tail
        else:
            ffn_w = (w_ffn_gate[j].astype(BF16), w_ffn_up[j].astype(BF16), w_ffn_down[j].astype(BF16)) + tail
        past = (cache_fox_k[i], cache_fox_v[i], cache_fox_logf[i], cache_mla_ckv[i], cache_mla_krope[i])
        hp, new_p = _layer(hp, None, p_prompt[i], proj_w, kvx_w, mix_w, ffn_w, is_moe, alpha)
        hs, new_s = _layer(hs, past, p_sample[i], proj_w, kvx_w, mix_w, ffn_w, is_moe, alpha)
        rows_p.append(new_p)
        rows_s.append(new_s)

    def stack(rows, idx):
        return jnp.stack([r[idx] for r in rows], axis=0)

    return (hp, hs) + tuple(stack(rows_p, k) for k in range(5)) + tuple(stack(rows_s, k) for k in range(5))
```

```python
import functools

import jax
import jax.numpy as jnp
from jax import lax
from jax.experimental import pallas as pl
from jax.experimental.pallas import tpu as pltpu

CHUNK = 64
FOX_HEADS = 8
FOX_HEAD_DIM = 64
FOX_W = FOX_HEADS * FOX_HEAD_DIM
MLA_HEADS = 8
MLA_Q_RANK = 256
MLA_KV_RANK = 128
MLA_NOPE = 64
MLA_ROPE = 32
MLA_V = 64
MLA_W = MLA_HEADS * MLA_V
ROPE_THETA = 10000.0
N_EXPERTS = 8
LN_EPS = 1e-5
RMS_EPS = 1e-6
NEG_INF = -1e30
LOG2E = 1.4426950408889634

LANES = 128
PAIR_W = 2 * MLA_V
MLA_QK_W = 256
N_PAIRS = FOX_HEADS // 2
V7X_SCOPED_VMEM_BYTES = 60000 * 1024

BF16 = jnp.bfloat16
F32 = jnp.float32


def _dot(a, b):
    return jnp.dot(a, b, preferred_element_type=F32)


def _dot_nt(a, b):
    return lax.dot_general(a, b, (((1,), (1,)), ((), ())), preferred_element_type=F32)


def _sigmoid(x):
    return 1.0 / (1.0 + jnp.exp(-x))


def _log_sigmoid(x):
    return jnp.minimum(x, 0.0) - jnp.log1p(jnp.exp(-jnp.abs(x)))


def _rms_norm(x, g):
    return x * lax.rsqrt(jnp.mean(jnp.square(x), axis=-1, keepdims=True) + RMS_EPS) * g


def _layer_norm(x, g, b):
    mu = jnp.mean(x, axis=-1, keepdims=True)
    xc = x - mu
    var = jnp.mean(jnp.square(xc), axis=-1, keepdims=True)
    return xc * lax.rsqrt(var + LN_EPS) * g + b


def _resident(shape):
    nd = len(shape)
    return pl.BlockSpec(shape, lambda *_: (0,) * nd, pipeline_mode=pl.Buffered(1))


def _params(semantics, vmem_bytes):
    return pltpu.CompilerParams(dimension_semantics=semantics,
                                vmem_limit_bytes=min(int(vmem_bytes), V7X_SCOPED_VMEM_BYTES))


def _pick_tile(n, candidates):
    for c in candidates:
        if n % c == 0:
            return c
    return n


def _proj_kernel(x_ref, w1_ref, w2_ref, wfft_ref, wfvt_ref, bf_ref, bft_ref, gcq_ref, gckv_ref, wqa_ref, wqb_ref,
                 ctab_ref, stab_ref, cos_ref, sin_ref,
                 fq_ref, fk_ref, fv_ref, fkb_ref, fvb_ref, fvt_ref, logf_ref, logft_ref, ckv_ref, kr_ref, qp_ref):
    xb = x_ref[0].astype(BF16)
    z1 = _dot(xb, w1_ref[...])
    fq_ref[0] = z1[:, :FOX_W].astype(BF16)
    fk = z1[:, FOX_W:2 * FOX_W]
    fk_ref[0] = fk
    fkb_ref[0] = fk.astype(BF16)
    fv = z1[:, 2 * FOX_W:]
    fv_ref[0] = fv
    fvb_ref[0] = fv.astype(BF16)
    fvt_ref[0] = _dot_nt(wfvt_ref[...], xb).astype(BF16)

    z2 = _dot(xb, w2_ref[...])
    cq = z2[:, :MLA_Q_RANK]
    o = MLA_Q_RANK
    ckv = z2[:, o:o + MLA_KV_RANK]
    o += MLA_KV_RANK
    kr = z2[:, o:o + MLA_ROPE]
    krr = z2[:, o + LANES:o + LANES + MLA_ROPE]
    ff = z2[:, o + 2 * LANES:o + 2 * LANES + FOX_HEADS]
    logf_ref[0] = _log_sigmoid(ff + bf_ref[...])
    fft = _dot_nt(wfft_ref[...], xb)
    logft_ref[0] = _log_sigmoid(fft[:FOX_HEADS] + bft_ref[...])
    ckv_ref[0] = _rms_norm(ckv, gckv_ref[...])
    kr_ref[0] = kr * cos_ref[...] + krr * sin_ref[...]
    cqn = _rms_norm(cq, gcq_ref[...]).astype(BF16)
    qp = _dot(cqn, wqa_ref[...]) * ctab_ref[...] + _dot(cqn, wqb_ref[...]) * stab_ref[...]
    qp_ref[0] = qp.astype(BF16)


def _proj(x, w, tabs):
    b, s, d = x.shape
    tm = _pick_tile(s, (512, 256, 128))
    ns = s // tm
    w1, w2, wfft, wfvt, bf, bft, gcq, gckv, wqa, wqb = w
    ctab, stab, cos2, sin2 = tabs
    qw = N_PAIRS * MLA_QK_W

    def tok(width):
        return pl.BlockSpec((1, tm, width), lambda si, bi: (bi, si, 0))

    def tok_t(height):
        return pl.BlockSpec((1, height, tm), lambda si, bi: (bi, 0, si))

    def tab(width):
        return pl.BlockSpec((tm, width), lambda si, bi: (si, 0))

    in_specs = [tok(d), _resident(w1.shape), _resident(w2.shape), _resident(wfft.shape), _resident(wfvt.shape),
                _resident(bf.shape), _resident(bft.shape), _resident(gcq.shape), _resident(gckv.shape),
                _resident(wqa.shape), _resident(wqb.shape), tab(qw), tab(qw), tab(MLA_ROPE), tab(MLA_ROPE)]
    out_shape = (
        jax.ShapeDtypeStruct((b, s, FOX_W), BF16),
        jax.ShapeDtypeStruct((b, s, FOX_W), F32),
        jax.ShapeDtypeStruct((b, s, FOX_W), F32),
        jax.ShapeDtypeStruct((b, s, FOX_W), BF16),
        jax.ShapeDtypeStruct((b, s, FOX_W), BF16),
        jax.ShapeDtypeStruct((b, FOX_W, s), BF16),
        jax.ShapeDtypeStruct((b, s, FOX_HEADS), F32),
        jax.ShapeDtypeStruct((b, FOX_HEADS, s), F32),
        jax.ShapeDtypeStruct((b, s, MLA_KV_RANK), F32),
        jax.ShapeDtypeStruct((b, s, MLA_ROPE), F32),
        jax.ShapeDtypeStruct((b, s, qw), BF16),
    )
    out_specs = (tok(FOX_W), tok(FOX_W), tok(FOX_W), tok(FOX_W), tok(FOX_W), tok_t(FOX_W), tok(FOX_HEADS),
                 tok_t(FOX_HEADS), tok(MLA_KV_RANK), tok(MLA_ROPE), tok(qw))
    return pl.pallas_call(
        _proj_kernel, grid=(ns, b), in_specs=in_specs, out_specs=out_specs, out_shape=out_shape,
        compiler_params=_params(("arbitrary", "arbitrary"), 48 << 20), name="proj",
    )(x, w1, w2, wfft, wfvt, bf, bft, gcq, gckv, wqa, wqb, ctab, stab, cos2, sin2)


def _cumsum_kernel(x_ref, o_ref, rep_ref, *, ch):
    t = x_ref.shape[2]
    r = lax.broadcasted_iota(jnp.int32, (ch, ch), 0)
    c = lax.broadcasted_iota(jnp.int32, (ch, ch), 1)
    upper = jnp.where(r <= c, 1.0, 0.0).astype(BF16)
    carry = jnp.zeros((FOX_HEADS, 1), F32)
    zeros = jnp.zeros((FOX_HEADS, ch), F32)
    for ci in range(t // ch):
        xc = x_ref[0, :, ci * ch:(ci + 1) * ch]
        hi = xc.astype(BF16).astype(F32)
        r1 = xc - hi
        mid = r1.astype(BF16).astype(F32)
        lo = (r1 - mid).astype(BF16).astype(F32)
        pieces = jnp.concatenate([hi, mid, lo, zeros], axis=0).astype(BF16)
        pc = _dot(pieces, upper)
        cum = pc[0:8] + pc[8:16] + pc[16:24] + carry
        nb = cum * (-LOG2E)
        o_ref[0, :, ci * ch:(ci + 1) * ch] = nb
        for hh in range(FOX_HEADS):
            rep_ref[0, hh, ci * ch:(ci + 1) * ch, :] = jnp.broadcast_to(nb[hh:hh + 1, :], (LANES, ch)).T
        carry = cum[:, ch - 1:ch]


def _neg_cumsum(logft):
    b, h, t = logft.shape
    ch = 256 if t % 256 == 0 else LANES
    spec = pl.BlockSpec((1, h, t), lambda bi: (bi, 0, 0))
    return pl.pallas_call(
        functools.partial(_cumsum_kernel, ch=ch), grid=(b,), in_specs=[spec],
        out_specs=(spec, pl.BlockSpec((1, h, t, LANES), lambda bi: (bi, 0, 0, 0))),
        out_shape=(jax.ShapeDtypeStruct((b, h, t), F32), jax.ShapeDtypeStruct((b, h, t, LANES), F32)),
        compiler_params=_params(("arbitrary",), 32 << 20), name="cumsum",
    )(logft)


def _kvx_kernel(ckv_ref, kr_ref, wk_ref, place_ref, wv_ref, wvt_ref, kp_ref, vm_ref, vmt_ref):
    cb = ckv_ref[0].astype(BF16)
    kp = _dot(cb, wk_ref[...]) + _dot(kr_ref[0].astype(BF16), place_ref[...])
    kp_ref[0] = kp.astype(BF16)
    vm_ref[0] = _dot(cb, wv_ref[...]).astype(BF16)
    vmt_ref[0] = _dot_nt(wvt_ref[...], cb).astype(BF16)


def _kv_expand(ckv, kr, wk, place, wv, wvt):
    b, t, _ = ckv.shape
    tm = _pick_tile(t, (512, 384, 256, 128))
    kw = N_PAIRS * MLA_QK_W

    def tok(width):
        return pl.BlockSpec((1, tm, width), lambda bi, ti: (bi, ti, 0))

    return pl.pallas_call(
        _kvx_kernel, grid=(b, t // tm),
        in_specs=[tok(MLA_KV_RANK), tok(MLA_ROPE), _resident(wk.shape), _resident(place.shape), _resident(wv.shape),
                  _resident(wvt.shape)],
        out_specs=(tok(kw), tok(MLA_W), pl.BlockSpec((1, MLA_W, tm), lambda bi, ti: (bi, 0, ti))),
        out_shape=(jax.ShapeDtypeStruct((b, t, kw), BF16), jax.ShapeDtypeStruct((b, t, MLA_W), BF16),
                   jax.ShapeDtypeStruct((b, MLA_W, t), BF16)),
        compiler_params=_params(("arbitrary", "arbitrary"), 32 << 20), name="kvexpand",
    )(ckv, kr, wk, place, wv, wvt)


def _head_lane_mask(lane, hh, width):
    if width == LANES:
        return (lane // FOX_HEAD_DIM) == hh
    nope = (lane < 2 * MLA_NOPE) & ((lane // MLA_NOPE) == hh)
    rope = (lane >= 2 * MLA_NOPE) & (lane < 2 * MLA_NOPE + 2 * MLA_ROPE) & (
        ((lane - 2 * MLA_NOPE) // MLA_ROPE) == hh)
    return nope | rope


def _attn_kernel(*refs, tq, tk, width, q_off, t_valid, chunked, has_bias):
    if has_bias:
        q_ref, k_ref, v_ref, nb_ref, o_ref = refs
    else:
        q_ref, k_ref, v_ref, o_ref = refs
        nb_ref = None
    nkb = k_ref.shape[1] // tk
    q_min = q_off + pl.program_id(2) * tq
    q_max = q_min + tq - 1
    if chunked:
        lim_min = (q_min // CHUNK + 1) * CHUNK
        lim_max = (q_max // CHUNK + 1) * CHUNK
    else:
        lim_min = q_min + 1
        lim_max = q_max + 1
    n_full = jnp.minimum(lim_min, t_valid) // tk
    n_vis = jnp.minimum((jnp.minimum(lim_max, t_valid) + tk - 1) // tk, nkb)

    qpos = q_min + lax.broadcasted_iota(jnp.int32, (tq, 1), 0)
    if chunked:
        row_lim = lax.shift_left(lax.shift_right_logical(qpos, 6) + 1, 6)
    else:
        row_lim = qpos + 1
    row_lim = jnp.minimum(row_lim, t_valid)

    q = q_ref[0]
    lane = lax.broadcasted_iota(jnp.int32, (1, width), 1)
    qh = [jnp.where(_head_lane_mask(lane, hh, width), q, jnp.zeros_like(q)) for hh in range(2)]

    def step(kb, carry, masked):
        k0 = pl.multiple_of(kb * tk, tk)
        kblk = k_ref[0, pl.ds(k0, tk), :]
        vblk = v_ref[0, pl.ds(k0, tk), :]
        out = []
        scores = [_dot_nt(qh[hh], kblk) for hh in range(2)]
        for hh in range(2):
            m, l, acc = carry[3 * hh:3 * hh + 3]
            s = scores[hh]
            if has_bias:
                s = s + nb_ref[0, 0, hh:hh + 1, pl.ds(k0, tk)]
            if masked:
                kpos = k0 + lax.broadcasted_iota(jnp.int32, (1, tk), 1)
                s = jnp.where(kpos < row_lim, s, NEG_INF)
            m_new = jnp.maximum(m, jnp.max(s, axis=-1, keepdims=True))
            p = jnp.exp2(s - m_new)
            alpha = jnp.exp2(m - m_new)
            l = alpha * l + jnp.sum(p, axis=-1, keepdims=True)
            acc = alpha * acc + _dot(p.astype(BF16), vblk)
            out += [m_new, l, acc]
        return tuple(out)

    init = (jnp.full((tq, 1), NEG_INF, F32), jnp.zeros((tq, 1), F32), jnp.zeros((tq, PAIR_W), F32)) * 2
    carry = lax.fori_loop(0, n_full, functools.partial(step, masked=False), init)
    carry = lax.fori_loop(n_full, n_vis, functools.partial(step, masked=True), carry)
    o0 = carry[2] * (1.0 / carry[1])
    o1 = carry[5] * (1.0 / carry[4])
    olane = lax.broadcasted_iota(jnp.int32, (1, PAIR_W), 1)
    o_ref[0] = jnp.where(olane < MLA_V, o0, o1).astype(BF16)


def _attention(q, k, v, nb, *, q_off, t_valid, chunked):
    assert CHUNK == 64
    b, sq, qw = q.shape
    width = qw // N_PAIRS
    tp = k.shape[1]
    tq = _pick_tile(sq, (512, 256, 128))
    tk = _pick_tile(tp, (512, 384, 256, 128))
    in_specs = [pl.BlockSpec((1, tq, width), lambda bi, pi, qi: (bi, qi, pi)),
                pl.BlockSpec((1, tp, width), lambda bi, pi, qi: (bi, 0, pi)),
                pl.BlockSpec((1, tp, PAIR_W), lambda bi, pi, qi: (bi, 0, pi))]
    args = [q, k, v]
    if nb is not None:
        in_specs.append(pl.BlockSpec((1, 1, 2, tp), lambda bi, pi, qi: (bi, pi, 0, 0)))
        args.append(nb)
    kern = functools.partial(_attn_kernel, tq=tq, tk=tk, width=width, q_off=q_off, t_valid=t_valid,
                             chunked=chunked, has_bias=nb is not None)
    return pl.pallas_call(
        kern, grid=(b, N_PAIRS, sq // tq), in_specs=in_specs,
        out_specs=pl.BlockSpec((1, tq, PAIR_W), lambda bi, pi, qi: (bi, qi, pi)),
        out_shape=jax.ShapeDtypeStruct((b, sq, N_PAIRS * PAIR_W), BF16),
        compiler_params=_params(("arbitrary", "arbitrary", "arbitrary"), 40 << 20),
        name="attn_mla" if chunked else "attn_fox",
    )(*args)


def _attn_t_kernel(*refs, tq, tk, width, q_off, t_valid, chunked, has_bias):
    if has_bias:
        q_ref, k_ref, vt_ref, nb_ref, o_ref, sa_ref, sb_ref = refs
    else:
        q_ref, k_ref, vt_ref, o_ref, sa_ref, sb_ref = refs
        nb_ref = None
    nq = q_ref.shape[1] // tq
    nkb = k_ref.shape[1] // tk

    items = []
    for qi in range(nq):
        q_min = q_off + qi * tq
        q_max = q_min + tq - 1
        if chunked:
            lim_min, lim_max = (q_min // CHUNK + 1) * CHUNK, (q_max // CHUNK + 1) * CHUNK
        else:
            lim_min, lim_max = q_min + 1, q_max + 1
        n_full = min(lim_min, t_valid) // tk
        n_vis = min(-(-min(lim_max, t_valid) // tk), nkb)
        items += [(qi, kb, kb >= n_full, kb == n_vis - 1) for kb in range(n_vis)]

    lane = lax.broadcasted_iota(jnp.int32, (1, width), 1)
    keep = [_head_lane_mask(lane, hh, width) for hh in range(2)]
    bufs = (sa_ref, sb_ref)

    def scores_into(s_ref, qi, kb):
        q = q_ref[0, qi * tq:(qi + 1) * tq, :]
        kblk = k_ref[0, kb * tk:(kb + 1) * tk, :]
        for hh in range(2):
            s_ref[hh] = _dot_nt(kblk, jnp.where(keep[hh], q, jnp.zeros_like(q)))

    def consume(s_ref, qi, kb, stats, masked):
        vtb = vt_ref[0, :, kb * tk:(kb + 1) * tk]
        if masked:
            qpos = q_off + qi * tq + lax.broadcasted_iota(jnp.int32, (1, tq), 1)
            if chunked:
                col_lim = lax.shift_left(lax.shift_right_logical(qpos, 6) + 1, 6)
            else:
                col_lim = qpos + 1
            col_lim = jnp.minimum(col_lim, t_valid)
            kpos = kb * tk + lax.broadcasted_iota(jnp.int32, (tk, 1), 0)
        out = []
        for hh in range(2):
            m, l, acc = stats[3 * hh:3 * hh + 3]
            s = s_ref[hh]
            if has_bias:
                nb = nb_ref[0, hh, kb * tk:(kb + 1) * tk, :]
                s = s + jnp.concatenate([nb] * (tq // LANES), axis=1)
            if masked:
                s = jnp.where(kpos < col_lim, s, NEG_INF)
            m_new = jnp.maximum(m, jnp.max(s, axis=0, keepdims=True))
            p = jnp.exp2(s - m_new)
            alpha = jnp.exp2(m - m_new)
            l = alpha * l + jnp.sum(p, axis=0, keepdims=True)
            acc = alpha * acc + _dot(vtb, p.astype(BF16))
            out += [m_new, l, acc]
        return tuple(out)

    orow = lax.broadcasted_iota(jnp.int32, (PAIR_W, 1), 0)
    scores_into(bufs[0], items[0][0], items[0][1])
    stats = None
    for i, (qi, kb, masked, last) in enumerate(items):
        if i + 1 < len(items):
            scores_into(bufs[(i + 1) % 2], items[i + 1][0], items[i + 1][1])
        if kb == 0:
            stats = (jnp.full((1, tq), NEG_INF, F32), jnp.zeros((1, tq), F32), jnp.zeros((PAIR_W, tq), F32)) * 2
        stats = consume(bufs[i % 2], qi, kb, stats, masked)
        if last:
            o0 = stats[2] * (1.0 / stats[1])
            o1 = stats[5] * (1.0 / stats[4])
            o_ref[0, qi * tq:(qi + 1) * tq, :] = jnp.where(orow < MLA_V, o0, o1).T.astype(BF16)


def _attention_t(q, k, vt, nbrep, *, q_off, t_valid, chunked):
    assert CHUNK == 64
    b, sq, qw = q.shape
    width = qw // N_PAIRS
    tp = k.shape[1]
    tq = _pick_tile(sq, (512, 256, 128))
    tk = _pick_tile(tp, (512, 384, 256, 128))
    in_specs = [pl.BlockSpec((1, sq, width), lambda bi, pi: (bi, 0, pi)),
                pl.BlockSpec((1, tp, width), lambda bi, pi: (bi, 0, pi)),
                pl.BlockSpec((1, PAIR_W, tp), lambda bi, pi: (bi, pi, 0))]
    args = [q, k, vt]
    if nbrep is not None:
        in_specs.append(pl.BlockSpec((1, 2, tp, LANES), lambda bi, pi: (bi, pi, 0, 0)))
        args.append(nbrep)
    kern = functools.partial(_attn_t_kernel, tq=tq, tk=tk, width=width, q_off=q_off, t_valid=t_valid,
                             chunked=chunked, has_bias=nbrep is not None)
    return pl.pallas_call(
        kern, grid=(b, N_PAIRS), in_specs=in_specs,
        out_specs=pl.BlockSpec((1, sq, PAIR_W), lambda bi, pi: (bi, 0, pi)),
        out_shape=jax.ShapeDtypeStruct((b, sq, N_PAIRS * PAIR_W), BF16),
        scratch_shapes=[pltpu.VMEM((2, tk, tq), F32), pltpu.VMEM((2, tk, tq), F32)],
        compiler_params=_params(("arbitrary", "arbitrary"), 40 << 20),
        name="attn_t_mla" if chunked else "attn_t_fox",
    )(*args)


def _mix_kernel(x_ref, of_ref, om_ref, wga_ref, wgb_ref, wof_ref, wom_ref, wout_ref, g_ref, b_ref, h_ref, *, alpha):
    x = x_ref[...]
    xb = x.astype(BF16)
    a = _sigmoid(_dot(xb, wga_ref[...])) * _dot(of_ref[...], wof_ref[...])
    bm = _sigmoid(_dot(xb, wgb_ref[...])) * _dot(om_ref[...], wom_ref[...])
    m = _dot((a + bm).astype(BF16), wout_ref[...])
    h_ref[...] = _layer_norm(alpha * x + m, g_ref[...], b_ref[...])


def _mix(x, of, om, w, alpha):
    r, d = x.shape
    tm = _pick_tile(r, (512, 256, 128))
    wga, wgb, wof, wom, wout, g, bb = w

    def row(width):
        return pl.BlockSpec((tm, width), lambda i: (i, 0))

    return pl.pallas_call(
        functools.partial(_mix_kernel, alpha=alpha), grid=(r // tm,),
        in_specs=[row(d), row(FOX_W), row(MLA_W)] + [_resident(a.shape) for a in w],
        out_specs=row(d), out_shape=jax.ShapeDtypeStruct((r, d), F32),
        compiler_params=_params(("arbitrary",), 48 << 20), name="mix",
    )(x, of, om, wga, wgb, wof, wom, wout, g, bb)


def _ple(hb, p_ref, wpg_ref, wpp_ref):
    return _sigmoid(_dot(hb, wpg_ref[...])) * _dot(p_ref[...].astype(BF16), wpp_ref[...])


def _ffn_chunk(n_ff):
    return _pick_tile(n_ff, (1408, 1024, 512, 256, 128))


def _dense_ffn_kernel(h_ref, p_ref, wg_ref, wu_ref, wd_ref, wpg_ref, wpp_ref, g_ref, b_ref, y_ref, *, alpha, fc):
    h = h_ref[...]
    hb = h.astype(BF16)
    n_ff = wg_ref.shape[1]
    acc = jnp.zeros(h.shape, F32)
    for c in range(n_ff // fc):
        sl = slice(c * fc, (c + 1) * fc)
        gt = _dot(hb, wg_ref[:, sl])
        act = (gt * _sigmoid(gt) * _dot(hb, wu_ref[:, sl])).astype(BF16)
        acc = acc + _dot(act, wd_ref[sl, :])
    y = alpha * h + acc + _ple(hb, p_ref, wpg_ref, wpp_ref)
    y_ref[...] = _layer_norm(y, g_ref[...], b_ref[...])


def _dense_ffn(h, p, w, alpha):
    r, d = h.shape
    tm = _pick_tile(r, (512, 256, 128))
    wg, wu, wd, wpg, wpp, g, bb = w

    def row(width):
        return pl.BlockSpec((tm, width), lambda i: (i, 0))

    return pl.pallas_call(
        functools.partial(_dense_ffn_kernel, alpha=alpha, fc=_ffn_chunk(wg.shape[1])), grid=(r // tm,),
        in_specs=[row(d), row(p.shape[1])] + [_resident(a.shape) for a in w],
        out_specs=row(d), out_shape=jax.ShapeDtypeStruct((r, d), F32),
        compiler_params=_params(("arbitrary",), 56 << 20), name="ffn_dense",
    )(h, p, wg, wu, wd, wpg, wpp, g, bb)


MOE_ROW_TILE = 512
ROUTE_LANES = 6


def _route_kernel(h_ref, wr_ref, br_ref, route_ref, hp_ref, counts_ref, carry_ref):
    @pl.when(pl.program_id(0) == 0)
    def _init():
        carry_ref[...] = jnp.zeros(carry_ref.shape, F32)

    hb = h_ref[...].astype(BF16)
    tm, d = hb.shape
    lane = lax.broadcasted_iota(jnp.int32, (1, LANES), 1)
    logits = _dot(hb, wr_ref[...]) + br_ref[...]
    lg = jnp.where(lane < N_EXPERTS, logits, -jnp.inf)
    m1 = jnp.max(lg, axis=-1, keepdims=True)
    i1 = jnp.min(jnp.where(lg == m1, lane, LANES), axis=-1, keepdims=True)
    lg2 = jnp.where(lane == i1, -jnp.inf, lg)
    m2 = jnp.max(lg2, axis=-1, keepdims=True)
    i2 = jnp.min(jnp.where(lg2 == m2, lane, LANES), axis=-1, keepdims=True)
    e2 = jnp.exp(m2 - m1)
    den = 1.0 + e2
    hit1 = lane == i1
    hit2 = lane == i2
    onehot = jnp.where(hit1, 1.0, 0.0) + jnp.where(hit2, 1.0, 0.0)
    rr = lax.broadcasted_iota(jnp.int32, (tm, tm), 0)
    cc = lax.broadcasted_iota(jnp.int32, (tm, tm), 1)
    lower = jnp.where(cc < rr, 1.0, 0.0).astype(BF16)
    prefix = _dot(lower, onehot.astype(BF16)) + carry_ref[...]
    rank1 = jnp.sum(jnp.where(hit1, prefix, 0.0), axis=-1, keepdims=True)
    rank2 = jnp.sum(jnp.where(hit2, prefix, 0.0), axis=-1, keepdims=True)
    fields = (i1.astype(F32), i2.astype(F32), rank1, rank2, 1.0 / den, e2 / den)
    route = jnp.zeros((tm, LANES), F32)
    for j, f in enumerate(fields):
        route = jnp.where(lane == j, f, route)
    route_ref[...] = route
    carry = carry_ref[...] + jnp.sum(onehot, axis=0, keepdims=True)
    carry_ref[...] = carry
    counts_ref[...] = carry
    bits = lax.bitcast_convert_type(hb.astype(F32), jnp.uint32)
    hp_ref[...] = (bits[:, d // 2:] & jnp.uint32(0xFFFF0000)) | lax.shift_right_logical(bits[:, :d // 2], jnp.uint32(16))


def _route(h, wr, br):
    r, d = h.shape
    tm = _pick_tile(r, (512, 256, 128))
    return pl.pallas_call(
        _route_kernel, grid=(r // tm,),
        in_specs=[pl.BlockSpec((tm, d), lambda i: (i, 0)), _resident(wr.shape), _resident(br.shape)],
        out_specs=(pl.BlockSpec((tm, LANES), lambda i: (i, 0)), pl.BlockSpec((tm, d // 2), lambda i: (i, 0)),
                   pl.BlockSpec((1, LANES), lambda i: (0, 0))),
        out_shape=(jax.ShapeDtypeStruct((r, LANES), F32), jax.ShapeDtypeStruct((r, d // 2), jnp.uint32),
                   jax.ShapeDtypeStruct((1, LANES), F32)),
        scratch_shapes=[pltpu.VMEM((1, LANES), F32)],
        compiler_params=_params(("arbitrary",), 32 << 20), name="moe_route",
    )(h, wr, br)


ROW_DMA_UNROLL = 8


def _row_scatter_kernel(idx_ref, src_ref, init_ref, dst_ref, sem, *, tm):
    del init_ref

    def copy(t, d_row):
        return pltpu.make_async_copy(src_ref.at[pl.ds(t, 1)], dst_ref.at[pl.ds(d_row, 1)], sem)

    def issue(t, c):
        copy(t, idx_ref[0, 0, 2 * t]).start()
        copy(t, idx_ref[0, 0, 2 * t + 1]).start()
        return c

    lax.fori_loop(0, tm, issue, 0, unroll=ROW_DMA_UNROLL)

    def drain(t, c):
        copy(0, 0).wait()
        copy(0, 0).wait()
        return c

    lax.fori_loop(0, tm, drain, 0, unroll=ROW_DMA_UNROLL)


def _row_scatter(idx3, src, init):
    steps, _, n = idx3.shape
    tm = n // 2
    any_spec = pl.BlockSpec(memory_space=pl.ANY)
    return pl.pallas_call(
        functools.partial(_row_scatter_kernel, tm=tm), grid=(steps,),
        in_specs=[pl.BlockSpec((1, 1, n), lambda i: (i, 0, 0), memory_space=pltpu.SMEM),
                  pl.BlockSpec((tm, src.shape[1]), lambda i: (i, 0)), any_spec],
        out_specs=any_spec, out_shape=jax.ShapeDtypeStruct(init.shape, init.dtype),
        scratch_shapes=[pltpu.SemaphoreType.DMA(())], input_output_aliases={2: 0},
        compiler_params=_params(("arbitrary",), 16 << 20), name="moe_scatter",
    )(idx3, src, init)


def _grouped_ffn_kernel(te_ref, tv_ref, x_ref, wg_ref, wu_ref, wd_ref, y_ref, *, fc):
    del te_ref
    valid = tv_ref[pl.program_id(0)] != 0

    @pl.when(valid)
    def _compute():
        pk = x_ref[...]
        lo = lax.bitcast_convert_type(lax.shift_left(pk, jnp.uint32(16)), F32)
        hi = lax.bitcast_convert_type(pk & jnp.uint32(0xFFFF0000), F32)
        xb = jnp.concatenate([lo, hi], axis=1).astype(BF16)
        n_ff = wg_ref.shape[2]
        acc = jnp.zeros(y_ref.shape, F32)
        for c in range(n_ff // fc):
            sl = slice(c * fc, (c + 1) * fc)
            gt = _dot(xb, wg_ref[0, :, sl])
            act = (gt * _sigmoid(gt) * _dot(xb, wu_ref[0, :, sl])).astype(BF16)
            acc = acc + _dot(act, wd_ref[0, sl, :])
        y_ref[...] = acc

    @pl.when(jnp.logical_not(valid))
    def _skip():
        y_ref[...] = jnp.zeros(y_ref.shape, F32)


def _grouped_ffn(tile_expert, tile_valid, xs, wg, wu, wd):
    rp, half = xs.shape
    n_e, d, n_ff = wg.shape
    tmx = MOE_ROW_TILE
    grid_spec = pltpu.PrefetchScalarGridSpec(
        num_scalar_prefetch=2, grid=(rp // tmx,),
        in_specs=[pl.BlockSpec((tmx, half), lambda i, te, tv: (i, 0)),
                  pl.BlockSpec((1, d, n_ff), lambda i, te, tv: (te[i], 0, 0)),
                  pl.BlockSpec((1, d, n_ff), lambda i, te, tv: (te[i], 0, 0)),
                  pl.BlockSpec((1, n_ff, d), lambda i, te, tv: (te[i], 0, 0))],
        out_specs=pl.BlockSpec((tmx, d), lambda i, te, tv: (i, 0)))
    return pl.pallas_call(
        functools.partial(_grouped_ffn_kernel, fc=_ffn_chunk(n_ff)), grid_spec=grid_spec,
        out_shape=jax.ShapeDtypeStruct((rp, d), F32),
        compiler_params=_params(("arbitrary",), V7X_SCOPED_VMEM_BYTES), name="moe_ffn",
    )(tile_expert, tile_valid, xs, wg, wu, wd)


def _moe_out_kernel(idx_ref, idx_next_ref, h_ref, p_ref, route_ref, ys_ref, wpg_ref, wpp_ref, g_ref, b_ref, y_ref,
                    buf_ref, sem, *, alpha):
    tm = h_ref.shape[0]
    step = pl.program_id(0)
    slot = lax.rem(step, 2)

    def copy(sl, k, t, s_row):
        return pltpu.make_async_copy(ys_ref.at[pl.ds(s_row, 1)], buf_ref.at[sl, k, pl.ds(t, 1)], sem.at[sl])

    def issue_block(rows_ref, sl):
        def issue(t, c):
            copy(sl, 0, t, rows_ref[0, 0, 2 * t]).start()
            copy(sl, 1, t, rows_ref[0, 0, 2 * t + 1]).start()
            return c

        lax.fori_loop(0, tm, issue, 0, unroll=ROW_DMA_UNROLL)

    @pl.when(step == 0)
    def _first():
        issue_block(idx_ref, 0)

    @pl.when(step + 1 < pl.num_programs(0))
    def _ahead():
        issue_block(idx_next_ref, 1 - slot)

    h = h_ref[...]
    hb = h.astype(BF16)
    ple = _ple(hb, p_ref, wpg_ref, wpp_ref)

    def drain(t, c):
        copy(slot, 0, 0, 0).wait()
        copy(slot, 1, 0, 0).wait()
        return c

    lax.fori_loop(0, tm, drain, 0, unroll=ROW_DMA_UNROLL)
    rt = route_ref[...]
    moe = rt[:, 4:5] * buf_ref[slot, 0] + rt[:, 5:6] * buf_ref[slot, 1]
    y_ref[...] = _layer_norm(alpha * h + moe + ple, g_ref[...], b_ref[...])


def _moe_out(idx3, h, p, route, ys, w, alpha):
    r, d = h.shape
    steps, _, n = idx3.shape
    tm = n // 2

    def row(width):
        return pl.BlockSpec((tm, width), lambda i: (i, 0))

    return pl.pallas_call(
        functools.partial(_moe_out_kernel, alpha=alpha), grid=(steps,),
        in_specs=[pl.BlockSpec((1, 1, n), lambda i: (i, 0, 0), memory_space=pltpu.SMEM),
                  pl.BlockSpec((1, 1, n), lambda i: (jnp.minimum(i + 1, steps - 1), 0, 0), memory_space=pltpu.SMEM),
                  row(d), row(p.shape[1]), row(LANES), pl.BlockSpec(memory_space=pl.ANY)]
        + [_resident(a.shape) for a in w],
        out_specs=row(d), out_shape=jax.ShapeDtypeStruct((r, d), F32),
        scratch_shapes=[pltpu.VMEM((2, 2, tm, d), F32), pltpu.SemaphoreType.DMA((2,))],
        compiler_params=_params(("arbitrary",), 48 << 20), name="moe_out",
    )(idx3, idx3, h, p, route, ys, *w)


def _moe_ffn(h, p, w, alpha):
    r, d = h.shape
    wr, br, wg, wu, wd, wpg, wpp, g, bb = w
    tmx = MOE_ROW_TILE
    route, hp, counts = _route(h, wr, br)
    cnt = counts[0, :N_EXPERTS].astype(jnp.int32)
    padded = (cnt + tmx - 1) // tmx * tmx
    ends = jnp.cumsum(padded)
    dest = jnp.take(ends - padded, route[:, 0:2].astype(jnp.int32)) + route[:, 2:4].astype(jnp.int32)
    n_tiles = -(-2 * r // tmx) + N_EXPERTS
    starts = jnp.arange(n_tiles, dtype=jnp.int32) * tmx
    tile_expert = jnp.minimum(jnp.sum(ends[None, :] <= starts[:, None], axis=1), N_EXPERTS - 1).astype(jnp.int32)
    tile_valid = (starts < ends[-1]).astype(jnp.int32)
    tm = _pick_tile(r, (512, 256, 128))
    dest3 = dest.reshape(r // tm, 1, 2 * tm)
    xs = _row_scatter(dest3, hp, jnp.zeros((n_tiles * tmx, d // 2), jnp.uint32))
    ys = _grouped_ffn(tile_expert, tile_valid, xs, wg, wu, wd)
    return _moe_out(dest3, h, p, route, ys, (wpg, wpp, g, bb), alpha)


def _pad_lanes(a, width=LANES):
    return jnp.pad(a, ((0, 0), (0, width - a.shape[1])))


def _rot_half_cols(a):
    half = a.shape[-1] // 2
    return jnp.concatenate([-a[..., half:], a[..., :half]], axis=-1)


def _mixer_weights(w_in, b_fox_f, g_cq, w_qb, g_ckv, w_kvb):
    d = w_in.shape[0]
    sizes = (FOX_W, FOX_W, FOX_W, FOX_HEADS, MLA_Q_RANK, MLA_KV_RANK, MLA_ROPE, d, d)
    cols, start = [], 0
    for n in sizes:
        cols.append(w_in[:, start:start + n])
        start += n
    wfq, wfk, wfv, wff, wcq, wckv, wkr, wga, wgb = cols
    w1 = jnp.concatenate([wfq * (FOX_HEAD_DIM ** -0.5 * LOG2E), wfk, wfv], axis=1).astype(BF16)
    w2 = jnp.concatenate([wcq, wckv, _pad_lanes(wkr), _pad_lanes(_rot_half_cols(wkr)), _pad_lanes(wff)],
                         axis=1).astype(BF16)
    wfft = jnp.pad(wff.T, ((0, 16 - FOX_HEADS), (0, 0))).astype(BF16)
    bf = b_fox_f.reshape(1, FOX_HEADS)
    bft = b_fox_f.reshape(FOX_HEADS, 1)

    qb = w_qb.reshape(MLA_Q_RANK, MLA_HEADS, MLA_NOPE + MLA_ROPE)
    q_nope, q_rope = qb[..., :MLA_NOPE], qb[..., MLA_NOPE:]
    q_rot = _rot_half_cols(q_rope)
    z_pad = jnp.zeros((MLA_Q_RANK, MLA_QK_W - 2 * MLA_NOPE - 2 * MLA_ROPE), F32)
    z_nope = jnp.zeros((MLA_Q_RANK, 2 * MLA_NOPE), F32)
    wqa = jnp.concatenate([jnp.concatenate([q_nope[:, 2 * j], q_nope[:, 2 * j + 1], q_rope[:, 2 * j],
                                            q_rope[:, 2 * j + 1], z_pad], axis=1) for j in range(N_PAIRS)], axis=1)
    wqb = jnp.concatenate([jnp.concatenate([z_nope, q_rot[:, 2 * j], q_rot[:, 2 * j + 1], z_pad], axis=1)
                           for j in range(N_PAIRS)], axis=1)

    kvb = w_kvb.reshape(MLA_KV_RANK, MLA_HEADS, MLA_NOPE + MLA_V)
    k_nope, v_up = kvb[..., :MLA_NOPE], kvb[..., MLA_NOPE:]
    zk = jnp.zeros((MLA_KV_RANK, MLA_QK_W - 2 * MLA_NOPE), F32)
    wk = jnp.concatenate([jnp.concatenate([k_nope[:, 2 * j], k_nope[:, 2 * j + 1], zk], axis=1)
                          for j in range(N_PAIRS)], axis=1).astype(BF16)
    eye = jnp.eye(MLA_ROPE, dtype=F32)
    place = jnp.concatenate([jnp.zeros((MLA_ROPE, 2 * MLA_NOPE), F32), eye, eye,
                             jnp.zeros((MLA_ROPE, MLA_QK_W - 2 * MLA_NOPE - 2 * MLA_ROPE), F32)], axis=1)
    place = jnp.tile(place, (1, N_PAIRS)).astype(BF16)
    wv = v_up.reshape(MLA_KV_RANK, MLA_W).astype(BF16)
    proj_w = (w1, w2, wfft, wfv.T.astype(BF16), bf, bft, g_cq.reshape(1, -1), g_ckv.reshape(1, -1),
              wqa.astype(BF16), wqb.astype(BF16))
    return proj_w, (wk, place, wv, wv.T), (wga.astype(BF16), wgb.astype(BF16))


def _rope_tables(pos):
    half = MLA_ROPE // 2
    inv = ROPE_THETA ** (-jnp.arange(half, dtype=F32) * 2.0 / MLA_ROPE)
    ang = pos.astype(F32)[:, None] * inv[None, :]
    cos2 = jnp.concatenate([jnp.cos(ang)] * 2, axis=1)
    sin2 = jnp.concatenate([jnp.sin(ang)] * 2, axis=1)
    n = pos.shape[0]
    scale = (MLA_NOPE + MLA_ROPE) ** -0.5 * LOG2E
    pad = jnp.zeros((n, MLA_QK_W - 2 * MLA_NOPE - 2 * MLA_ROPE), F32)
    ctab = jnp.concatenate([jnp.full((n, 2 * MLA_NOPE), scale, F32), scale * cos2, scale * cos2, pad], axis=1)
    stab = jnp.concatenate([jnp.zeros((n, 2 * MLA_NOPE), F32), scale * sin2, scale * sin2, pad], axis=1)
    return jnp.tile(ctab, (1, N_PAIRS)), jnp.tile(stab, (1, N_PAIRS)), cos2, sin2


def _pad_time(a, tp, axis):
    pad = [(0, 0)] * a.ndim
    pad[axis] = (0, tp - a.shape[axis])
    return jnp.pad(a, pad)


def _layer(x, past, p, proj_w, kvx_w, mix_w, ffn_w, is_moe, alpha):
    b, s, d = x.shape
    n_past = 0 if past is None else past[0].shape[1]
    t = n_past + s
    tabs = _rope_tables(n_past + jnp.arange(s))
    fq, fk, fv, fkb, fvb, fvt, logf, logft, ckv, kr, qp = _proj(x, proj_w, tabs)
    if past is None:
        tp = t
        k_fox, logft_all, ckv_all, kr_all = fkb, logft, ckv, kr
    else:
        tp = -(-t // LANES) * LANES
        pk, pv, plogf, pckv, pkr = past
        k_fox = _pad_time(jnp.concatenate([pk.reshape(b, n_past, FOX_W).astype(BF16), fkb], axis=1), tp, 1)
        v_fox = _pad_time(jnp.concatenate([pv.reshape(b, n_past, FOX_W).astype(BF16), fvb], axis=1), tp, 1)
        logft_all = _pad_time(jnp.concatenate([jnp.swapaxes(plogf, 1, 2), logft], axis=2), tp, 2)
        ckv_all = _pad_time(jnp.concatenate([pckv, ckv], axis=1), tp, 1)
        kr_all = _pad_time(jnp.concatenate([pkr, kr], axis=1), tp, 1)
    nb, nbrep = _neg_cumsum(logft_all)
    k_mla, v_mla, vt_mla = _kv_expand(ckv_all, kr_all, *kvx_w)
    if past is None:
        o_fox = _attention_t(fq, k_fox, fvt, nbrep, q_off=0, t_valid=t, chunked=False)
        o_mla = _attention_t(qp, k_mla, vt_mla, None, q_off=0, t_valid=t, chunked=True)
    else:
        nb = nb.reshape(b, N_PAIRS, 2, tp)
        o_fox = _attention(fq, k_fox, v_fox, nb, q_off=n_past, t_valid=t, chunked=False)
        o_mla = _attention(qp, k_mla, v_mla, None, q_off=n_past, t_valid=t, chunked=True)
    r = b * s
    h = _mix(x.reshape(r, d), o_fox.reshape(r, FOX_W), o_mla.reshape(r, MLA_W), mix_w, alpha)
    ffn = _moe_ffn if is_moe else _dense_ffn
    y = ffn(h, p.reshape(r, -1), ffn_w, alpha).reshape(b, s, d)
    new_rows = (fk.reshape(b, s, FOX_HEADS, FOX_HEAD_DIM), fv.reshape(b, s, FOX_HEADS, FOX_HEAD_DIM), logf, ckv, kr)
    return y, new_rows


def kernel(x_prompt, x_sample, cache_fox_k, cache_fox_v, cache_fox_logf, cache_mla_ckv, cache_mla_krope,
           p_prompt, p_sample, w_in, b_fox_f, g_mla_cq, w_mla_qb, g_mla_ckv, w_mla_kvb, w_o_fox, w_o_mla,
           w_out, ln_mix_g, ln_mix_b, w_ffn_gate, w_ffn_up, w_ffn_down, w_router, b_router, w_moe_gate,
           w_moe_up, w_moe_down, w_ple_proj, w_ple_gate, ln_ffn_g, ln_ffn_b):
    depth = w_in.shape[0]
    alpha = (2 * depth) ** 0.25
    hp, hs = x_prompt, x_sample
    rows_p, rows_s = [], []
    for i in range(depth):
        proj_w, kvx_w, (wga, wgb) = _mixer_weights(w_in[i], b_fox_f[i], g_mla_cq[i], w_mla_qb[i], g_mla_ckv[i],
                                                   w_mla_kvb[i])
        mix_w = (wga, wgb, w_o_fox[i].astype(BF16), w_o_mla[i].astype(BF16), w_out[i].astype(BF16),
                 ln_mix_g[i].reshape(1, -1), ln_mix_b[i].reshape(1, -1))
        tail = (w_ple_gate[i].astype(BF16), w_ple_proj[i].astype(BF16),
                ln_ffn_g[i].reshape(1, -1), ln_ffn_b[i].reshape(1, -1))
        j = i // 2
        is_moe = i % 2 == 1
        if is_moe:
            ffn_w = (_pad_lanes(w_router[j]).astype(BF16), _pad_lanes(b_router[j].reshape(1, -1)),
                     w_moe_gate[j].astype(BF16), w_moe_up[j].astype(BF16), w_moe_down[j].astype(BF16)) + tail
        else:
            ffn_w = (w_ffn_gate[j].astype(BF16), w_ffn_up[j].astype(BF16), w_ffn_down[j].astype(BF16)) + tail
        past = (cache_fox_k[i], cache_fox_v[i], cache_fox_logf[i], cache_mla_ckv[i], cache_mla_krope[i])
        hp, new_p = _layer(hp, None, p_prompt[i], proj_w, kvx_w, mix_w, ffn_w, is_moe, alpha)
        hs, new_s = _layer(hs, past, p_sample[i], proj_w, kvx_w, mix_w, ffn_w, is_moe, alpha)
        rows_p.append(new_p)
        rows_s.append(new_s)

    def stack(rows, idx):
        return jnp.stack([r[idx] for r in rows], axis=0)

    return (hp, hs) + tuple(stack(rows_p, k) for k in range(5)) + tuple(stack(rows_s, k) for k in range(5))
```

```python
import functools

import jax
import jax.numpy as jnp
from jax import lax
from jax.experimental import pallas as pl
from jax.experimental.pallas import tpu as pltpu

CHUNK = 64
FOX_HEADS = 8
FOX_HEAD_DIM = 64
FOX_W = FOX_HEADS * FOX_HEAD_DIM
MLA_HEADS = 8
MLA_Q_RANK = 256
MLA_KV_RANK = 128
MLA_NOPE = 64
MLA_ROPE = 32
MLA_V = 64
MLA_W = MLA_HEADS * MLA_V
ROPE_THETA = 10000.0
N_EXPERTS = 8
LN_EPS = 1e-5
RMS_EPS = 1e-6
NEG_INF = -1e30
LOG2E = 1.4426950408889634

LANES = 128
PAIR_W = 2 * MLA_V
MLA_QK_W = 256
N_PAIRS = FOX_HEADS // 2
V7X_SCOPED_VMEM_BYTES = 60000 * 1024

BF16 = jnp.bfloat16
F32 = jnp.float32


def _dot(a, b):
    return jnp.dot(a, b, preferred_element_type=F32)


def _dot_nt(a, b):
    return lax.dot_general(a, b, (((1,), (1,)), ((), ())), preferred_element_type=F32)


def _sigmoid(x):
    return 1.0 / (1.0 + jnp.exp(-x))


def _log_sigmoid(x):
    return jnp.minimum(x, 0.0) - jnp.log1p(jnp.exp(-jnp.abs(x)))


def _rms_norm(x, g):
    return x * lax.rsqrt(jnp.mean(jnp.square(x), axis=-1, keepdims=True) + RMS_EPS) * g


def _layer_norm(x, g, b):
    mu = jnp.mean(x, axis=-1, keepdims=True)
    xc = x - mu
    var = jnp.mean(jnp.square(xc), axis=-1, keepdims=True)
    return xc * lax.rsqrt(var + LN_EPS) * g + b


def _resident(shape):
    nd = len(shape)
    return pl.BlockSpec(shape, lambda *_: (0,) * nd, pipeline_mode=pl.Buffered(1))


def _params(semantics, vmem_bytes):
    return pltpu.CompilerParams(dimension_semantics=semantics,
                                vmem_limit_bytes=min(int(vmem_bytes), V7X_SCOPED_VMEM_BYTES))


def _pick_tile(n, candidates):
    for c in candidates:
        if n % c == 0:
            return c
    return n


def _proj_kernel(x_ref, w1_ref, w2_ref, wfft_ref, wfvt_ref, bf_ref, bft_ref, gcq_ref, gckv_ref, wqa_ref, wqb_ref,
                 ctab_ref, stab_ref, cos_ref, sin_ref,
                 fq_ref, fk_ref, fv_ref, fkb_ref, fvb_ref, fvt_ref, logf_ref, logft_ref, ckv_ref, kr_ref, qp_ref):
    xb = x_ref[0].astype(BF16)
    z1 = _dot(xb, w1_ref[...])
    fq_ref[0] = z1[:, :FOX_W].astype(BF16)
    fk = z1[:, FOX_W:2 * FOX_W]
    fk_ref[0] = fk
    fkb_ref[0] = fk.astype(BF16)
    fv = z1[:, 2 * FOX_W:]
    fv_ref[0] = fv
    fvb_ref[0] = fv.astype(BF16)
    fvt_ref[0] = _dot_nt(wfvt_ref[...], xb).astype(BF16)

    z2 = _dot(xb, w2_ref[...])
    cq = z2[:, :MLA_Q_RANK]
    o = MLA_Q_RANK
    ckv = z2[:, o:o + MLA_KV_RANK]
    o += MLA_KV_RANK
    kr = z2[:, o:o + MLA_ROPE]
    krr = z2[:, o + LANES:o + LANES + MLA_ROPE]
    ff = z2[:, o + 2 * LANES:o + 2 * LANES + FOX_HEADS]
    logf_ref[0] = _log_sigmoid(ff + bf_ref[...])
    fft = _dot_nt(wfft_ref[...], xb)
    logft_ref[0] = _log_sigmoid(fft[:FOX_HEADS] + bft_ref[...])
    ckv_ref[0] = _rms_norm(ckv, gckv_ref[...])
    kr_ref[0] = kr * cos_ref[...] + krr * sin_ref[...]
    cqn = _rms_norm(cq, gcq_ref[...]).astype(BF16)
    qp = _dot(cqn, wqa_ref[...]) * ctab_ref[...] + _dot(cqn, wqb_ref[...]) * stab_ref[...]
    qp_ref[0] = qp.astype(BF16)


def _proj(x, w, tabs):
    b, s, d = x.shape
    tm = _pick_tile(s, (512, 256, 128))
    ns = s // tm
    w1, w2, wfft, wfvt, bf, bft, gcq, gckv, wqa, wqb = w
    ctab, stab, cos2, sin2 = tabs
    qw = N_PAIRS * MLA_QK_W

    def tok(width):
        return pl.BlockSpec((1, tm, width), lambda si, bi: (bi, si, 0))

    def tok_t(height):
        return pl.BlockSpec((1, height, tm), lambda si, bi: (bi, 0, si))

    def tab(width):
        return pl.BlockSpec((tm, width), lambda si, bi: (si, 0))

    in_specs = [tok(d), _resident(w1.shape), _resident(w2.shape), _resident(wfft.shape), _resident(wfvt.shape),
                _resident(bf.shape), _resident(bft.shape), _resident(gcq.shape), _resident(gckv.shape),
                _resident(wqa.shape), _resident(wqb.shape), tab(qw), tab(qw), tab(MLA_ROPE), tab(MLA_ROPE)]
    out_shape = (
        jax.ShapeDtypeStruct((b, s, FOX_W), BF16),
        jax.ShapeDtypeStruct((b, s, FOX_W), F32),
        jax.ShapeDtypeStruct((b, s, FOX_W), F32),
        jax.ShapeDtypeStruct((b, s, FOX_W), BF16),
        jax.ShapeDtypeStruct((b, s, FOX_W), BF16),
        jax.ShapeDtypeStruct((b, FOX_W, s), BF16),
        jax.ShapeDtypeStruct((b, s, FOX_HEADS), F32),
        jax.ShapeDtypeStruct((b, FOX_HEADS, s), F32),
        jax.ShapeDtypeStruct((b, s, MLA_KV_RANK), F32),
        jax.ShapeDtypeStruct((b, s, MLA_ROPE), F32),
        jax.ShapeDtypeStruct((b, s, qw), BF16),
    )
    out_specs = (tok(FOX_W), tok(FOX_W), tok(FOX_W), tok(FOX_W), tok(FOX_W), tok_t(FOX_W), tok(FOX_HEADS),
                 tok_t(FOX_HEADS), tok(MLA_KV_RANK), tok(MLA_ROPE), tok(qw))
    return pl.pallas_call(
        _proj_kernel, grid=(ns, b), in_specs=in_specs, out_specs=out_specs, out_shape=out_shape,
        compiler_params=_params(("arbitrary", "arbitrary"), 48 << 20), name="proj",
    )(x, w1, w2, wfft, wfvt, bf, bft, gcq, gckv, wqa, wqb, ctab, stab, cos2, sin2)


def _cumsum_kernel(x_ref, o_ref, aug_ref, *, ch):
    t = x_ref.shape[2]
    lane = lax.broadcasted_iota(jnp.int32, (1, LANES), 1)
    r = lax.broadcasted_iota(jnp.int32, (ch, ch), 0)
    c = lax.broadcasted_iota(jnp.int32, (ch, ch), 1)
    upper = jnp.where(r <= c, 1.0, 0.0).astype(BF16)
    carry = jnp.zeros((FOX_HEADS, 1), F32)
    zeros = jnp.zeros((FOX_HEADS, ch), F32)
    for ci in range(t // ch):
        xc = x_ref[0, :, ci * ch:(ci + 1) * ch]
        hi = xc.astype(BF16).astype(F32)
        r1 = xc - hi
        mid = r1.astype(BF16).astype(F32)
        lo = (r1 - mid).astype(BF16).astype(F32)
        pieces = jnp.concatenate([hi, mid, lo, zeros], axis=0).astype(BF16)
        pc = _dot(pieces, upper)
        cum = pc[0:8] + pc[8:16] + pc[16:24] + carry
        nb = cum * (-LOG2E)
        o_ref[0, :, ci * ch:(ci + 1) * ch] = nb
        for hh in range(FOX_HEADS):
            rep = jnp.broadcast_to(nb[hh:hh + 1, :], (LANES, ch)).T
            a_hi = rep.astype(BF16).astype(F32)
            a_r = rep - a_hi
            a_mid = a_r.astype(BF16).astype(F32)
            a_lo = (a_r - a_mid).astype(BF16).astype(F32)
            aug = jnp.where(lane == 0, a_hi, jnp.where(lane == 1, a_mid, jnp.where(lane == 2, a_lo, 0.0)))
            aug_ref[0, hh, ci * ch:(ci + 1) * ch, :] = aug.astype(BF16)
        carry = cum[:, ch - 1:ch]


def _neg_cumsum(logft):
    b, h, t = logft.shape
    ch = 256 if t % 256 == 0 else LANES
    spec = pl.BlockSpec((1, h, t), lambda bi: (bi, 0, 0))
    return pl.pallas_call(
        functools.partial(_cumsum_kernel, ch=ch), grid=(b,), in_specs=[spec],
        out_specs=(spec, pl.BlockSpec((1, h, t, LANES), lambda bi: (bi, 0, 0, 0))),
        out_shape=(jax.ShapeDtypeStruct((b, h, t), F32), jax.ShapeDtypeStruct((b, h, t, LANES), BF16)),
        compiler_params=_params(("arbitrary",), 32 << 20), name="cumsum",
    )(logft)


def _kvx_kernel(ckv_ref, kr_ref, wk_ref, place_ref, wv_ref, wvt_ref, kp_ref, vm_ref, vmt_ref):
    cb = ckv_ref[0].astype(BF16)
    kp = _dot(cb, wk_ref[...]) + _dot(kr_ref[0].astype(BF16), place_ref[...])
    kp_ref[0] = kp.astype(BF16)
    vm_ref[0] = _dot(cb, wv_ref[...]).astype(BF16)
    vmt_ref[0] = _dot_nt(wvt_ref[...], cb).astype(BF16)


def _kv_expand(ckv, kr, wk, place, wv, wvt):
    b, t, _ = ckv.shape
    tm = _pick_tile(t, (512, 384, 256, 128))
    kw = N_PAIRS * MLA_QK_W

    def tok(width):
        return pl.BlockSpec((1, tm, width), lambda bi, ti: (bi, ti, 0))

    return pl.pallas_call(
        _kvx_kernel, grid=(b, t // tm),
        in_specs=[tok(MLA_KV_RANK), tok(MLA_ROPE), _resident(wk.shape), _resident(place.shape), _resident(wv.shape),
                  _resident(wvt.shape)],
        out_specs=(tok(kw), tok(MLA_W), pl.BlockSpec((1, MLA_W, tm), lambda bi, ti: (bi, 0, ti))),
        out_shape=(jax.ShapeDtypeStruct((b, t, kw), BF16), jax.ShapeDtypeStruct((b, t, MLA_W), BF16),
                   jax.ShapeDtypeStruct((b, MLA_W, t), BF16)),
        compiler_params=_params(("arbitrary", "arbitrary"), 32 << 20), name="kvexpand",
    )(ckv, kr, wk, place, wv, wvt)


def _head_lane_mask(lane, hh, width):
    if width == LANES:
        return (lane // FOX_HEAD_DIM) == hh
    nope = (lane < 2 * MLA_NOPE) & ((lane // MLA_NOPE) == hh)
    rope = (lane >= 2 * MLA_NOPE) & (lane < 2 * MLA_NOPE + 2 * MLA_ROPE) & (
        ((lane - 2 * MLA_NOPE) // MLA_ROPE) == hh)
    return nope | rope


def _attn_kernel(*refs, tq, tk, width, q_off, t_valid, chunked, has_bias):
    if has_bias:
        q_ref, k_ref, v_ref, nb_ref, o_ref = refs
    else:
        q_ref, k_ref, v_ref, o_ref = refs
        nb_ref = None
    nkb = k_ref.shape[1] // tk
    q_min = q_off + pl.program_id(2) * tq
    q_max = q_min + tq - 1
    if chunked:
        lim_min = (q_min // CHUNK + 1) * CHUNK
        lim_max = (q_max // CHUNK + 1) * CHUNK
    else:
        lim_min = q_min + 1
        lim_max = q_max + 1
    n_full = jnp.minimum(lim_min, t_valid) // tk
    n_vis = jnp.minimum((jnp.minimum(lim_max, t_valid) + tk - 1) // tk, nkb)

    qpos = q_min + lax.broadcasted_iota(jnp.int32, (tq, 1), 0)
    if chunked:
        row_lim = lax.shift_left(lax.shift_right_logical(qpos, 6) + 1, 6)
    else:
        row_lim = qpos + 1
    row_lim = jnp.minimum(row_lim, t_valid)

    q = q_ref[0]
    lane = lax.broadcasted_iota(jnp.int32, (1, width), 1)
    qh = [jnp.where(_head_lane_mask(lane, hh, width), q, jnp.zeros_like(q)) for hh in range(2)]

    def step(kb, carry, masked):
        k0 = pl.multiple_of(kb * tk, tk)
        kblk = k_ref[0, pl.ds(k0, tk), :]
        vblk = v_ref[0, pl.ds(k0, tk), :]
        out = []
        scores = [_dot_nt(qh[hh], kblk) for hh in range(2)]
        for hh in range(2):
            m, l, acc = carry[3 * hh:3 * hh + 3]
            s = scores[hh]
            if has_bias:
                s = s + nb_ref[0, 0, hh:hh + 1, pl.ds(k0, tk)]
            if masked:
                kpos = k0 + lax.broadcasted_iota(jnp.int32, (1, tk), 1)
                s = jnp.where(kpos < row_lim, s, NEG_INF)
            m_new = jnp.maximum(m, jnp.max(s, axis=-1, keepdims=True))
            p = jnp.exp2(s - m_new)
            alpha = jnp.exp2(m - m_new)
            l = alpha * l + jnp.sum(p, axis=-1, keepdims=True)
            acc = alpha * acc + _dot(p.astype(BF16), vblk)
            out += [m_new, l, acc]
        return tuple(out)

    init = (jnp.full((tq, 1), NEG_INF, F32), jnp.zeros((tq, 1), F32), jnp.zeros((tq, PAIR_W), F32)) * 2
    carry = lax.fori_loop(0, n_full, functools.partial(step, masked=False), init)
    carry = lax.fori_loop(n_full, n_vis, functools.partial(step, masked=True), carry)
    o0 = carry[2] * (1.0 / carry[1])
    o1 = carry[5] * (1.0 / carry[4])
    olane = lax.broadcasted_iota(jnp.int32, (1, PAIR_W), 1)
    o_ref[0] = jnp.where(olane < MLA_V, o0, o1).astype(BF16)


def _attention(q, k, v, nb, *, q_off, t_valid, chunked):
    assert CHUNK == 64
    b, sq, qw = q.shape
    width = qw // N_PAIRS
    tp = k.shape[1]
    tq = _pick_tile(sq, (512, 256, 128))
    tk = _pick_tile(tp, (512, 384, 256, 128))
    in_specs = [pl.BlockSpec((1, tq, width), lambda bi, pi, qi: (bi, qi, pi)),
                pl.BlockSpec((1, tp, width), lambda bi, pi, qi: (bi, 0, pi)),
                pl.BlockSpec((1, tp, PAIR_W), lambda bi, pi, qi: (bi, 0, pi))]
    args = [q, k, v]
    if nb is not None:
        in_specs.append(pl.BlockSpec((1, 1, 2, tp), lambda bi, pi, qi: (bi, pi, 0, 0)))
        args.append(nb)
    kern = functools.partial(_attn_kernel, tq=tq, tk=tk, width=width, q_off=q_off, t_valid=t_valid,
                             chunked=chunked, has_bias=nb is not None)
    return pl.pallas_call(
        kern, grid=(b, N_PAIRS, sq // tq), in_specs=in_specs,
        out_specs=pl.BlockSpec((1, tq, PAIR_W), lambda bi, pi, qi: (bi, qi, pi)),
        out_shape=jax.ShapeDtypeStruct((b, sq, N_PAIRS * PAIR_W), BF16),
        compiler_params=_params(("arbitrary", "arbitrary", "arbitrary"), 40 << 20),
        name="attn_mla" if chunked else "attn_fox",
    )(*args)


def _attn_t_kernel(*refs, tq, tk, width, q_off, t_valid, chunked, has_bias):
    if has_bias:
        q_ref, k_ref, vt_ref, nb_ref, o_ref, sa_ref, sb_ref = refs
    else:
        q_ref, k_ref, vt_ref, o_ref, sa_ref, sb_ref = refs
        nb_ref = None
    nq = q_ref.shape[1] // tq
    nkb = k_ref.shape[1] // tk

    items = []
    for qi in range(nq):
        q_min = q_off + qi * tq
        q_max = q_min + tq - 1
        if chunked:
            lim_min, lim_max = (q_min // CHUNK + 1) * CHUNK, (q_max // CHUNK + 1) * CHUNK
        else:
            lim_min, lim_max = q_min + 1, q_max + 1
        n_full = min(lim_min, t_valid) // tk
        n_vis = min(-(-min(lim_max, t_valid) // tk), nkb)
        items += [(qi, kb, kb >= n_full, kb == n_vis - 1) for kb in range(n_vis)]

    lane = lax.broadcasted_iota(jnp.int32, (1, width), 1)
    keep = [_head_lane_mask(lane, hh, width) for hh in range(2)]
    bufs = (sa_ref, sb_ref)

    ones3 = jnp.where(lax.broadcasted_iota(jnp.int32, (tq, LANES), 1) < 3, 1.0, 0.0).astype(BF16)
    ones_rows = jnp.ones((16, tk), BF16)

    def scores_into(s_ref, qi, kb):
        q = q_ref[0, qi * tq:(qi + 1) * tq, :]
        kblk = k_ref[0, kb * tk:(kb + 1) * tk, :]
        for hh in range(2):
            qh = jnp.where(keep[hh], q, jnp.zeros_like(q))
            if has_bias:
                kh = jnp.concatenate([kblk, nb_ref[0, hh, kb * tk:(kb + 1) * tk, :]], axis=1)
                s_ref[hh] = _dot_nt(kh, jnp.concatenate([qh, ones3], axis=1))
            else:
                s_ref[hh] = _dot_nt(kblk, qh)

    def consume(s_ref, qi, kb, stats, masked):
        if masked:
            qpos = q_off + qi * tq + lax.broadcasted_iota(jnp.int32, (1, tq), 1)
            if chunked:
                col_lim = lax.shift_left(lax.shift_right_logical(qpos, 6) + 1, 6)
            else:
                col_lim = qpos + 1
            col_lim = jnp.minimum(col_lim, t_valid)
            kpos = kb * tk + lax.broadcasted_iota(jnp.int32, (tk, 1), 0)
        out = []
        for hh in range(2):
            m, acc = stats[2 * hh:2 * hh + 2]
            s = s_ref[hh]
            if masked:
                s = jnp.where(kpos < col_lim, s, NEG_INF)
            m_new = jnp.maximum(m, jnp.max(s, axis=0, keepdims=True))
            p = jnp.exp2(s - m_new)
            alpha = jnp.exp2(m - m_new)
            vth = jnp.concatenate([vt_ref[0, hh * MLA_V:(hh + 1) * MLA_V, kb * tk:(kb + 1) * tk], ones_rows], axis=0)
            acc = alpha * acc + _dot(vth, p.astype(BF16))
            out += [m_new, acc]
        return tuple(out)

    scores_into(bufs[0], items[0][0], items[0][1])
    stats = None
    for i, (qi, kb, masked, last) in enumerate(items):
        if i + 1 < len(items):
            scores_into(bufs[(i + 1) % 2], items[i + 1][0], items[i + 1][1])
        if kb == 0:
            stats = (jnp.full((1, tq), NEG_INF, F32), jnp.zeros((MLA_V + 16, tq), F32)) * 2
        stats = consume(bufs[i % 2], qi, kb, stats, masked)
        if last:
            outs = [stats[2 * hh + 1][:MLA_V] * (1.0 / stats[2 * hh + 1][MLA_V:MLA_V + 1]) for hh in range(2)]
            o_ref[0, qi * tq:(qi + 1) * tq, :] = jnp.concatenate(outs, axis=0).T.astype(BF16)


def _attention_t(q, k, vt, nbrep, *, q_off, t_valid, chunked):
    assert CHUNK == 64
    b, sq, qw = q.shape
    width = qw // N_PAIRS
    tp = k.shape[1]
    tq = _pick_tile(sq, (512, 256, 128))
    tk = _pick_tile(tp, (512, 384, 256, 128))
    in_specs = [pl.BlockSpec((1, sq, width), lambda bi, pi: (bi, 0, pi)),
                pl.BlockSpec((1, tp, width), lambda bi, pi: (bi, 0, pi)),
                pl.BlockSpec((1, PAIR_W, tp), lambda bi, pi: (bi, pi, 0))]
    args = [q, k, vt]
    if nbrep is not None:
        in_specs.append(pl.BlockSpec((1, 2, tp, LANES), lambda bi, pi: (bi, pi, 0, 0)))
        args.append(nbrep)
    kern = functools.partial(_attn_t_kernel, tq=tq, tk=tk, width=width, q_off=q_off, t_valid=t_valid,
                             chunked=chunked, has_bias=nbrep is not None)
    return pl.pallas_call(
        kern, grid=(b, N_PAIRS), in_specs=in_specs,
        out_specs=pl.BlockSpec((1, sq, PAIR_W), lambda bi, pi: (bi, 0, pi)),
        out_shape=jax.ShapeDtypeStruct((b, sq, N_PAIRS * PAIR_W), BF16),
        scratch_shapes=[pltpu.VMEM((2, tk, tq), F32), pltpu.VMEM((2, tk, tq), F32)],
        compiler_params=_params(("arbitrary", "arbitrary"), 40 << 20),
        name="attn_t_mla" if chunked else "attn_t_fox",
    )(*args)


def _mix_kernel(x_ref, of_ref, om_ref, wga_ref, wgb_ref, wof_ref, wom_ref, wout_ref, g_ref, b_ref, h_ref, *, alpha):
    x = x_ref[...]
    xb = x.astype(BF16)
    a = _sigmoid(_dot(xb, wga_ref[...])) * _dot(of_ref[...], wof_ref[...])
    bm = _sigmoid(_dot(xb, wgb_ref[...])) * _dot(om_ref[...], wom_ref[...])
    m = _dot((a + bm).astype(BF16), wout_ref[...])
    h_ref[...] = _layer_norm(alpha * x + m, g_ref[...], b_ref[...])


def _mix(x, of, om, w, alpha):
    r, d = x.shape
    tm = _pick_tile(r, (512, 256, 128))
    wga, wgb, wof, wom, wout, g, bb = w

    def row(width):
        return pl.BlockSpec((tm, width), lambda i: (i, 0))

    return pl.pallas_call(
        functools.partial(_mix_kernel, alpha=alpha), grid=(r // tm,),
        in_specs=[row(d), row(FOX_W), row(MLA_W)] + [_resident(a.shape) for a in w],
        out_specs=row(d), out_shape=jax.ShapeDtypeStruct((r, d), F32),
        compiler_params=_params(("arbitrary",), 48 << 20), name="mix",
    )(x, of, om, wga, wgb, wof, wom, wout, g, bb)


def _ple(hb, p_ref, wpg_ref, wpp_ref):
    return _sigmoid(_dot(hb, wpg_ref[...])) * _dot(p_ref[...].astype(BF16), wpp_ref[...])


def _ffn_chunk(n_ff):
    return _pick_tile(n_ff, (1408, 1024, 512, 256, 128))


def _dense_ffn_kernel(h_ref, p_ref, wg_ref, wu_ref, wd_ref, wpg_ref, wpp_ref, g_ref, b_ref, y_ref, *, alpha, fc):
    h = h_ref[...]
    hb = h.astype(BF16)
    n_ff = wg_ref.shape[1]
    acc = jnp.zeros(h.shape, F32)
    for c in range(n_ff // fc):
        sl = slice(c * fc, (c + 1) * fc)
        gt = _dot(hb, wg_ref[:, sl])
        act = (gt * _sigmoid(gt) * _dot(hb, wu_ref[:, sl])).astype(BF16)
        acc = acc + _dot(act, wd_ref[sl, :])
    y = alpha * h + acc + _ple(hb, p_ref, wpg_ref, wpp_ref)
    y_ref[...] = _layer_norm(y, g_ref[...], b_ref[...])


def _dense_ffn(h, p, w, alpha):
    r, d = h.shape
    tm = _pick_tile(r, (512, 256, 128))
    wg, wu, wd, wpg, wpp, g, bb = w

    def row(width):
        return pl.BlockSpec((tm, width), lambda i: (i, 0))

    return pl.pallas_call(
        functools.partial(_dense_ffn_kernel, alpha=alpha, fc=_ffn_chunk(wg.shape[1])), grid=(r // tm,),
        in_specs=[row(d), row(p.shape[1])] + [_resident(a.shape) for a in w],
        out_specs=row(d), out_shape=jax.ShapeDtypeStruct((r, d), F32),
        compiler_params=_params(("arbitrary",), 56 << 20), name="ffn_dense",
    )(h, p, wg, wu, wd, wpg, wpp, g, bb)


MOE_ROW_TILE = 512
ROUTE_LANES = 6


def _route_kernel(h_ref, wr_ref, br_ref, route_ref, hp_ref, counts_ref, carry_ref):
    @pl.when(pl.program_id(0) == 0)
    def _init():
        carry_ref[...] = jnp.zeros(carry_ref.shape, F32)

    hb = h_ref[...].astype(BF16)
    tm, d = hb.shape
    lane = lax.broadcasted_iota(jnp.int32, (1, LANES), 1)
    logits = _dot(hb, wr_ref[...]) + br_ref[...]
    lg = jnp.where(lane < N_EXPERTS, logits, -jnp.inf)
    m1 = jnp.max(lg, axis=-1, keepdims=True)
    i1 = jnp.min(jnp.where(lg == m1, lane, LANES), axis=-1, keepdims=True)
    lg2 = jnp.where(lane == i1, -jnp.inf, lg)
    m2 = jnp.max(lg2, axis=-1, keepdims=True)
    i2 = jnp.min(jnp.where(lg2 == m2, lane, LANES), axis=-1, keepdims=True)
    e2 = jnp.exp(m2 - m1)
    den = 1.0 + e2
    hit1 = lane == i1
    hit2 = lane == i2
    onehot = jnp.where(hit1, 1.0, 0.0) + jnp.where(hit2, 1.0, 0.0)
    rr = lax.broadcasted_iota(jnp.int32, (tm, tm), 0)
    cc = lax.broadcasted_iota(jnp.int32, (tm, tm), 1)
    lower = jnp.where(cc < rr, 1.0, 0.0).astype(BF16)
    prefix = _dot(lower, onehot.astype(BF16)) + carry_ref[...]
    rank1 = jnp.sum(jnp.where(hit1, prefix, 0.0), axis=-1, keepdims=True)
    rank2 = jnp.sum(jnp.where(hit2, prefix, 0.0), axis=-1, keepdims=True)
    fields = (i1.astype(F32), i2.astype(F32), rank1, rank2, 1.0 / den, e2 / den)
    route = jnp.zeros((tm, LANES), F32)
    for j, f in enumerate(fields):
        route = jnp.where(lane == j, f, route)
    route_ref[...] = route
    carry = carry_ref[...] + jnp.sum(onehot, axis=0, keepdims=True)
    carry_ref[...] = carry
    counts_ref[...] = carry
    bits = lax.bitcast_convert_type(hb.astype(F32), jnp.uint32)
    hp_ref[...] = (bits[:, d // 2:] & jnp.uint32(0xFFFF0000)) | lax.shift_right_logical(bits[:, :d // 2], jnp.uint32(16))


def _route(h, wr, br):
    r, d = h.shape
    tm = _pick_tile(r, (512, 256, 128))
    return pl.pallas_call(
        _route_kernel, grid=(r // tm,),
        in_specs=[pl.BlockSpec((tm, d), lambda i: (i, 0)), _resident(wr.shape), _resident(br.shape)],
        out_specs=(pl.BlockSpec((tm, LANES), lambda i: (i, 0)), pl.BlockSpec((tm, d // 2), lambda i: (i, 0)),
                   pl.BlockSpec((1, LANES), lambda i: (0, 0))),
        out_shape=(jax.ShapeDtypeStruct((r, LANES), F32), jax.ShapeDtypeStruct((r, d // 2), jnp.uint32),
                   jax.ShapeDtypeStruct((1, LANES), F32)),
        scratch_shapes=[pltpu.VMEM((1, LANES), F32)],
        compiler_params=_params(("arbitrary",), 32 << 20), name="moe_route",
    )(h, wr, br)


ROW_DMA_UNROLL = 8


def _row_scatter_kernel(idx_ref, src_ref, init_ref, dst_ref, sem, *, tm):
    del init_ref

    def copy(t, d_row):
        return pltpu.make_async_copy(src_ref.at[pl.ds(t, 1)], dst_ref.at[pl.ds(d_row, 1)], sem)

    def issue(t, c):
        copy(t, idx_ref[0, 0, 2 * t]).start()
        copy(t, idx_ref[0, 0, 2 * t + 1]).start()
        return c

    lax.fori_loop(0, tm, issue, 0, unroll=ROW_DMA_UNROLL)
    for _ in range(2):
        pltpu.make_async_copy(src_ref, dst_ref.at[pl.ds(0, tm)], sem).wait()


def _row_scatter(idx3, src, init):
    steps, _, n = idx3.shape
    tm = n // 2
    any_spec = pl.BlockSpec(memory_space=pl.ANY)
    return pl.pallas_call(
        functools.partial(_row_scatter_kernel, tm=tm), grid=(steps,),
        in_specs=[pl.BlockSpec((1, 1, n), lambda i: (i, 0, 0), memory_space=pltpu.SMEM),
                  pl.BlockSpec((tm, src.shape[1]), lambda i: (i, 0)), any_spec],
        out_specs=any_spec, out_shape=jax.ShapeDtypeStruct(init.shape, init.dtype),
        scratch_shapes=[pltpu.SemaphoreType.DMA(())], input_output_aliases={2: 0},
        compiler_params=_params(("arbitrary",), 16 << 20), name="moe_scatter",
    )(idx3, src, init)


def _grouped_ffn_kernel(te_ref, tv_ref, x_ref, wg_ref, wu_ref, wd_ref, y_ref, *, fc):
    del te_ref
    valid = tv_ref[pl.program_id(0)] != 0

    @pl.when(valid)
    def _compute():
        pk = x_ref[...]
        lo = lax.bitcast_convert_type(lax.shift_left(pk, jnp.uint32(16)), F32)
        hi = lax.bitcast_convert_type(pk & jnp.uint32(0xFFFF0000), F32)
        xb = jnp.concatenate([lo, hi], axis=1).astype(BF16)
        n_ff = wg_ref.shape[2]
        acc = jnp.zeros(y_ref.shape, F32)
        for c in range(n_ff // fc):
            sl = slice(c * fc, (c + 1) * fc)
            gt = _dot(xb, wg_ref[0, :, sl])
            act = (gt * _sigmoid(gt) * _dot(xb, wu_ref[0, :, sl])).astype(BF16)
            acc = acc + _dot(act, wd_ref[0, sl, :])
        y_ref[...] = acc

    @pl.when(jnp.logical_not(valid))
    def _skip():
        y_ref[...] = jnp.zeros(y_ref.shape, F32)


def _grouped_ffn(tile_expert, tile_valid, xs, wg, wu, wd):
    rp, half = xs.shape
    n_e, d, n_ff = wg.shape
    tmx = MOE_ROW_TILE
    grid_spec = pltpu.PrefetchScalarGridSpec(
        num_scalar_prefetch=2, grid=(rp // tmx,),
        in_specs=[pl.BlockSpec((tmx, half), lambda i, te, tv: (i, 0)),
                  pl.BlockSpec((1, d, n_ff), lambda i, te, tv: (te[i], 0, 0)),
                  pl.BlockSpec((1, d, n_ff), lambda i, te, tv: (te[i], 0, 0)),
                  pl.BlockSpec((1, n_ff, d), lambda i, te, tv: (te[i], 0, 0))],
        out_specs=pl.BlockSpec((tmx, d), lambda i, te, tv: (i, 0)))
    return pl.pallas_call(
        functools.partial(_grouped_ffn_kernel, fc=_ffn_chunk(n_ff)), grid_spec=grid_spec,
        out_shape=jax.ShapeDtypeStruct((rp, d), F32),
        compiler_params=_params(("arbitrary",), V7X_SCOPED_VMEM_BYTES), name="moe_ffn",
    )(tile_expert, tile_valid, xs, wg, wu, wd)


def _moe_out_kernel(idx_ref, idx_next_ref, h_ref, p_ref, route_ref, ys_ref, wpg_ref, wpp_ref, g_ref, b_ref, y_ref,
                    buf_ref, sem, *, alpha):
    tm = h_ref.shape[0]
    step = pl.program_id(0)
    slot = lax.rem(step, 2)

    def copy(sl, k, t, s_row):
        return pltpu.make_async_copy(ys_ref.at[pl.ds(s_row, 1)], buf_ref.at[sl, k, pl.ds(t, 1)], sem.at[sl])

    def issue_block(rows_ref, sl):
        def issue(t, c):
            copy(sl, 0, t, rows_ref[0, 0, 2 * t]).start()
            copy(sl, 1, t, rows_ref[0, 0, 2 * t + 1]).start()
            return c

        lax.fori_loop(0, tm, issue, 0, unroll=ROW_DMA_UNROLL)

    @pl.when(step == 0)
    def _first():
        issue_block(idx_ref, 0)

    @pl.when(step + 1 < pl.num_programs(0))
    def _ahead():
        issue_block(idx_next_ref, 1 - slot)

    h = h_ref[...]
    hb = h.astype(BF16)
    ple = _ple(hb, p_ref, wpg_ref, wpp_ref)

    for k in range(2):
        pltpu.make_async_copy(ys_ref.at[pl.ds(0, tm)], buf_ref.at[slot, k], sem.at[slot]).wait()
    rt = route_ref[...]
    moe = rt[:, 4:5] * buf_ref[slot, 0] + rt[:, 5:6] * buf_ref[slot, 1]
    y_ref[...] = _layer_norm(alpha * h + moe + ple, g_ref[...], b_ref[...])


def _moe_out(idx3, h, p, route, ys, w, alpha):
    r, d = h.shape
    steps, _, n = idx3.shape
    tm = n // 2

    def row(width):
        return pl.BlockSpec((tm, width), lambda i: (i, 0))

    return pl.pallas_call(
        functools.partial(_moe_out_kernel, alpha=alpha), grid=(steps,),
        in_specs=[pl.BlockSpec((1, 1, n), lambda i: (i, 0, 0), memory_space=pltpu.SMEM),
                  pl.BlockSpec((1, 1, n), lambda i: (jnp.minimum(i + 1, steps - 1), 0, 0), memory_space=pltpu.SMEM),
                  row(d), row(p.shape[1]), row(LANES), pl.BlockSpec(memory_space=pl.ANY)]
        + [_resident(a.shape) for a in w],
        out_specs=row(d), out_shape=jax.ShapeDtypeStruct((r, d), F32),
        scratch_shapes=[pltpu.VMEM((2, 2, tm, d), F32), pltpu.SemaphoreType.DMA((2,))],
        compiler_params=_params(("arbitrary",), 48 << 20), name="moe_out",
    )(idx3, idx3, h, p, route, ys, *w)


def _moe_ffn(h, p, w, alpha):
    r, d = h.shape
    wr, br, wg, wu, wd, wpg, wpp, g, bb = w
    tmx = MOE_ROW_TILE
    route, hp, counts = _route(h, wr, br)
    cnt = counts[0, :N_EXPERTS].astype(jnp.int32)
    padded = (cnt + tmx - 1) // tmx * tmx
    ends = jnp.cumsum(padded)
    dest = jnp.take(ends - padded, route[:, 0:2].astype(jnp.int32)) + route[:, 2:4].astype(jnp.int32)
    n_tiles = -(-2 * r // tmx) + N_EXPERTS
    starts = jnp.arange(n_tiles, dtype=jnp.int32) * tmx
    tile_expert = jnp.minimum(jnp.sum(ends[None, :] <= starts[:, None], axis=1), N_EXPERTS - 1).astype(jnp.int32)
    tile_valid = (starts < ends[-1]).astype(jnp.int32)
    tm = _pick_tile(r, (512, 256, 128))
    dest3 = dest.reshape(r // tm, 1, 2 * tm)
    xs = _row_scatter(dest3, hp, jnp.zeros((n_tiles * tmx, d // 2), jnp.uint32))
    ys = _grouped_ffn(tile_expert, tile_valid, xs, wg, wu, wd)
    return _moe_out(dest3, h, p, route, ys, (wpg, wpp, g, bb), alpha)


def _pad_lanes(a, width=LANES):
    return jnp.pad(a, ((0, 0), (0, width - a.shape[1])))


def _rot_half_cols(a):
    half = a.shape[-1] // 2
    return jnp.concatenate([-a[..., half:], a[..., :half]], axis=-1)


def _mixer_weights(w_in, b_fox_f, g_cq, w_qb, g_ckv, w_kvb):
    d = w_in.shape[0]
    sizes = (FOX_W, FOX_W, FOX_W, FOX_HEADS, MLA_Q_RANK, MLA_KV_RANK, MLA_ROPE, d, d)
    cols, start = [], 0
    for n in sizes:
        cols.append(w_in[:, start:start + n])
        start += n
    wfq, wfk, wfv, wff, wcq, wckv, wkr, wga, wgb = cols
    w1 = jnp.concatenate([wfq * (FOX_HEAD_DIM ** -0.5 * LOG2E), wfk, wfv], axis=1).astype(BF16)
    w2 = jnp.concatenate([wcq, wckv, _pad_lanes(wkr), _pad_lanes(_rot_half_cols(wkr)), _pad_lanes(wff)],
                         axis=1).astype(BF16)
    wfft = jnp.pad(wff.T, ((0, 16 - FOX_HEADS), (0, 0))).astype(BF16)
    bf = b_fox_f.reshape(1, FOX_HEADS)
    bft = b_fox_f.reshape(FOX_HEADS, 1)

    qb = w_qb.reshape(MLA_Q_RANK, MLA_HEADS, MLA_NOPE + MLA_ROPE)
    q_nope, q_rope = qb[..., :MLA_NOPE], qb[..., MLA_NOPE:]
    q_rot = _rot_half_cols(q_rope)
    z_pad = jnp.zeros((MLA_Q_RANK, MLA_QK_W - 2 * MLA_NOPE - 2 * MLA_ROPE), F32)
    z_nope = jnp.zeros((MLA_Q_RANK, 2 * MLA_NOPE), F32)
    wqa = jnp.concatenate([jnp.concatenate([q_nope[:, 2 * j], q_nope[:, 2 * j + 1], q_rope[:, 2 * j],
                                            q_rope[:, 2 * j + 1], z_pad], axis=1) for j in range(N_PAIRS)], axis=1)
    wqb = jnp.concatenate([jnp.concatenate([z_nope, q_rot[:, 2 * j], q_rot[:, 2 * j + 1], z_pad], axis=1)
                           for j in range(N_PAIRS)], axis=1)

    kvb = w_kvb.reshape(MLA_KV_RANK, MLA_HEADS, MLA_NOPE + MLA_V)
    k_nope, v_up = kvb[..., :MLA_NOPE], kvb[..., MLA_NOPE:]
    zk = jnp.zeros((MLA_KV_RANK, MLA_QK_W - 2 * MLA_NOPE), F32)
    wk = jnp.concatenate([jnp.concatenate([k_nope[:, 2 * j], k_nope[:, 2 * j + 1], zk], axis=1)
                          for j in range(N_PAIRS)], axis=1).astype(BF16)
    eye = jnp.eye(MLA_ROPE, dtype=F32)
    place = jnp.concatenate([jnp.zeros((MLA_ROPE, 2 * MLA_NOPE), F32), eye, eye,
                             jnp.zeros((MLA_ROPE, MLA_QK_W - 2 * MLA_NOPE - 2 * MLA_ROPE), F32)], axis=1)
    place = jnp.tile(place, (1, N_PAIRS)).astype(BF16)
    wv = v_up.reshape(MLA_KV_RANK, MLA_W).astype(BF16)
    proj_w = (w1, w2, wfft, wfv.T.astype(BF16), bf, bft, g_cq.reshape(1, -1), g_ckv.reshape(1, -1),
              wqa.astype(BF16), wqb.astype(BF16))
    return proj_w, (wk, place, wv, wv.T), (wga.astype(BF16), wgb.astype(BF16))


def _rope_tables(pos):
    half = MLA_ROPE // 2
    inv = ROPE_THETA ** (-jnp.arange(half, dtype=F32) * 2.0 / MLA_ROPE)
    ang = pos.astype(F32)[:, None] * inv[None, :]
    cos2 = jnp.concatenate([jnp.cos(ang)] * 2, axis=1)
    sin2 = jnp.concatenate([jnp.sin(ang)] * 2, axis=1)
    n = pos.shape[0]
    scale = (MLA_NOPE + MLA_ROPE) ** -0.5 * LOG2E
    pad = jnp.zeros((n, MLA_QK_W - 2 * MLA_NOPE - 2 * MLA_ROPE), F32)
    ctab = jnp.concatenate([jnp.full((n, 2 * MLA_NOPE), scale, F32), scale * cos2, scale * cos2, pad], axis=1)
    stab = jnp.concatenate([jnp.zeros((n, 2 * MLA_NOPE), F32), scale * sin2, scale * sin2, pad], axis=1)
    return jnp.tile(ctab, (1, N_PAIRS)), jnp.tile(stab, (1, N_PAIRS)), cos2, sin2


def _pad_time(a, tp, axis):
    pad = [(0, 0)] * a.ndim
    pad[axis] = (0, tp - a.shape[axis])
    return jnp.pad(a, pad)


def _layer(x, past, p, proj_w, kvx_w, mix_w, ffn_w, is_moe, alpha):
    b, s, d = x.shape
    n_past = 0 if past is None else past[0].shape[1]
    t = n_past + s
    tabs = _rope_tables(n_past + jnp.arange(s))
    fq, fk, fv, fkb, fvb, fvt, logf, logft, ckv, kr, qp = _proj(x, proj_w, tabs)
    if past is None:
        tp = t
        k_fox, logft_all, ckv_all, kr_all = fkb, logft, ckv, kr
    else:
        tp = -(-t // LANES) * LANES
        pk, pv, plogf, pckv, pkr = past
        k_fox = _pad_time(jnp.concatenate([pk.reshape(b, n_past, FOX_W).astype(BF16), fkb], axis=1), tp, 1)
        v_fox = _pad_time(jnp.concatenate([pv.reshape(b, n_past, FOX_W).astype(BF16), fvb], axis=1), tp, 1)
        logft_all = _pad_time(jnp.concatenate([jnp.swapaxes(plogf, 1, 2), logft], axis=2), tp, 2)
        ckv_all = _pad_time(jnp.concatenate([pckv, ckv], axis=1), tp, 1)
        kr_all = _pad_time(jnp.concatenate([pkr, kr], axis=1), tp, 1)
    nb, nbrep = _neg_cumsum(logft_all)
    k_mla, v_mla, vt_mla = _kv_expand(ckv_all, kr_all, *kvx_w)
    if past is None:
        o_fox = _attention_t(fq, k_fox, fvt, nbrep, q_off=0, t_valid=t, chunked=False)
        o_mla = _attention_t(qp, k_mla, vt_mla, None, q_off=0, t_valid=t, chunked=True)
    else:
        nb = nb.reshape(b, N_PAIRS, 2, tp)
        o_fox = _attention(fq, k_fox, v_fox, nb, q_off=n_past, t_valid=t, chunked=False)
        o_mla = _attention(qp, k_mla, v_mla, None, q_off=n_past, t_valid=t, chunked=True)
    r = b * s
    h = _mix(x.reshape(r, d), o_fox.reshape(r, FOX_W), o_mla.reshape(r, MLA_W), mix_w, alpha)
    ffn = _moe_ffn if is_moe else _dense_ffn
    y = ffn(h, p.reshape(r, -1), ffn_w, alpha).reshape(b, s, d)
    new_rows = (fk.reshape(b, s, FOX_HEADS, FOX_HEAD_DIM), fv.reshape(b, s, FOX_HEADS, FOX_HEAD_DIM), logf, ckv, kr)
    return y, new_rows


def kernel(x_prompt, x_sample, cache_fox_k, cache_fox_v, cache_fox_logf, cache_mla_ckv, cache_mla_krope,
           p_prompt, p_sample, w_in, b_fox_f, g_mla_cq, w_mla_qb, g_mla_ckv, w_mla_kvb, w_o_fox, w_o_mla,
           w_out, ln_mix_g, ln_mix_b, w_ffn_gate, w_ffn_up, w_ffn_down, w_router, b_router, w_moe_gate,
           w_moe_up, w_moe_down, w_ple_proj, w_ple_gate, ln_ffn_g, ln_ffn_b):
    depth = w_in.shape[0]
    alpha = (2 * depth) ** 0.25
    hp, hs = x_prompt, x_sample
    rows_p, rows_s = [], []
    for i in range(depth):
        proj_w, kvx_w, (wga, wgb) = _mixer_weights(w_in[i], b_fox_f[i], g_mla_cq[i], w_mla_qb[i], g_mla_ckv[i],
                                                   w_mla_kvb[i])
        mix_w = (wga, wgb, w_o_fox[i].astype(BF16), w_o_mla[i].astype(BF16), w_out[i].astype(BF16),
                 ln_mix_g[i].reshape(1, -1), ln_mix_b[i].reshape(1, -1))
        tail = (w_ple_gate[i].astype(BF16), w_ple_proj[i].astype(BF16),
                ln_ffn_g[i].reshape(1, -1), ln_ffn_b[i].reshape(1, -1))
        j = i // 2
        is_moe = i % 2 == 1
        if is_moe:
            ffn_w = (_pad_lanes(w_router[j]).astype(BF16), _pad_lanes(b_router[j].reshape(1, -1)),
                     w_moe_gate[j].astype(BF16), w_moe_up[j].astype(BF16), w_moe_down[j].astype(BF16)) + tail
        else:
            ffn_w = (w_ffn_gate[j].astype(BF16), w_ffn_up[j].astype(BF16), w_ffn_down[j].astype(BF16)) + tail
        past = (cache_fox_k[i], cache_fox_v[i], cache_fox_logf[i], cache_mla_ckv[i], cache_mla_krope[i])
        hp, new_p = _layer(hp, None, p_prompt[i], proj_w, kvx_w, mix_w, ffn_w, is_moe, alpha)
        hs, new_s = _layer(hs, past, p_sample[i], proj_w, kvx_w, mix_w, ffn_w, is_moe, alpha)
        rows_p.append(new_p)
        rows_s.append(new_s)

    def stack(rows, idx):
        return jnp.stack([r[idx] for r in rows], axis=0)

    return (hp, hs) + tuple(stack(rows_p, k) for k in range(5)) + tuple(stack(rows_s, k) for k in range(5))
```

```python
import functools

import jax
import jax.numpy as jnp
from jax import lax
from jax.experimental import pallas as pl
from jax.experimental.pallas import tpu as pltpu

CHUNK = 64
FOX_HEADS = 8
FOX_HEAD_DIM = 64
FOX_W = FOX_HEADS * FOX_HEAD_DIM
MLA_HEADS = 8
MLA_Q_RANK = 256
MLA_KV_RANK = 128
MLA_NOPE = 64
MLA_ROPE = 32
MLA_V = 64
MLA_W = MLA_HEADS * MLA_V
ROPE_THETA = 10000.0
N_EXPERTS = 8
LN_EPS = 1e-5
RMS_EPS = 1e-6
NEG_INF = -1e30
LOG2E = 1.4426950408889634

LANES = 128
PAIR_W = 2 * MLA_V
MLA_QK_W = 256
N_PAIRS = FOX_HEADS // 2
V7X_SCOPED_VMEM_BYTES = 60000 * 1024

BF16 = jnp.bfloat16
F32 = jnp.float32


def _dot(a, b):
    return jnp.dot(a, b, preferred_element_type=F32)


def _dot_nt(a, b):
    return lax.dot_general(a, b, (((1,), (1,)), ((), ())), preferred_element_type=F32)


def _sigmoid(x):
    return 1.0 / (1.0 + jnp.exp(-x))


def _log_sigmoid(x):
    return jnp.minimum(x, 0.0) - jnp.log1p(jnp.exp(-jnp.abs(x)))


def _rms_norm(x, g):
    return x * lax.rsqrt(jnp.mean(jnp.square(x), axis=-1, keepdims=True) + RMS_EPS) * g


def _layer_norm(x, g, b):
    mu = jnp.mean(x, axis=-1, keepdims=True)
    xc = x - mu
    var = jnp.mean(jnp.square(xc), axis=-1, keepdims=True)
    return xc * lax.rsqrt(var + LN_EPS) * g + b


def _resident(shape):
    nd = len(shape)
    return pl.BlockSpec(shape, lambda *_: (0,) * nd, pipeline_mode=pl.Buffered(1))


def _params(semantics, vmem_bytes):
    return pltpu.CompilerParams(dimension_semantics=semantics,
                                vmem_limit_bytes=min(int(vmem_bytes), V7X_SCOPED_VMEM_BYTES))


def _pick_tile(n, candidates):
    for c in candidates:
        if n % c == 0:
            return c
    return n


def _proj_kernel(x_ref, w1_ref, w2_ref, wfft_ref, wfvt_ref, bf_ref, bft_ref, gcq_ref, gckv_ref, wqa_ref, wqb_ref,
                 ctab_ref, stab_ref, cos_ref, sin_ref,
                 fq_ref, fk_ref, fv_ref, fkb_ref, fvt_ref, logf_ref, logft_ref, ckv_ref, kr_ref, qp_ref):
    xb = x_ref[0].astype(BF16)
    z1 = _dot(xb, w1_ref[...])
    fq_ref[0] = z1[:, :FOX_W].astype(BF16)
    fk = z1[:, FOX_W:2 * FOX_W]
    fk_ref[0] = fk
    fkb_ref[0] = fk.astype(BF16)
    fv_ref[0] = z1[:, 2 * FOX_W:]
    fvt_ref[0] = _dot_nt(wfvt_ref[...], xb).astype(BF16)

    z2 = _dot(xb, w2_ref[...])
    cq = z2[:, :MLA_Q_RANK]
    o = MLA_Q_RANK
    ckv = z2[:, o:o + MLA_KV_RANK]
    o += MLA_KV_RANK
    kr = z2[:, o:o + MLA_ROPE]
    krr = z2[:, o + LANES:o + LANES + MLA_ROPE]
    ff = z2[:, o + 2 * LANES:o + 2 * LANES + FOX_HEADS]
    logf_ref[0] = _log_sigmoid(ff + bf_ref[...])
    fft = _dot_nt(wfft_ref[...], xb)
    logft_ref[0] = _log_sigmoid(fft[:FOX_HEADS] + bft_ref[...])
    ckv_ref[0] = _rms_norm(ckv, gckv_ref[...])
    kr_ref[0] = kr * cos_ref[...] + krr * sin_ref[...]
    cqn = _rms_norm(cq, gcq_ref[...]).astype(BF16)
    qp = _dot(cqn, wqa_ref[...]) * ctab_ref[...] + _dot(cqn, wqb_ref[...]) * stab_ref[...]
    qp_ref[0] = qp.astype(BF16)


def _proj(x, w, tabs):
    b, s, d = x.shape
    tm = _pick_tile(s, (512, 256, 128))
    ns = s // tm
    w1, w2, wfft, wfvt, bf, bft, gcq, gckv, wqa, wqb = w
    ctab, stab, cos2, sin2 = tabs
    qw = N_PAIRS * MLA_QK_W

    def tok(width):
        return pl.BlockSpec((1, tm, width), lambda si, bi: (bi, si, 0))

    def tok_t(height):
        return pl.BlockSpec((1, height, tm), lambda si, bi: (bi, 0, si))

    def tab(width):
        return pl.BlockSpec((tm, width), lambda si, bi: (si, 0))

    in_specs = [tok(d), _resident(w1.shape), _resident(w2.shape), _resident(wfft.shape), _resident(wfvt.shape),
                _resident(bf.shape), _resident(bft.shape), _resident(gcq.shape), _resident(gckv.shape),
                _resident(wqa.shape), _resident(wqb.shape), tab(qw), tab(qw), tab(MLA_ROPE), tab(MLA_ROPE)]
    out_shape = (
        jax.ShapeDtypeStruct((b, s, FOX_W), BF16),
        jax.ShapeDtypeStruct((b, s, FOX_W), F32),
        jax.ShapeDtypeStruct((b, s, FOX_W), F32),
        jax.ShapeDtypeStruct((b, s, FOX_W), BF16),
        jax.ShapeDtypeStruct((b, FOX_W, s), BF16),
        jax.ShapeDtypeStruct((b, s, FOX_HEADS), F32),
        jax.ShapeDtypeStruct((b, FOX_HEADS, s), F32),
        jax.ShapeDtypeStruct((b, s, MLA_KV_RANK), F32),
        jax.ShapeDtypeStruct((b, s, MLA_ROPE), F32),
        jax.ShapeDtypeStruct((b, s, qw), BF16),
    )
    out_specs = (tok(FOX_W), tok(FOX_W), tok(FOX_W), tok(FOX_W), tok_t(FOX_W), tok(FOX_HEADS),
                 tok_t(FOX_HEADS), tok(MLA_KV_RANK), tok(MLA_ROPE), tok(qw))
    return pl.pallas_call(
        _proj_kernel, grid=(ns, b), in_specs=in_specs, out_specs=out_specs, out_shape=out_shape,
        compiler_params=_params(("arbitrary", "arbitrary"), 48 << 20), name="proj",
    )(x, w1, w2, wfft, wfvt, bf, bft, gcq, gckv, wqa, wqb, ctab, stab, cos2, sin2)


def _cumsum_kernel(x_ref, aug_ref, *, ch):
    t = x_ref.shape[2]
    lane = lax.broadcasted_iota(jnp.int32, (1, LANES), 1)
    r = lax.broadcasted_iota(jnp.int32, (ch, ch), 0)
    c = lax.broadcasted_iota(jnp.int32, (ch, ch), 1)
    upper = jnp.where(r <= c, 1.0, 0.0).astype(BF16)
    carry = jnp.zeros((FOX_HEADS, 1), F32)
    zeros = jnp.zeros((FOX_HEADS, ch), F32)
    for ci in range(t // ch):
        xc = x_ref[0, :, ci * ch:(ci + 1) * ch]
        hi = xc.astype(BF16).astype(F32)
        r1 = xc - hi
        mid = r1.astype(BF16).astype(F32)
        lo = (r1 - mid).astype(BF16).astype(F32)
        pieces = jnp.concatenate([hi, mid, lo, zeros], axis=0).astype(BF16)
        pc = _dot(pieces, upper)
        cum = pc[0:8] + pc[8:16] + pc[16:24] + carry
        nb = cum * (-LOG2E)
        for hh in range(FOX_HEADS):
            rep = jnp.broadcast_to(nb[hh:hh + 1, :], (LANES, ch)).T
            a_hi = rep.astype(BF16).astype(F32)
            a_r = rep - a_hi
            a_mid = a_r.astype(BF16).astype(F32)
            a_lo = (a_r - a_mid).astype(BF16).astype(F32)
            aug = jnp.where(lane == 0, a_hi, jnp.where(lane == 1, a_mid, jnp.where(lane == 2, a_lo, 0.0)))
            aug_ref[0, hh, ci * ch:(ci + 1) * ch, :] = aug.astype(BF16)
        carry = cum[:, ch - 1:ch]


def _neg_cumsum(logft):
    b, h, t = logft.shape
    ch = 256 if t % 256 == 0 else LANES
    spec = pl.BlockSpec((1, h, t), lambda bi: (bi, 0, 0))
    return pl.pallas_call(
        functools.partial(_cumsum_kernel, ch=ch), grid=(b,), in_specs=[spec],
        out_specs=pl.BlockSpec((1, h, t, LANES), lambda bi: (bi, 0, 0, 0)),
        out_shape=jax.ShapeDtypeStruct((b, h, t, LANES), BF16),
        compiler_params=_params(("arbitrary",), 32 << 20), name="cumsum",
    )(logft)


def _kvx_kernel(ckv_ref, kr_ref, wk_ref, place_ref, wvt_ref, kp_ref, vmt_ref):
    cb = ckv_ref[0].astype(BF16)
    kp = _dot(cb, wk_ref[...]) + _dot(kr_ref[0].astype(BF16), place_ref[...])
    kp_ref[0] = kp.astype(BF16)
    vmt_ref[0] = _dot_nt(wvt_ref[...], cb).astype(BF16)


def _kv_expand(ckv, kr, wk, place, wvt):
    b, t, _ = ckv.shape
    tm = _pick_tile(t, (512, 384, 256, 128))
    kw = N_PAIRS * MLA_QK_W

    def tok(width):
        return pl.BlockSpec((1, tm, width), lambda bi, ti: (bi, ti, 0))

    return pl.pallas_call(
        _kvx_kernel, grid=(b, t // tm),
        in_specs=[tok(MLA_KV_RANK), tok(MLA_ROPE), _resident(wk.shape), _resident(place.shape), _resident(wvt.shape)],
        out_specs=(tok(kw), pl.BlockSpec((1, MLA_W, tm), lambda bi, ti: (bi, 0, ti))),
        out_shape=(jax.ShapeDtypeStruct((b, t, kw), BF16), jax.ShapeDtypeStruct((b, MLA_W, t), BF16)),
        compiler_params=_params(("arbitrary", "arbitrary"), 32 << 20), name="kvexpand",
    )(ckv, kr, wk, place, wvt)


def _head_lane_mask(lane, hh, width):
    if width == LANES:
        return (lane // FOX_HEAD_DIM) == hh
    nope = (lane < 2 * MLA_NOPE) & ((lane // MLA_NOPE) == hh)
    rope = (lane >= 2 * MLA_NOPE) & (lane < 2 * MLA_NOPE + 2 * MLA_ROPE) & (
        ((lane - 2 * MLA_NOPE) // MLA_ROPE) == hh)
    return nope | rope


def _attn_t_kernel(*refs, tq, tk, width, q_off, t_valid, chunked, has_bias):
    if has_bias:
        q_ref, k_ref, vt_ref, nb_ref, o_ref, sa_ref, sb_ref = refs
    else:
        q_ref, k_ref, vt_ref, o_ref, sa_ref, sb_ref = refs
        nb_ref = None
    nq = q_ref.shape[1] // tq
    nkb = k_ref.shape[1] // tk

    items = []
    for qi in range(nq):
        q_min = q_off + qi * tq
        q_max = q_min + tq - 1
        if chunked:
            lim_min, lim_max = (q_min // CHUNK + 1) * CHUNK, (q_max // CHUNK + 1) * CHUNK
        else:
            lim_min, lim_max = q_min + 1, q_max + 1
        n_full = min(lim_min, t_valid) // tk
        n_vis = min(-(-min(lim_max, t_valid) // tk), nkb)
        items += [(qi, kb, kb >= n_full, kb == n_vis - 1) for kb in range(n_vis)]

    lane = lax.broadcasted_iota(jnp.int32, (1, width), 1)
    keep = [_head_lane_mask(lane, hh, width) for hh in range(2)]
    bufs = (sa_ref, sb_ref)

    ones3 = jnp.where(lax.broadcasted_iota(jnp.int32, (tq, LANES), 1) < 3, 1.0, 0.0).astype(BF16)
    ones_rows = jnp.ones((16, tk), BF16)

    def scores_into(s_ref, qi, kb):
        q = q_ref[0, qi * tq:(qi + 1) * tq, :]
        kblk = k_ref[0, kb * tk:(kb + 1) * tk, :]
        for hh in range(2):
            qh = jnp.where(keep[hh], q, jnp.zeros_like(q))
            if has_bias:
                kh = jnp.concatenate([kblk, nb_ref[0, hh, kb * tk:(kb + 1) * tk, :]], axis=1)
                s_ref[hh] = _dot_nt(kh, jnp.concatenate([qh, ones3], axis=1))
            else:
                s_ref[hh] = _dot_nt(kblk, qh)

    def consume(s_ref, qi, kb, stats, masked):
        if masked:
            qpos = q_off + qi * tq + lax.broadcasted_iota(jnp.int32, (1, tq), 1)
            if chunked:
                col_lim = lax.shift_left(lax.shift_right_logical(qpos, 6) + 1, 6)
            else:
                col_lim = qpos + 1
            col_lim = jnp.minimum(col_lim, t_valid)
            kpos = kb * tk + lax.broadcasted_iota(jnp.int32, (tk, 1), 0)
        out = []
        for hh in range(2):
            m, acc = stats[2 * hh:2 * hh + 2]
            s = s_ref[hh]
            if masked:
                s = jnp.where(kpos < col_lim, s, NEG_INF)
            m_new = jnp.maximum(m, jnp.max(s, axis=0, keepdims=True))
            p = jnp.exp2(s - m_new)
            alpha = jnp.exp2(m - m_new)
            vth = jnp.concatenate([vt_ref[0, hh * MLA_V:(hh + 1) * MLA_V, kb * tk:(kb + 1) * tk], ones_rows], axis=0)
            acc = alpha * acc + _dot(vth, p.astype(BF16))
            out += [m_new, acc]
        return tuple(out)

    scores_into(bufs[0], items[0][0], items[0][1])
    stats = None
    for i, (qi, kb, masked, last) in enumerate(items):
        if i + 1 < len(items):
            scores_into(bufs[(i + 1) % 2], items[i + 1][0], items[i + 1][1])
        if kb == 0:
            stats = (jnp.full((1, tq), NEG_INF, F32), jnp.zeros((MLA_V + 16, tq), F32)) * 2
        stats = consume(bufs[i % 2], qi, kb, stats, masked)
        if last:
            outs = [stats[2 * hh + 1][:MLA_V] * (1.0 / stats[2 * hh + 1][MLA_V:MLA_V + 1]) for hh in range(2)]
            o_ref[0, qi * tq:(qi + 1) * tq, :] = jnp.concatenate(outs, axis=0).T.astype(BF16)


def _attention_t(q, k, vt, nbrep, *, q_off, t_valid, chunked):
    assert CHUNK == 64
    b, sq, qw = q.shape
    width = qw // N_PAIRS
    tp = k.shape[1]
    tq = _pick_tile(sq, (512, 256, 128))
    tk = _pick_tile(tp, (512, 384, 256, 128))
    in_specs = [pl.BlockSpec((1, sq, width), lambda bi, pi: (bi, 0, pi)),
                pl.BlockSpec((1, tp, width), lambda bi, pi: (bi, 0, pi)),
                pl.BlockSpec((1, PAIR_W, tp), lambda bi, pi: (bi, pi, 0))]
    args = [q, k, vt]
    if nbrep is not None:
        in_specs.append(pl.BlockSpec((1, 2, tp, LANES), lambda bi, pi: (bi, pi, 0, 0)))
        args.append(nbrep)
    kern = functools.partial(_attn_t_kernel, tq=tq, tk=tk, width=width, q_off=q_off, t_valid=t_valid,
                             chunked=chunked, has_bias=nbrep is not None)
    return pl.pallas_call(
        kern, grid=(b, N_PAIRS), in_specs=in_specs,
        out_specs=pl.BlockSpec((1, sq, PAIR_W), lambda bi, pi: (bi, 0, pi)),
        out_shape=jax.ShapeDtypeStruct((b, sq, N_PAIRS * PAIR_W), BF16),
        scratch_shapes=[pltpu.VMEM((2, tk, tq), F32), pltpu.VMEM((2, tk, tq), F32)],
        compiler_params=_params(("arbitrary", "arbitrary"), 40 << 20),
        name="attn_t_mla" if chunked else "attn_t_fox",
    )(*args)


def _mix_kernel(x_ref, of_ref, om_ref, wga_ref, wgb_ref, wof_ref, wom_ref, wout_ref, g_ref, b_ref, h_ref, *, alpha):
    x = x_ref[...]
    xb = x.astype(BF16)
    a = _sigmoid(_dot(xb, wga_ref[...])) * _dot(of_ref[...], wof_ref[...])
    bm = _sigmoid(_dot(xb, wgb_ref[...])) * _dot(om_ref[...], wom_ref[...])
    m = _dot((a + bm).astype(BF16), wout_ref[...])
    h_ref[...] = _layer_norm(alpha * x + m, g_ref[...], b_ref[...])


def _mix(x, of, om, w, alpha):
    r, d = x.shape
    tm = _pick_tile(r, (512, 256, 128))
    wga, wgb, wof, wom, wout, g, bb = w

    def row(width):
        return pl.BlockSpec((tm, width), lambda i: (i, 0))

    return pl.pallas_call(
        functools.partial(_mix_kernel, alpha=alpha), grid=(r // tm,),
        in_specs=[row(d), row(FOX_W), row(MLA_W)] + [_resident(a.shape) for a in w],
        out_specs=row(d), out_shape=jax.ShapeDtypeStruct((r, d), F32),
        compiler_params=_params(("arbitrary",), 48 << 20), name="mix",
    )(x, of, om, wga, wgb, wof, wom, wout, g, bb)


def _ple(hb, p_ref, wpg_ref, wpp_ref):
    return _sigmoid(_dot(hb, wpg_ref[...])) * _dot(p_ref[...].astype(BF16), wpp_ref[...])


def _ffn_chunk(n_ff):
    return _pick_tile(n_ff, (1408, 1024, 512, 256, 128))


def _dense_ffn_kernel(h_ref, p_ref, wg_ref, wu_ref, wd_ref, wpg_ref, wpp_ref, g_ref, b_ref, y_ref, *, alpha, fc):
    h = h_ref[...]
    hb = h.astype(BF16)
    n_ff = wg_ref.shape[1]
    acc = jnp.zeros(h.shape, F32)
    for c in range(n_ff // fc):
        sl = slice(c * fc, (c + 1) * fc)
        gt = _dot(hb, wg_ref[:, sl])
        act = (gt * _sigmoid(gt) * _dot(hb, wu_ref[:, sl])).astype(BF16)
        acc = acc + _dot(act, wd_ref[sl, :])
    y = alpha * h + acc + _ple(hb, p_ref, wpg_ref, wpp_ref)
    y_ref[...] = _layer_norm(y, g_ref[...], b_ref[...])


def _dense_ffn(h, p, w, alpha):
    r, d = h.shape
    tm = _pick_tile(r, (512, 256, 128))
    wg, wu, wd, wpg, wpp, g, bb = w

    def row(width):
        return pl.BlockSpec((tm, width), lambda i: (i, 0))

    return pl.pallas_call(
        functools.partial(_dense_ffn_kernel, alpha=alpha, fc=_ffn_chunk(wg.shape[1])), grid=(r // tm,),
        in_specs=[row(d), row(p.shape[1])] + [_resident(a.shape) for a in w],
        out_specs=row(d), out_shape=jax.ShapeDtypeStruct((r, d), F32),
        compiler_params=_params(("arbitrary",), 56 << 20), name="ffn_dense",
    )(h, p, wg, wu, wd, wpg, wpp, g, bb)


MOE_ROW_TILE = 512
ROUTE_LANES = 6


def _route_kernel(h_ref, wr_ref, br_ref, route_ref, hp_ref, counts_ref, carry_ref):
    @pl.when(pl.program_id(0) == 0)
    def _init():
        carry_ref[...] = jnp.zeros(carry_ref.shape, F32)

    hb = h_ref[...].astype(BF16)
    tm, d = hb.shape
    lane = lax.broadcasted_iota(jnp.int32, (1, LANES), 1)
    logits = _dot(hb, wr_ref[...]) + br_ref[...]
    lg = jnp.where(lane < N_EXPERTS, logits, -jnp.inf)
    m1 = jnp.max(lg, axis=-1, keepdims=True)
    i1 = jnp.min(jnp.where(lg == m1, lane, LANES), axis=-1, keepdims=True)
    lg2 = jnp.where(lane == i1, -jnp.inf, lg)
    m2 = jnp.max(lg2, axis=-1, keepdims=True)
    i2 = jnp.min(jnp.where(lg2 == m2, lane, LANES), axis=-1, keepdims=True)
    e2 = jnp.exp(m2 - m1)
    den = 1.0 + e2
    hit1 = lane == i1
    hit2 = lane == i2
    onehot = jnp.where(hit1, 1.0, 0.0) + jnp.where(hit2, 1.0, 0.0)
    rr = lax.broadcasted_iota(jnp.int32, (tm, tm), 0)
    cc = lax.broadcasted_iota(jnp.int32, (tm, tm), 1)
    lower = jnp.where(cc < rr, 1.0, 0.0).astype(BF16)
    prefix = _dot(lower, onehot.astype(BF16)) + carry_ref[...]
    rank1 = jnp.sum(jnp.where(hit1, prefix, 0.0), axis=-1, keepdims=True)
    rank2 = jnp.sum(jnp.where(hit2, prefix, 0.0), axis=-1, keepdims=True)
    fields = (i1.astype(F32), i2.astype(F32), rank1, rank2, 1.0 / den, e2 / den)
    route = jnp.zeros((tm, LANES), F32)
    for j, f in enumerate(fields):
        route = jnp.where(lane == j, f, route)
    route_ref[...] = route
    carry = carry_ref[...] + jnp.sum(onehot, axis=0, keepdims=True)
    carry_ref[...] = carry
    counts_ref[...] = carry
    bits = lax.bitcast_convert_type(hb.astype(F32), jnp.uint32)
    hp_ref[...] = (bits[:, d // 2:] & jnp.uint32(0xFFFF0000)) | lax.shift_right_logical(bits[:, :d // 2], jnp.uint32(16))


def _route(h, wr, br):
    r, d = h.shape
    tm = _pick_tile(r, (512, 256, 128))
    return pl.pallas_call(
        _route_kernel, grid=(r // tm,),
        in_specs=[pl.BlockSpec((tm, d), lambda i: (i, 0)), _resident(wr.shape), _resident(br.shape)],
        out_specs=(pl.BlockSpec((tm, LANES), lambda i: (i, 0)), pl.BlockSpec((tm, d // 2), lambda i: (i, 0)),
                   pl.BlockSpec((1, LANES), lambda i: (0, 0))),
        out_shape=(jax.ShapeDtypeStruct((r, LANES), F32), jax.ShapeDtypeStruct((r, d // 2), jnp.uint32),
                   jax.ShapeDtypeStruct((1, LANES), F32)),
        scratch_shapes=[pltpu.VMEM((1, LANES), F32)],
        compiler_params=_params(("arbitrary",), 32 << 20), name="moe_route",
    )(h, wr, br)


ROW_DMA_UNROLL = 8


def _row_scatter_kernel(idx_ref, src_ref, init_ref, dst_ref, sem, *, tm):
    del init_ref

    def copy(t, d_row):
        return pltpu.make_async_copy(src_ref.at[pl.ds(t, 1)], dst_ref.at[pl.ds(d_row, 1)], sem)

    def issue(t, c):
        copy(t, idx_ref[0, 0, 2 * t]).start(priority=0)
        copy(t, idx_ref[0, 0, 2 * t + 1]).start(priority=1)
        return c

    lax.fori_loop(0, tm, issue, 0, unroll=ROW_DMA_UNROLL)
    for _ in range(2):
        pltpu.make_async_copy(src_ref, dst_ref.at[pl.ds(0, tm)], sem).wait()


def _row_scatter(idx3, src, init):
    steps, _, n = idx3.shape
    tm = n // 2
    any_spec = pl.BlockSpec(memory_space=pl.ANY)
    return pl.pallas_call(
        functools.partial(_row_scatter_kernel, tm=tm), grid=(steps,),
        in_specs=[pl.BlockSpec((1, 1, n), lambda i: (i, 0, 0), memory_space=pltpu.SMEM),
                  pl.BlockSpec((tm, src.shape[1]), lambda i: (i, 0)), any_spec],
        out_specs=any_spec, out_shape=jax.ShapeDtypeStruct(init.shape, init.dtype),
        scratch_shapes=[pltpu.SemaphoreType.DMA(())], input_output_aliases={2: 0},
        compiler_params=_params(("arbitrary",), 16 << 20), name="moe_scatter",
    )(idx3, src, init)


def _grouped_ffn_kernel(te_ref, tv_ref, x_ref, wg_ref, wu_ref, wd_ref, y_ref, *, fc):
    del te_ref
    valid = tv_ref[pl.program_id(0)] != 0

    @pl.when(valid)
    def _compute():
        pk = x_ref[...]
        lo = lax.bitcast_convert_type(lax.shift_left(pk, jnp.uint32(16)), F32)
        hi = lax.bitcast_convert_type(pk & jnp.uint32(0xFFFF0000), F32)
        xb = jnp.concatenate([lo, hi], axis=1).astype(BF16)
        n_ff = wg_ref.shape[2]
        acc = jnp.zeros(y_ref.shape, F32)
        for c in range(n_ff // fc):
            sl = slice(c * fc, (c + 1) * fc)
            gt = _dot(xb, wg_ref[0, :, sl])
            act = (gt * _sigmoid(gt) * _dot(xb, wu_ref[0, :, sl])).astype(BF16)
            acc = acc + _dot(act, wd_ref[0, sl, :])
        y_ref[...] = acc

    @pl.when(jnp.logical_not(valid))
    def _skip():
        y_ref[...] = jnp.zeros(y_ref.shape, F32)


def _grouped_ffn(tile_expert, tile_valid, xs, wg, wu, wd):
    rp, half = xs.shape
    n_e, d, n_ff = wg.shape
    tmx = MOE_ROW_TILE
    grid_spec = pltpu.PrefetchScalarGridSpec(
        num_scalar_prefetch=2, grid=(rp // tmx,),
        in_specs=[pl.BlockSpec((tmx, half), lambda i, te, tv: (i, 0)),
                  pl.BlockSpec((1, d, n_ff), lambda i, te, tv: (te[i], 0, 0)),
                  pl.BlockSpec((1, d, n_ff), lambda i, te, tv: (te[i], 0, 0)),
                  pl.BlockSpec((1, n_ff, d), lambda i, te, tv: (te[i], 0, 0))],
        out_specs=pl.BlockSpec((tmx, d), lambda i, te, tv: (i, 0)))
    return pl.pallas_call(
        functools.partial(_grouped_ffn_kernel, fc=_ffn_chunk(n_ff)), grid_spec=grid_spec,
        out_shape=jax.ShapeDtypeStruct((rp, d), F32),
        compiler_params=_params(("arbitrary",), V7X_SCOPED_VMEM_BYTES), name="moe_ffn",
    )(tile_expert, tile_valid, xs, wg, wu, wd)


def _moe_out_kernel(idx_ref, idx_next_ref, h_ref, p_ref, route_ref, ys_ref, wpg_ref, wpp_ref, g_ref, b_ref, y_ref,
                    buf_ref, sem, *, alpha):
    tm = h_ref.shape[0]
    step = pl.program_id(0)
    slot = lax.rem(step, 2)

    def copy(sl, k, t, s_row):
        return pltpu.make_async_copy(ys_ref.at[pl.ds(s_row, 1)], buf_ref.at[sl, k, pl.ds(t, 1)], sem.at[sl])

    def issue_block(rows_ref, sl):
        def issue(t, c):
            copy(sl, 0, t, rows_ref[0, 0, 2 * t]).start(priority=0)
            copy(sl, 1, t, rows_ref[0, 0, 2 * t + 1]).start(priority=1)
            return c

        lax.fori_loop(0, tm, issue, 0, unroll=ROW_DMA_UNROLL)

    @pl.when(step == 0)
    def _first():
        issue_block(idx_ref, 0)

    @pl.when(step + 1 < pl.num_programs(0))
    def _ahead():
        issue_block(idx_next_ref, 1 - slot)

    h = h_ref[...]
    hb = h.astype(BF16)
    ple = _ple(hb, p_ref, wpg_ref, wpp_ref)

    for k in range(2):
        pltpu.make_async_copy(ys_ref.at[pl.ds(0, tm)], buf_ref.at[slot, k], sem.at[slot]).wait()
    rt = route_ref[...]
    moe = rt[:, 4:5] * buf_ref[slot, 0] + rt[:, 5:6] * buf_ref[slot, 1]
    y_ref[...] = _layer_norm(alpha * h + moe + ple, g_ref[...], b_ref[...])


def _moe_out(idx3, h, p, route, ys, w, alpha):
    r, d = h.shape
    steps, _, n = idx3.shape
    tm = n // 2

    def row(width):
        return pl.BlockSpec((tm, width), lambda i: (i, 0))

    return pl.pallas_call(
        functools.partial(_moe_out_kernel, alpha=alpha), grid=(steps,),
        in_specs=[pl.BlockSpec((1, 1, n), lambda i: (i, 0, 0), memory_space=pltpu.SMEM),
                  pl.BlockSpec((1, 1, n), lambda i: (jnp.minimum(i + 1, steps - 1), 0, 0), memory_space=pltpu.SMEM),
                  row(d), row(p.shape[1]), row(LANES), pl.BlockSpec(memory_space=pl.ANY)]
        + [_resident(a.shape) for a in w],
        out_specs=row(d), out_shape=jax.ShapeDtypeStruct((r, d), F32),
        scratch_shapes=[pltpu.VMEM((2, 2, tm, d), F32), pltpu.SemaphoreType.DMA((2,))],
        compiler_params=_params(("arbitrary",), 48 << 20), name="moe_out",
    )(idx3, idx3, h, p, route, ys, *w)


def _moe_ffn(h, p, w, alpha):
    r, d = h.shape
    wr, br, wg, wu, wd, wpg, wpp, g, bb = w
    tmx = MOE_ROW_TILE
    route, hp, counts = _route(h, wr, br)
    cnt = counts[0, :N_EXPERTS].astype(jnp.int32)
    padded = (cnt + tmx - 1) // tmx * tmx
    ends = jnp.cumsum(padded)
    dest = jnp.take(ends - padded, route[:, 0:2].astype(jnp.int32)) + route[:, 2:4].astype(jnp.int32)
    n_tiles = -(-2 * r // tmx) + N_EXPERTS
    starts = jnp.arange(n_tiles, dtype=jnp.int32) * tmx
    tile_expert = jnp.minimum(jnp.sum(ends[None, :] <= starts[:, None], axis=1), N_EXPERTS - 1).astype(jnp.int32)
    tile_valid = (starts < ends[-1]).astype(jnp.int32)
    tm = _pick_tile(r, (512, 256, 128))
    dest3 = dest.reshape(r // tm, 1, 2 * tm)
    xs = _row_scatter(dest3, hp, jnp.zeros((n_tiles * tmx, d // 2), jnp.uint32))
    ys = _grouped_ffn(tile_expert, tile_valid, xs, wg, wu, wd)
    return _moe_out(dest3, h, p, route, ys, (wpg, wpp, g, bb), alpha)


def _pad_lanes(a, width=LANES):
    return jnp.pad(a, ((0, 0), (0, width - a.shape[1])))


def _rot_half_cols(a):
    half = a.shape[-1] // 2
    return jnp.concatenate([-a[..., half:], a[..., :half]], axis=-1)


def _mixer_weights(w_in, b_fox_f, g_cq, w_qb, g_ckv, w_kvb):
    d = w_in.shape[0]
    sizes = (FOX_W, FOX_W, FOX_W, FOX_HEADS, MLA_Q_RANK, MLA_KV_RANK, MLA_ROPE, d, d)
    cols, start = [], 0
    for n in sizes:
        cols.append(w_in[:, start:start + n])
        start += n
    wfq, wfk, wfv, wff, wcq, wckv, wkr, wga, wgb = cols
    w1 = jnp.concatenate([wfq * (FOX_HEAD_DIM ** -0.5 * LOG2E), wfk, wfv], axis=1).astype(BF16)
    w2 = jnp.concatenate([wcq, wckv, _pad_lanes(wkr), _pad_lanes(_rot_half_cols(wkr)), _pad_lanes(wff)],
                         axis=1).astype(BF16)
    wfft = jnp.pad(wff.T, ((0, 16 - FOX_HEADS), (0, 0))).astype(BF16)
    bf = b_fox_f.reshape(1, FOX_HEADS)
    bft = b_fox_f.reshape(FOX_HEADS, 1)

    qb = w_qb.reshape(MLA_Q_RANK, MLA_HEADS, MLA_NOPE + MLA_ROPE)
    q_nope, q_rope = qb[..., :MLA_NOPE], qb[..., MLA_NOPE:]
    q_rot = _rot_half_cols(q_rope)
    z_pad = jnp.zeros((MLA_Q_RANK, MLA_QK_W - 2 * MLA_NOPE - 2 * MLA_ROPE), F32)
    z_nope = jnp.zeros((MLA_Q_RANK, 2 * MLA_NOPE), F32)
    wqa = jnp.concatenate([jnp.concatenate([q_nope[:, 2 * j], q_nope[:, 2 * j + 1], q_rope[:, 2 * j],
                                            q_rope[:, 2 * j + 1], z_pad], axis=1) for j in range(N_PAIRS)], axis=1)
    wqb = jnp.concatenate([jnp.concatenate([z_nope, q_rot[:, 2 * j], q_rot[:, 2 * j + 1], z_pad], axis=1)
                           for j in range(N_PAIRS)], axis=1)

    kvb = w_kvb.reshape(MLA_KV_RANK, MLA_HEADS, MLA_NOPE + MLA_V)
    k_nope, v_up = kvb[..., :MLA_NOPE], kvb[..., MLA_NOPE:]
    zk = jnp.zeros((MLA_KV_RANK, MLA_QK_W - 2 * MLA_NOPE), F32)
    wk = jnp.concatenate([jnp.concatenate([k_nope[:, 2 * j], k_nope[:, 2 * j + 1], zk], axis=1)
                          for j in range(N_PAIRS)], axis=1).astype(BF16)
    eye = jnp.eye(MLA_ROPE, dtype=F32)
    place = jnp.concatenate([jnp.zeros((MLA_ROPE, 2 * MLA_NOPE), F32), eye, eye,
                             jnp.zeros((MLA_ROPE, MLA_QK_W - 2 * MLA_NOPE - 2 * MLA_ROPE), F32)], axis=1)
    place = jnp.tile(place, (1, N_PAIRS)).astype(BF16)
    wv = v_up.reshape(MLA_KV_RANK, MLA_W).astype(BF16)
    proj_w = (w1, w2, wfft, wfv.T.astype(BF16), bf, bft, g_cq.reshape(1, -1), g_ckv.reshape(1, -1),
              wqa.astype(BF16), wqb.astype(BF16))
    return proj_w, (wk, place, wv.T), (wga.astype(BF16), wgb.astype(BF16))


def _rope_tables(pos):
    half = MLA_ROPE // 2
    inv = ROPE_THETA ** (-jnp.arange(half, dtype=F32) * 2.0 / MLA_ROPE)
    ang = pos.astype(F32)[:, None] * inv[None, :]
    cos2 = jnp.concatenate([jnp.cos(ang)] * 2, axis=1)
    sin2 = jnp.concatenate([jnp.sin(ang)] * 2, axis=1)
    n = pos.shape[0]
    scale = (MLA_NOPE + MLA_ROPE) ** -0.5 * LOG2E
    pad = jnp.zeros((n, MLA_QK_W - 2 * MLA_NOPE - 2 * MLA_ROPE), F32)
    ctab = jnp.concatenate([jnp.full((n, 2 * MLA_NOPE), scale, F32), scale * cos2, scale * cos2, pad], axis=1)
    stab = jnp.concatenate([jnp.zeros((n, 2 * MLA_NOPE), F32), scale * sin2, scale * sin2, pad], axis=1)
    return jnp.tile(ctab, (1, N_PAIRS)), jnp.tile(stab, (1, N_PAIRS)), cos2, sin2


def _pad_time(a, tp, axis):
    pad = [(0, 0)] * a.ndim
    pad[axis] = (0, tp - a.shape[axis])
    return jnp.pad(a, pad)


def _layer(x, past, p, proj_w, kvx_w, mix_w, ffn_w, is_moe, alpha):
    b, s, d = x.shape
    n_past = 0 if past is None else past[0].shape[1]
    t = n_past + s
    tabs = _rope_tables(n_past + jnp.arange(s))
    fq, fk, fv, fkb, fvt, logf, logft, ckv, kr, qp = _proj(x, proj_w, tabs)
    if past is None:
        tp = t
        k_fox, vt_fox, logft_all, ckv_all, kr_all = fkb, fvt, logft, ckv, kr
    else:
        tp = -(-t // LANES) * LANES
        pk, pv, plogf, pckv, pkr = past
        k_fox = _pad_time(jnp.concatenate([pk.reshape(b, n_past, FOX_W).astype(BF16), fkb], axis=1), tp, 1)
        pvt = jnp.swapaxes(pv.reshape(b, n_past, FOX_W), 1, 2).astype(BF16)
        vt_fox = _pad_time(jnp.concatenate([pvt, fvt], axis=2), tp, 2)
        logft_all = _pad_time(jnp.concatenate([jnp.swapaxes(plogf, 1, 2), logft], axis=2), tp, 2)
        ckv_all = _pad_time(jnp.concatenate([pckv, ckv], axis=1), tp, 1)
        kr_all = _pad_time(jnp.concatenate([pkr, kr], axis=1), tp, 1)
    nb_aug = _neg_cumsum(logft_all)
    k_mla, vt_mla = _kv_expand(ckv_all, kr_all, *kvx_w)
    sq = -(-s // LANES) * LANES
    fq_p, qp_p = _pad_time(fq, sq, 1), _pad_time(qp, sq, 1)
    o_fox = _attention_t(fq_p, k_fox, vt_fox, nb_aug, q_off=n_past, t_valid=t, chunked=False)[:, :s]
    o_mla = _attention_t(qp_p, k_mla, vt_mla, None, q_off=n_past, t_valid=t, chunked=True)[:, :s]
    r = b * s
    h = _mix(x.reshape(r, d), o_fox.reshape(r, FOX_W), o_mla.reshape(r, MLA_W), mix_w, alpha)
    ffn = _moe_ffn if is_moe else _dense_ffn
    y = ffn(h, p.reshape(r, -1), ffn_w, alpha).reshape(b, s, d)
    new_rows = (fk.reshape(b, s, FOX_HEADS, FOX_HEAD_DIM), fv.reshape(b, s, FOX_HEADS, FOX_HEAD_DIM), logf, ckv, kr)
    return y, new_rows


def kernel(x_prompt, x_sample, cache_fox_k, cache_fox_v, cache_fox_logf, cache_mla_ckv, cache_mla_krope,
           p_prompt, p_sample, w_in, b_fox_f, g_mla_cq, w_mla_qb, g_mla_ckv, w_mla_kvb, w_o_fox, w_o_mla,
           w_out, ln_mix_g, ln_mix_b, w_ffn_gate, w_ffn_up, w_ffn_down, w_router, b_router, w_moe_gate,
           w_moe_up, w_moe_down, w_ple_proj, w_ple_gate, ln_ffn_g, ln_ffn_b):
    depth = w_in.shape[0]
    alpha = (2 * depth) ** 0.25
    hp, hs = x_prompt, x_sample
    rows_p, rows_s = [], []
    for i in range(depth):
        proj_w, kvx_w, (wga, wgb) = _mixer_weights(w_in[i], b_fox_f[i], g_mla_cq[i], w_mla_qb[i], g_mla_ckv[i],
                                                   w_mla_kvb[i])
        mix_w = (wga, wgb, w_o_fox[i].astype(BF16), w_o_mla[i].astype(BF16), w_out[i].astype(BF16),
                 ln_mix_g[i].reshape(1, -1), ln_mix_b[i].reshape(1, -1))
        tail = (w_ple_gate[i].astype(BF16), w_ple_proj[i].astype(BF16),
                ln_ffn_g[i].reshape(1, -1), ln_ffn_b[i].reshape(1, -1))
        j = i // 2
        is_moe = i % 2 == 1
        if is_moe:
            ffn_w = (_pad_lanes(w_router[j]).astype(BF16), _pad_lanes(b_router[j].reshape(1, -1)),
                     w_moe_gate[j].astype(BF16), w_moe_up[j].astype(BF16), w_moe_down[j].astype(BF16)) + tail
        else:
            ffn_w = (w_ffn_gate[j].astype(BF16), w_ffn_up[j].astype(BF16), w_ffn_down[j].astype(BF16)) + tail
        past = (cache_fox_k[i], cache_fox_v[i], cache_fox_logf[i], cache_mla_ckv[i], cache_mla_krope[i])
        hp, new_p = _layer(hp, None, p_prompt[i], proj_w, kvx_w, mix_w, ffn_w, is_moe, alpha)
        hs, new_s = _layer(hs, past, p_sample[i], proj_w, kvx_w, mix_w, ffn_w, is_moe, alpha)
        rows_p.append(new_p)
        rows_s.append(new_s)

    def stack(rows, idx):
        return jnp.stack([r[idx] for r in rows], axis=0)

    return (hp, hs) + tuple(stack(rows_p, k) for k in range(5)) + tuple(stack(rows_s, k) for k in range(5))
```

```python
import functools

import jax
import jax.numpy as jnp
from jax import lax
from jax.experimental import pallas as pl
from jax.experimental.pallas import tpu as pltpu

CHUNK = 64
FOX_HEADS = 8
FOX_HEAD_DIM = 64
FOX_W = FOX_HEADS * FOX_HEAD_DIM
MLA_HEADS = 8
MLA_Q_RANK = 256
MLA_KV_RANK = 128
MLA_NOPE = 64
MLA_ROPE = 32
MLA_V = 64
MLA_W = MLA_HEADS * MLA_V
ROPE_THETA = 10000.0
N_EXPERTS = 8
LN_EPS = 1e-5
RMS_EPS = 1e-6
NEG_INF = -1e30
LOG2E = 1.4426950408889634

LANES = 128
SUBLANES = 8
PAIR_W = 2 * MLA_V
MLA_QK_W = 256
N_PAIRS = FOX_HEADS // 2
V7X_SCOPED_VMEM_BYTES = 60000 * 1024

BF16 = jnp.bfloat16
F32 = jnp.float32


def _dot(a, b):
    return jnp.dot(a, b, preferred_element_type=F32)


def _dot_nt(a, b):
    return lax.dot_general(a, b, (((1,), (1,)), ((), ())), preferred_element_type=F32)


def _sigmoid(x):
    return 1.0 / (1.0 + jnp.exp(-x))


def _log_sigmoid(x):
    return jnp.minimum(x, 0.0) - jnp.log1p(jnp.exp(-jnp.abs(x)))


def _rms_norm(x, g):
    return x * lax.rsqrt(jnp.mean(jnp.square(x), axis=-1, keepdims=True) + RMS_EPS) * g


def _layer_norm(x, g, b):
    mu = jnp.mean(x, axis=-1, keepdims=True)
    xc = x - mu
    var = jnp.mean(jnp.square(xc), axis=-1, keepdims=True)
    return xc * lax.rsqrt(var + LN_EPS) * g + b


def _resident(shape):
    nd = len(shape)
    return pl.BlockSpec(shape, lambda *_: (0,) * nd, pipeline_mode=pl.Buffered(1))


def _params(semantics, vmem_bytes):
    return pltpu.CompilerParams(dimension_semantics=semantics,
                                vmem_limit_bytes=min(int(vmem_bytes), V7X_SCOPED_VMEM_BYTES))


def _pick_tile(n, candidates):
    for c in candidates:
        if n % c == 0:
            return c
    return n


def _proj_kernel(x_ref, w1_ref, w2_ref, wfft_ref, wfvt_ref, bf_ref, bft_ref, gcq_ref, gckv_ref, wqa_ref, wqb_ref,
                 ctab_ref, stab_ref, cos_ref, sin_ref,
                 fq_ref, fk_ref, fv_ref, fkb_ref, fvt_ref, logf_ref, logft_ref, ckv_ref, kr_ref, qp_ref):
    xb = x_ref[0].astype(BF16)
    z1 = _dot(xb, w1_ref[...])
    fq_ref[0] = z1[:, :FOX_W].astype(BF16)
    fk = z1[:, FOX_W:2 * FOX_W]
    fk_ref[0] = fk
    fkb_ref[0] = fk.astype(BF16)
    fv_ref[0] = z1[:, 2 * FOX_W:]
    fvt_ref[0] = _dot_nt(wfvt_ref[...], xb).astype(BF16)

    z2 = _dot(xb, w2_ref[...])
    cq = z2[:, :MLA_Q_RANK]
    o = MLA_Q_RANK
    ckv = z2[:, o:o + MLA_KV_RANK]
    o += MLA_KV_RANK
    kr = z2[:, o:o + MLA_ROPE]
    krr = z2[:, o + LANES:o + LANES + MLA_ROPE]
    ff = z2[:, o + 2 * LANES:o + 2 * LANES + FOX_HEADS]
    logf_ref[0] = _log_sigmoid(ff + bf_ref[...])
    fft = _dot_nt(wfft_ref[...], xb)
    logft_ref[0] = _log_sigmoid(fft[:FOX_HEADS] + bft_ref[...])
    ckv_ref[0] = _rms_norm(ckv, gckv_ref[...])
    kr_ref[0] = kr * cos_ref[...] + krr * sin_ref[...]
    cqn = _rms_norm(cq, gcq_ref[...]).astype(BF16)
    qp = _dot(cqn, wqa_ref[...]) * ctab_ref[...] + _dot(cqn, wqb_ref[...]) * stab_ref[...]
    qp_ref[0] = qp.astype(BF16)


def _proj(x, w, tabs):
    b, s, d = x.shape
    tm = _pick_tile(s, (512, 256, 128))
    ns = s // tm
    w1, w2, wfft, wfvt, bf, bft, gcq, gckv, wqa, wqb = w
    ctab, stab, cos2, sin2 = tabs
    qw = N_PAIRS * MLA_QK_W

    def tok(width):
        return pl.BlockSpec((1, tm, width), lambda si, bi: (bi, si, 0))

    def tok_t(height):
        return pl.BlockSpec((1, height, tm), lambda si, bi: (bi, 0, si))

    def tab(width):
        return pl.BlockSpec((tm, width), lambda si, bi: (si, 0))

    in_specs = [tok(d), _resident(w1.shape), _resident(w2.shape), _resident(wfft.shape), _resident(wfvt.shape),
                _resident(bf.shape), _resident(bft.shape), _resident(gcq.shape), _resident(gckv.shape),
                _resident(wqa.shape), _resident(wqb.shape), tab(qw), tab(qw), tab(MLA_ROPE), tab(MLA_ROPE)]
    out_shape = (
        jax.ShapeDtypeStruct((b, s, FOX_W), BF16),
        jax.ShapeDtypeStruct((b, s, FOX_W), F32),
        jax.ShapeDtypeStruct((b, s, FOX_W), F32),
        jax.ShapeDtypeStruct((b, s, FOX_W), BF16),
        jax.ShapeDtypeStruct((b, FOX_W, s), BF16),
        jax.ShapeDtypeStruct((b, s, FOX_HEADS), F32),
        jax.ShapeDtypeStruct((b, FOX_HEADS, s), F32),
        jax.ShapeDtypeStruct((b, s, MLA_KV_RANK), F32),
        jax.ShapeDtypeStruct((b, s, MLA_ROPE), F32),
        jax.ShapeDtypeStruct((b, s, qw), BF16),
    )
    out_specs = (tok(FOX_W), tok(FOX_W), tok(FOX_W), tok(FOX_W), tok_t(FOX_W), tok(FOX_HEADS),
                 tok_t(FOX_HEADS), tok(MLA_KV_RANK), tok(MLA_ROPE), tok(qw))
    return pl.pallas_call(
        _proj_kernel, grid=(ns, b), in_specs=in_specs, out_specs=out_specs, out_shape=out_shape,
        compiler_params=_params(("arbitrary", "arbitrary"), 48 << 20), name="proj",
    )(x, w1, w2, wfft, wfvt, bf, bft, gcq, gckv, wqa, wqb, ctab, stab, cos2, sin2)


def _cumsum_kernel(x_ref, aug_ref, *, ch):
    t = x_ref.shape[2]
    lane = lax.broadcasted_iota(jnp.int32, (1, LANES), 1)
    r = lax.broadcasted_iota(jnp.int32, (ch, ch), 0)
    c = lax.broadcasted_iota(jnp.int32, (ch, ch), 1)
    upper = jnp.where(r <= c, 1.0, 0.0).astype(BF16)
    carry = jnp.zeros((FOX_HEADS, 1), F32)
    zeros = jnp.zeros((FOX_HEADS, ch), F32)
    for ci in range(t // ch):
        xc = x_ref[0, :, ci * ch:(ci + 1) * ch]
        hi = xc.astype(BF16).astype(F32)
        r1 = xc - hi
        mid = r1.astype(BF16).astype(F32)
        lo = (r1 - mid).astype(BF16).astype(F32)
        pieces = jnp.concatenate([hi, mid, lo, zeros], axis=0).astype(BF16)
        pc = _dot(pieces, upper)
        cum = pc[0:8] + pc[8:16] + pc[16:24] + carry
        nb = cum * (-LOG2E)
        for hh in range(FOX_HEADS):
            rep = jnp.broadcast_to(nb[hh:hh + 1, :], (LANES, ch)).T
            a_hi = rep.astype(BF16).astype(F32)
            a_r = rep - a_hi
            a_mid = a_r.astype(BF16).astype(F32)
            a_lo = (a_r - a_mid).astype(BF16).astype(F32)
            aug = jnp.where(lane == 0, a_hi, jnp.where(lane == 1, a_mid, jnp.where(lane == 2, a_lo, 0.0)))
            aug_ref[0, hh, ci * ch:(ci + 1) * ch, :] = aug.astype(BF16)
        carry = cum[:, ch - 1:ch]


def _neg_cumsum(logft):
    b, h, t = logft.shape
    ch = 256 if t % 256 == 0 else LANES
    spec = pl.BlockSpec((1, h, t), lambda bi: (bi, 0, 0))
    return pl.pallas_call(
        functools.partial(_cumsum_kernel, ch=ch), grid=(b,), in_specs=[spec],
        out_specs=pl.BlockSpec((1, h, t, LANES), lambda bi: (bi, 0, 0, 0)),
        out_shape=jax.ShapeDtypeStruct((b, h, t, LANES), BF16),
        compiler_params=_params(("arbitrary",), 32 << 20), name="cumsum",
    )(logft)


def _kvx_kernel(ckv_ref, kr_ref, wk_ref, place_ref, wvt_ref, kp_ref, vmt_ref):
    cb = ckv_ref[0].astype(BF16)
    kp = _dot(cb, wk_ref[...]) + _dot(kr_ref[0].astype(BF16), place_ref[...])
    kp_ref[0] = kp.astype(BF16)
    vmt_ref[0] = _dot_nt(wvt_ref[...], cb).astype(BF16)


def _kv_expand(ckv, kr, wk, place, wvt):
    b, t, _ = ckv.shape
    tm = _pick_tile(t, (512, 384, 256, 128))
    kw = N_PAIRS * MLA_QK_W

    def tok(width):
        return pl.BlockSpec((1, tm, width), lambda bi, ti: (bi, ti, 0))

    return pl.pallas_call(
        _kvx_kernel, grid=(b, t // tm),
        in_specs=[tok(MLA_KV_RANK), tok(MLA_ROPE), _resident(wk.shape), _resident(place.shape), _resident(wvt.shape)],
        out_specs=(tok(kw), pl.BlockSpec((1, MLA_W, tm), lambda bi, ti: (bi, 0, ti))),
        out_shape=(jax.ShapeDtypeStruct((b, t, kw), BF16), jax.ShapeDtypeStruct((b, MLA_W, t), BF16)),
        compiler_params=_params(("arbitrary", "arbitrary"), 32 << 20), name="kvexpand",
    )(ckv, kr, wk, place, wvt)


def _head_lane_mask(lane, hh, width):
    if width == LANES:
        return (lane // FOX_HEAD_DIM) == hh
    nope = (lane < 2 * MLA_NOPE) & ((lane // MLA_NOPE) == hh)
    rope = (lane >= 2 * MLA_NOPE) & (lane < 2 * MLA_NOPE + 2 * MLA_ROPE) & (
        ((lane - 2 * MLA_NOPE) // MLA_ROPE) == hh)
    return nope | rope


def _attn_t_kernel(*refs, tq, tk, width, q_off, t_valid, chunked, has_bias):
    if has_bias:
        q_ref, k_ref, vt_ref, nb_ref, o_ref, sa_ref, sb_ref = refs
    else:
        q_ref, k_ref, vt_ref, o_ref, sa_ref, sb_ref = refs
        nb_ref = None
    nq = q_ref.shape[1] // tq
    nkb = k_ref.shape[1] // tk

    items = []
    for qi in range(nq):
        q_min = q_off + qi * tq
        q_max = q_min + tq - 1
        if chunked:
            lim_min, lim_max = (q_min // CHUNK + 1) * CHUNK, (q_max // CHUNK + 1) * CHUNK
        else:
            lim_min, lim_max = q_min + 1, q_max + 1
        n_full = min(lim_min, t_valid) // tk
        n_vis = min(-(-min(lim_max, t_valid) // tk), nkb)
        items += [(qi, kb, kb >= n_full, kb == n_vis - 1) for kb in range(n_vis)]

    lane = lax.broadcasted_iota(jnp.int32, (1, width), 1)
    keep = [_head_lane_mask(lane, hh, width) for hh in range(2)]
    bufs = (sa_ref, sb_ref)

    ones3 = jnp.where(lax.broadcasted_iota(jnp.int32, (tq, LANES), 1) < 3, 1.0, 0.0).astype(BF16)
    ones_rows = jnp.ones((16, tk), BF16)

    def scores_into(s_ref, qi, kb):
        q = q_ref[0, qi * tq:(qi + 1) * tq, :]
        kblk = k_ref[0, kb * tk:(kb + 1) * tk, :]
        for hh in range(2):
            qh = jnp.where(keep[hh], q, jnp.zeros_like(q))
            if has_bias:
                kh = jnp.concatenate([kblk, nb_ref[0, hh, kb * tk:(kb + 1) * tk, :]], axis=1)
                s_ref[hh] = _dot_nt(kh, jnp.concatenate([qh, ones3], axis=1))
            else:
                s_ref[hh] = _dot_nt(kblk, qh)

    def consume(s_ref, qi, kb, stats, masked):
        if masked:
            qpos = q_off + qi * tq + lax.broadcasted_iota(jnp.int32, (1, tq), 1)
            if chunked:
                col_lim = lax.shift_left(lax.shift_right_logical(qpos, 6) + 1, 6)
            else:
                col_lim = qpos + 1
            col_lim = jnp.minimum(col_lim, t_valid)
            kpos = kb * tk + lax.broadcasted_iota(jnp.int32, (tk, 1), 0)
        out = []
        for hh in range(2):
            m, acc = stats[2 * hh:2 * hh + 2]
            s = s_ref[hh]
            if masked:
                s = jnp.where(kpos < col_lim, s, NEG_INF)
            m_new = jnp.maximum(m, jnp.max(s, axis=0, keepdims=True))
            p = jnp.exp2(s - m_new)
            alpha = jnp.exp2(m - m_new)
            vth = jnp.concatenate([vt_ref[0, hh * MLA_V:(hh + 1) * MLA_V, kb * tk:(kb + 1) * tk], ones_rows], axis=0)
            acc = alpha * acc + _dot(vth, p.astype(BF16))
            out += [m_new, acc]
        return tuple(out)

    scores_into(bufs[0], items[0][0], items[0][1])
    stats = None
    for i, (qi, kb, masked, last) in enumerate(items):
        if i + 1 < len(items):
            scores_into(bufs[(i + 1) % 2], items[i + 1][0], items[i + 1][1])
        if kb == 0:
            stats = (jnp.full((1, tq), NEG_INF, F32), jnp.zeros((MLA_V + 16, tq), F32)) * 2
        stats = consume(bufs[i % 2], qi, kb, stats, masked)
        if last:
            outs = [stats[2 * hh + 1][:MLA_V] * (1.0 / stats[2 * hh + 1][MLA_V:MLA_V + 1]) for hh in range(2)]
            o_ref[0, qi * tq:(qi + 1) * tq, :] = jnp.concatenate(outs, axis=0).T.astype(BF16)


def _attention_t(q, k, vt, nbrep, *, q_off, t_valid, chunked):
    assert CHUNK == 64
    b, sq, qw = q.shape
    width = qw // N_PAIRS
    tp = k.shape[1]
    tq = _pick_tile(sq, (512, 256, 128))
    tk = _pick_tile(tp, (512, 384, 256, 128))
    in_specs = [pl.BlockSpec((1, sq, width), lambda bi, pi: (bi, 0, pi)),
                pl.BlockSpec((1, tp, width), lambda bi, pi: (bi, 0, pi)),
                pl.BlockSpec((1, PAIR_W, tp), lambda bi, pi: (bi, pi, 0))]
    args = [q, k, vt]
    if nbrep is not None:
        in_specs.append(pl.BlockSpec((1, 2, tp, LANES), lambda bi, pi: (bi, pi, 0, 0)))
        args.append(nbrep)
    kern = functools.partial(_attn_t_kernel, tq=tq, tk=tk, width=width, q_off=q_off, t_valid=t_valid,
                             chunked=chunked, has_bias=nbrep is not None)
    return pl.pallas_call(
        kern, grid=(b, N_PAIRS), in_specs=in_specs,
        out_specs=pl.BlockSpec((1, sq, PAIR_W), lambda bi, pi: (bi, 0, pi)),
        out_shape=jax.ShapeDtypeStruct((b, sq, N_PAIRS * PAIR_W), BF16),
        scratch_shapes=[pltpu.VMEM((2, tk, tq), F32), pltpu.VMEM((2, tk, tq), F32)],
        compiler_params=_params(("arbitrary", "arbitrary"), 40 << 20),
        name="attn_t_mla" if chunked else "attn_t_fox",
    )(*args)


def _mix_kernel(x_ref, of_ref, om_ref, wga_ref, wgb_ref, wof_ref, wom_ref, wout_ref, g_ref, b_ref, h_ref, *, alpha):
    x = x_ref[...]
    xb = x.astype(BF16)
    a = _sigmoid(_dot(xb, wga_ref[...])) * _dot(of_ref[...], wof_ref[...])
    bm = _sigmoid(_dot(xb, wgb_ref[...])) * _dot(om_ref[...], wom_ref[...])
    m = _dot((a + bm).astype(BF16), wout_ref[...])
    h_ref[...] = _layer_norm(alpha * x + m, g_ref[...], b_ref[...])


def _mix(x, of, om, w, alpha):
    r, d = x.shape
    tm = _pick_tile(r, (512, 256, 128))
    wga, wgb, wof, wom, wout, g, bb = w

    def row(width):
        return pl.BlockSpec((tm, width), lambda i: (i, 0))

    return pl.pallas_call(
        functools.partial(_mix_kernel, alpha=alpha), grid=(r // tm,),
        in_specs=[row(d), row(FOX_W), row(MLA_W)] + [_resident(a.shape) for a in w],
        out_specs=row(d), out_shape=jax.ShapeDtypeStruct((r, d), F32),
        compiler_params=_params(("arbitrary",), 48 << 20), name="mix",
    )(x, of, om, wga, wgb, wof, wom, wout, g, bb)


def _ple(hb, p_ref, wpg_ref, wpp_ref):
    return _sigmoid(_dot(hb, wpg_ref[...])) * _dot(p_ref[...].astype(BF16), wpp_ref[...])


def _ffn_chunk(n_ff):
    return _pick_tile(n_ff, (1408, 1024, 512, 256, 128))


def _dense_ffn_kernel(h_ref, p_ref, wg_ref, wu_ref, wd_ref, wpg_ref, wpp_ref, g_ref, b_ref, y_ref, *, alpha, fc):
    h = h_ref[...]
    hb = h.astype(BF16)
    n_ff = wg_ref.shape[1]
    acc = jnp.zeros(h.shape, F32)
    for c in range(n_ff // fc):
        sl = slice(c * fc, (c + 1) * fc)
        gt = _dot(hb, wg_ref[:, sl])
        act = (gt * _sigmoid(gt) * _dot(hb, wu_ref[:, sl])).astype(BF16)
        acc = acc + _dot(act, wd_ref[sl, :])
    y = alpha * h + acc + _ple(hb, p_ref, wpg_ref, wpp_ref)
    y_ref[...] = _layer_norm(y, g_ref[...], b_ref[...])


def _dense_ffn(h, p, w, alpha):
    r, d = h.shape
    tm = _pick_tile(r, (512, 256, 128))
    wg, wu, wd, wpg, wpp, g, bb = w

    def row(width):
        return pl.BlockSpec((tm, width), lambda i: (i, 0))

    return pl.pallas_call(
        functools.partial(_dense_ffn_kernel, alpha=alpha, fc=_ffn_chunk(wg.shape[1])), grid=(r // tm,),
        in_specs=[row(d), row(p.shape[1])] + [_resident(a.shape) for a in w],
        out_specs=row(d), out_shape=jax.ShapeDtypeStruct((r, d), F32),
        compiler_params=_params(("arbitrary",), 56 << 20), name="ffn_dense",
    )(h, p, wg, wu, wd, wpg, wpp, g, bb)


MOE_ROW_TILE = 512
ROUTE_LANES = 6


def _route_kernel(h_ref, wr_ref, br_ref, route_ref, hp_ref, counts_ref, carry_ref):
    @pl.when(pl.program_id(0) == 0)
    def _init():
        carry_ref[...] = jnp.zeros(carry_ref.shape, F32)

    hb = h_ref[...].astype(BF16)
    tm, d = hb.shape
    lane = lax.broadcasted_iota(jnp.int32, (1, LANES), 1)
    logits = _dot(hb, wr_ref[...]) + br_ref[...]
    lg = jnp.where(lane < N_EXPERTS, logits, -jnp.inf)
    m1 = jnp.max(lg, axis=-1, keepdims=True)
    i1 = jnp.min(jnp.where(lg == m1, lane, LANES), axis=-1, keepdims=True)
    lg2 = jnp.where(lane == i1, -jnp.inf, lg)
    m2 = jnp.max(lg2, axis=-1, keepdims=True)
    i2 = jnp.min(jnp.where(lg2 == m2, lane, LANES), axis=-1, keepdims=True)
    e2 = jnp.exp(m2 - m1)
    den = 1.0 + e2
    hit1 = lane == i1
    hit2 = lane == i2
    onehot = jnp.where(hit1, 1.0, 0.0) + jnp.where(hit2, 1.0, 0.0)
    rr = lax.broadcasted_iota(jnp.int32, (tm, tm), 0)
    cc = lax.broadcasted_iota(jnp.int32, (tm, tm), 1)
    lower = jnp.where(cc < rr, 1.0, 0.0).astype(BF16)
    prefix = _dot(lower, onehot.astype(BF16)) + carry_ref[...]
    rank1 = jnp.sum(jnp.where(hit1, prefix, 0.0), axis=-1, keepdims=True)
    rank2 = jnp.sum(jnp.where(hit2, prefix, 0.0), axis=-1, keepdims=True)
    fields = (i1.astype(F32), i2.astype(F32), rank1, rank2, 1.0 / den, e2 / den)
    route = jnp.zeros((tm, LANES), F32)
    for j, f in enumerate(fields):
        route = jnp.where(lane == j, f, route)
    route_ref[...] = route
    carry = carry_ref[...] + jnp.sum(onehot, axis=0, keepdims=True)
    carry_ref[...] = carry
    counts_ref[...] = carry
    bits = lax.bitcast_convert_type(hb.astype(F32), jnp.uint32)
    hp_ref[...] = (bits[:, d // 2:] & jnp.uint32(0xFFFF0000)) | lax.shift_right_logical(bits[:, :d // 2], jnp.uint32(16))


def _route(h, wr, br):
    r, d = h.shape
    tm = _pick_tile(r, (512, 256, 128))
    return pl.pallas_call(
        _route_kernel, grid=(r // tm,),
        in_specs=[pl.BlockSpec((tm, d), lambda i: (i, 0)), _resident(wr.shape), _resident(br.shape)],
        out_specs=(pl.BlockSpec((tm, LANES), lambda i: (i, 0)), pl.BlockSpec((tm, d // 2), lambda i: (i, 0)),
                   pl.BlockSpec((1, LANES), lambda i: (0, 0))),
        out_shape=(jax.ShapeDtypeStruct((r, LANES), F32), jax.ShapeDtypeStruct((r, d // 2), jnp.uint32),
                   jax.ShapeDtypeStruct((1, LANES), F32)),
        scratch_shapes=[pltpu.VMEM((1, LANES), F32)],
        compiler_params=_params(("arbitrary",), 32 << 20), name="moe_route",
    )(h, wr, br)


ROW_DMA_UNROLL = 8


def _row_scatter_kernel(idx_ref, src_ref, init_ref, dst_ref, sem, *, tm):
    del init_ref

    def copy(t, d_row):
        return pltpu.make_async_copy(src_ref.at[pl.ds(t, 1)], dst_ref.at[pl.ds(d_row, 1)], sem)

    def issue(t, c):
        copy(t, idx_ref[0, 0, 2 * t]).start(priority=0)
        copy(t, idx_ref[0, 0, 2 * t + 1]).start(priority=1)
        return c

    lax.fori_loop(0, tm, issue, 0, unroll=ROW_DMA_UNROLL)
    for _ in range(2):
        pltpu.make_async_copy(src_ref, dst_ref.at[pl.ds(0, tm)], sem).wait()


def _row_scatter(idx3, src, init):
    steps, _, n = idx3.shape
    tm = n // 2
    any_spec = pl.BlockSpec(memory_space=pl.ANY)
    return pl.pallas_call(
        functools.partial(_row_scatter_kernel, tm=tm), grid=(steps,),
        in_specs=[pl.BlockSpec((1, 1, n), lambda i: (i, 0, 0), memory_space=pltpu.SMEM),
                  pl.BlockSpec((tm, src.shape[1]), lambda i: (i, 0)), any_spec],
        out_specs=any_spec, out_shape=jax.ShapeDtypeStruct(init.shape, init.dtype),
        scratch_shapes=[pltpu.SemaphoreType.DMA(())], input_output_aliases={2: 0},
        compiler_params=_params(("arbitrary",), 16 << 20), name="moe_scatter",
    )(idx3, src, init)


def _grouped_ffn_kernel(te_ref, tv_ref, x_ref, wg_ref, wu_ref, wd_ref, y_ref, *, fc):
    del te_ref
    valid = tv_ref[pl.program_id(0)] != 0

    @pl.when(valid)
    def _compute():
        pk = x_ref[...]
        lo = lax.bitcast_convert_type(lax.shift_left(pk, jnp.uint32(16)), F32)
        hi = lax.bitcast_convert_type(pk & jnp.uint32(0xFFFF0000), F32)
        xb = jnp.concatenate([lo, hi], axis=1).astype(BF16)
        n_ff = wg_ref.shape[2]
        tmx = xb.shape[0]
        acc = jnp.zeros((tmx, wd_ref.shape[2]), F32)
        for c in range(n_ff // fc):
            sl = slice(c * fc, (c + 1) * fc)
            gt = _dot(xb, wg_ref[0, :, sl])
            act = (gt * _sigmoid(gt) * _dot(xb, wu_ref[0, :, sl])).astype(BF16)
            acc = acc + _dot(act, wd_ref[0, sl, :])
        for j in range(SUBLANES):
            y_ref[pl.ds(j, tmx, stride=SUBLANES), :] = acc[:, j * LANES:(j + 1) * LANES]

    @pl.when(jnp.logical_not(valid))
    def _skip():
        y_ref[...] = jnp.zeros(y_ref.shape, F32)


def _grouped_ffn(tile_expert, tile_valid, xs, wg, wu, wd):
    rp, half = xs.shape
    n_e, d, n_ff = wg.shape
    tmx = MOE_ROW_TILE
    grid_spec = pltpu.PrefetchScalarGridSpec(
        num_scalar_prefetch=2, grid=(rp // tmx,),
        in_specs=[pl.BlockSpec((tmx, half), lambda i, te, tv: (i, 0)),
                  pl.BlockSpec((1, d, n_ff), lambda i, te, tv: (te[i], 0, 0)),
                  pl.BlockSpec((1, d, n_ff), lambda i, te, tv: (te[i], 0, 0)),
                  pl.BlockSpec((1, n_ff, d), lambda i, te, tv: (te[i], 0, 0))],
        out_specs=pl.BlockSpec((tmx * SUBLANES, LANES), lambda i, te, tv: (i, 0)))
    assert d == SUBLANES * LANES
    return pl.pallas_call(
        functools.partial(_grouped_ffn_kernel, fc=_ffn_chunk(n_ff)), grid_spec=grid_spec,
        out_shape=jax.ShapeDtypeStruct((rp * SUBLANES, LANES), F32),
        compiler_params=_params(("arbitrary",), V7X_SCOPED_VMEM_BYTES), name="moe_ffn",
    )(tile_expert, tile_valid, xs, wg, wu, wd)


def _moe_out_kernel(idx_ref, idx_next_ref, h_ref, p_ref, route_ref, ys_ref, wpg_ref, wpp_ref, g_ref, b_ref, y_ref,
                    buf_ref, sem, *, alpha):
    tm = h_ref.shape[0]
    step = pl.program_id(0)
    slot = lax.rem(step, 2)

    def copy(sl, k, t, s_row):
        src = ys_ref.at[pl.ds(pl.multiple_of(s_row * SUBLANES, SUBLANES), SUBLANES)]
        dst = buf_ref.at[sl, k, pl.ds(pl.multiple_of(t * SUBLANES, SUBLANES), SUBLANES)]
        return pltpu.make_async_copy(src, dst, sem.at[sl])

    def issue_block(rows_ref, sl):
        def issue(t, c):
            copy(sl, 0, t, rows_ref[0, 0, 2 * t]).start(priority=0)
            copy(sl, 1, t, rows_ref[0, 0, 2 * t + 1]).start(priority=1)
            return c

        lax.fori_loop(0, tm, issue, 0, unroll=ROW_DMA_UNROLL)

    @pl.when(step == 0)
    def _first():
        issue_block(idx_ref, 0)

    @pl.when(step + 1 < pl.num_programs(0))
    def _ahead():
        issue_block(idx_next_ref, 1 - slot)

    h = h_ref[...]
    hb = h.astype(BF16)
    ple = _ple(hb, p_ref, wpg_ref, wpp_ref)

    for k in range(2):
        pltpu.make_async_copy(ys_ref.at[pl.ds(0, tm * SUBLANES)], buf_ref.at[slot, k], sem.at[slot]).wait()

    def rows(k):
        view = buf_ref.at[slot, k]
        return jnp.concatenate([view[pl.ds(j, tm, stride=SUBLANES), :] for j in range(SUBLANES)], axis=1)

    rt = route_ref[...]
    moe = rt[:, 4:5] * rows(0) + rt[:, 5:6] * rows(1)
    y_ref[...] = _layer_norm(alpha * h + moe + ple, g_ref[...], b_ref[...])


def _moe_out(idx3, h, p, route, ys, w, alpha):
    r, d = h.shape
    steps, _, n = idx3.shape
    tm = n // 2

    def row(width):
        return pl.BlockSpec((tm, width), lambda i: (i, 0))

    return pl.pallas_call(
        functools.partial(_moe_out_kernel, alpha=alpha), grid=(steps,),
        in_specs=[pl.BlockSpec((1, 1, n), lambda i: (i, 0, 0), memory_space=pltpu.SMEM),
                  pl.BlockSpec((1, 1, n), lambda i: (jnp.minimum(i + 1, steps - 1), 0, 0), memory_space=pltpu.SMEM),
                  row(d), row(p.shape[1]), row(LANES), pl.BlockSpec(memory_space=pl.ANY)]
        + [_resident(a.shape) for a in w],
        out_specs=row(d), out_shape=jax.ShapeDtypeStruct((r, d), F32),
        scratch_shapes=[pltpu.VMEM((2, 2, tm * SUBLANES, LANES), F32), pltpu.SemaphoreType.DMA((2,))],
        compiler_params=_params(("arbitrary",), 48 << 20), name="moe_out",
    )(idx3, idx3, h, p, route, ys, *w)


def _moe_ffn(h, p, w, alpha):
    r, d = h.shape
    wr, br, wg, wu, wd, wpg, wpp, g, bb = w
    tmx = MOE_ROW_TILE
    route, hp, counts = _route(h, wr, br)
    cnt = counts[0, :N_EXPERTS].astype(jnp.int32)
    padded = (cnt + tmx - 1) // tmx * tmx
    ends = jnp.cumsum(padded)
    dest = jnp.take(ends - padded, route[:, 0:2].astype(jnp.int32)) + route[:, 2:4].astype(jnp.int32)
    n_tiles = -(-2 * r // tmx) + N_EXPERTS
    starts = jnp.arange(n_tiles, dtype=jnp.int32) * tmx
    tile_expert = jnp.minimum(jnp.sum(ends[None, :] <= starts[:, None], axis=1), N_EXPERTS - 1).astype(jnp.int32)
    tile_valid = (starts < ends[-1]).astype(jnp.int32)
    tm = _pick_tile(r, (512, 256, 128))
    dest3 = dest.reshape(r // tm, 1, 2 * tm)
    xs = _row_scatter(dest3, hp, jnp.zeros((n_tiles * tmx, d // 2), jnp.uint32))
    ys = _grouped_ffn(tile_expert, tile_valid, xs, wg, wu, wd)
    return _moe_out(dest3, h, p, route, ys, (wpg, wpp, g, bb), alpha)


def _pad_lanes(a, width=LANES):
    return jnp.pad(a, ((0, 0), (0, width - a.shape[1])))


def _rot_half_cols(a):
    half = a.shape[-1] // 2
    return jnp.concatenate([-a[..., half:], a[..., :half]], axis=-1)


def _mixer_weights(w_in, b_fox_f, g_cq, w_qb, g_ckv, w_kvb):
    d = w_in.shape[0]
    sizes = (FOX_W, FOX_W, FOX_W, FOX_HEADS, MLA_Q_RANK, MLA_KV_RANK, MLA_ROPE, d, d)
    cols, start = [], 0
    for n in sizes:
        cols.append(w_in[:, start:start + n])
        start += n
    wfq, wfk, wfv, wff, wcq, wckv, wkr, wga, wgb = cols
    w1 = jnp.concatenate([wfq * (FOX_HEAD_DIM ** -0.5 * LOG2E), wfk, wfv], axis=1).astype(BF16)
    w2 = jnp.concatenate([wcq, wckv, _pad_lanes(wkr), _pad_lanes(_rot_half_cols(wkr)), _pad_lanes(wff)],
                         axis=1).astype(BF16)
    wfft = jnp.pad(wff.T, ((0, 16 - FOX_HEADS), (0, 0))).astype(BF16)
    bf = b_fox_f.reshape(1, FOX_HEADS)
    bft = b_fox_f.reshape(FOX_HEADS, 1)

    qb = w_qb.reshape(MLA_Q_RANK, MLA_HEADS, MLA_NOPE + MLA_ROPE)
    q_nope, q_rope = qb[..., :MLA_NOPE], qb[..., MLA_NOPE:]
    q_rot = _rot_half_cols(q_rope)
    z_pad = jnp.zeros((MLA_Q_RANK, MLA_QK_W - 2 * MLA_NOPE - 2 * MLA_ROPE), F32)
    z_nope = jnp.zeros((MLA_Q_RANK, 2 * MLA_NOPE), F32)
    wqa = jnp.concatenate([jnp.concatenate([q_nope[:, 2 * j], q_nope[:, 2 * j + 1], q_rope[:, 2 * j],
                                            q_rope[:, 2 * j + 1], z_pad], axis=1) for j in range(N_PAIRS)], axis=1)
    wqb = jnp.concatenate([jnp.concatenate([z_nope, q_rot[:, 2 * j], q_rot[:, 2 * j + 1], z_pad], axis=1)
                           for j in range(N_PAIRS)], axis=1)

    kvb = w_kvb.reshape(MLA_KV_RANK, MLA_HEADS, MLA_NOPE + MLA_V)
    k_nope, v_up = kvb[..., :MLA_NOPE], kvb[..., MLA_NOPE:]
    zk = jnp.zeros((MLA_KV_RANK, MLA_QK_W - 2 * MLA_NOPE), F32)
    wk = jnp.concatenate([jnp.concatenate([k_nope[:, 2 * j], k_nope[:, 2 * j + 1], zk], axis=1)
                          for j in range(N_PAIRS)], axis=1).astype(BF16)
    eye = jnp.eye(MLA_ROPE, dtype=F32)
    place = jnp.concatenate([jnp.zeros((MLA_ROPE, 2 * MLA_NOPE), F32), eye, eye,
                             jnp.zeros((MLA_ROPE, MLA_QK_W - 2 * MLA_NOPE - 2 * MLA_ROPE), F32)], axis=1)
    place = jnp.tile(place, (1, N_PAIRS)).astype(BF16)
    wv = v_up.reshape(MLA_KV_RANK, MLA_W).astype(BF16)
    proj_w = (w1, w2, wfft, wfv.T.astype(BF16), bf, bft, g_cq.reshape(1, -1), g_ckv.reshape(1, -1),
              wqa.astype(BF16), wqb.astype(BF16))
    return proj_w, (wk, place, wv.T), (wga.astype(BF16), wgb.astype(BF16))


def _rope_tables(pos):
    half = MLA_ROPE // 2
    inv = ROPE_THETA ** (-jnp.arange(half, dtype=F32) * 2.0 / MLA_ROPE)
    ang = pos.astype(F32)[:, None] * inv[None, :]
    cos2 = jnp.concatenate([jnp.cos(ang)] * 2, axis=1)
    sin2 = jnp.concatenate([jnp.sin(ang)] * 2, axis=1)
    n = pos.shape[0]
    scale = (MLA_NOPE + MLA_ROPE) ** -0.5 * LOG2E
    pad = jnp.zeros((n, MLA_QK_W - 2 * MLA_NOPE - 2 * MLA_ROPE), F32)
    ctab = jnp.concatenate([jnp.full((n, 2 * MLA_NOPE), scale, F32), scale * cos2, scale * cos2, pad], axis=1)
    stab = jnp.concatenate([jnp.zeros((n, 2 * MLA_NOPE), F32), scale * sin2, scale * sin2, pad], axis=1)
    return jnp.tile(ctab, (1, N_PAIRS)), jnp.tile(stab, (1, N_PAIRS)), cos2, sin2


def _pad_time(a, tp, axis):
    pad = [(0, 0)] * a.ndim
    pad[axis] = (0, tp - a.shape[axis])
    return jnp.pad(a, pad)


def _layer(x, past, p, proj_w, kvx_w, mix_w, ffn_w, is_moe, alpha):
    b, s, d = x.shape
    n_past = 0 if past is None else past[0].shape[1]
    t = n_past + s
    tabs = _rope_tables(n_past + jnp.arange(s))
    fq, fk, fv, fkb, fvt, logf, logft, ckv, kr, qp = _proj(x, proj_w, tabs)
    if past is None:
        tp = t
        k_fox, vt_fox, logft_all, ckv_all, kr_all = fkb, fvt, logft, ckv, kr
    else:
        tp = -(-t // LANES) * LANES
        pk, pv, plogf, pckv, pkr = past
        k_fox = _pad_time(jnp.concatenate([pk.reshape(b, n_past, FOX_W).astype(BF16), fkb], axis=1), tp, 1)
        pvt = jnp.swapaxes(pv.reshape(b, n_past, FOX_W), 1, 2).astype(BF16)
        vt_fox = _pad_time(jnp.concatenate([pvt, fvt], axis=2), tp, 2)
        logft_all = _pad_time(jnp.concatenate([jnp.swapaxes(plogf, 1, 2), logft], axis=2), tp, 2)
        ckv_all = _pad_time(jnp.concatenate([pckv, ckv], axis=1), tp, 1)
        kr_all = _pad_time(jnp.concatenate([pkr, kr], axis=1), tp, 1)
    nb_aug = _neg_cumsum(logft_all)
    k_mla, vt_mla = _kv_expand(ckv_all, kr_all, *kvx_w)
    sq = -(-s // LANES) * LANES
    fq_p, qp_p = _pad_time(fq, sq, 1), _pad_time(qp, sq, 1)
    o_fox = _attention_t(fq_p, k_fox, vt_fox, nb_aug, q_off=n_past, t_valid=t, chunked=False)[:, :s]
    o_mla = _attention_t(qp_p, k_mla, vt_mla, None, q_off=n_past, t_valid=t, chunked=True)[:, :s]
    r = b * s
    h = _mix(x.reshape(r, d), o_fox.reshape(r, FOX_W), o_mla.reshape(r, MLA_W), mix_w, alpha)
    ffn = _moe_ffn if is_moe else _dense_ffn
    y = ffn(h, p.reshape(r, -1), ffn_w, alpha).reshape(b, s, d)
    new_rows = (fk.reshape(b, s, FOX_HEADS, FOX_HEAD_DIM), fv.reshape(b, s, FOX_HEADS, FOX_HEAD_DIM), logf, ckv, kr)
    return y, new_rows


def kernel(x_prompt, x_sample, cache_fox_k, cache_fox_v, cache_fox_logf, cache_mla_ckv, cache_mla_krope,
           p_prompt, p_sample, w_in, b_fox_f, g_mla_cq, w_mla_qb, g_mla_ckv, w_mla_kvb, w_o_fox, w_o_mla,
           w_out, ln_mix_g, ln_mix_b, w_ffn_gate, w_ffn_up, w_ffn_down, w_router, b_router, w_moe_gate,
           w_moe_up, w_moe_down, w_ple_proj, w_ple_gate, ln_ffn_g, ln_ffn_b):
    depth = w_in.shape[0]
    alpha = (2 * depth) ** 0.25
    hp, hs = x_prompt, x_sample
    rows_p, rows_s = [], []
    for i in range(depth):
        proj_w, kvx_w, (wga, wgb) = _mixer_weights(w_in[i], b_fox_f[i], g_mla_cq[i], w_mla_qb[i], g_mla_ckv[i],
                                                   w_mla_kvb[i])
        mix_w = (wga, wgb, w_o_fox[i].astype(BF16), w_o_mla[i].astype(BF16), w_out[i].astype(BF16),
                 ln_mix_g[i].reshape(1, -1), ln_mix_b[i].reshape(1, -1))
        tail = (w_ple_gate[i].astype(BF16), w_ple_proj[i].astype(BF16),
                ln_ffn_g[i].reshape(1, -1), ln_ffn_b[i].reshape(1, -1))
        j = i // 2
        is_moe = i % 2 == 1
        if is_moe:
            ffn_w = (_pad_lanes(w_router[j]).astype(BF16), _pad_lanes(b_router[j].reshape(1, -1)),
                     w_moe_gate[j].astype(BF16), w_moe_up[j].astype(BF16), w_moe_down[j].astype(BF16)) + tail
        else:
            ffn_w = (w_ffn_gate[j].astype(BF16), w_ffn_up[j].astype(BF16), w_ffn_down[j].astype(BF16)) + tail
        past = (cache_fox_k[i], cache_fox_v[i], cache_fox_logf[i], cache_mla_ckv[i], cache_mla_krope[i])
        hp, new_p = _layer(hp, None, p_prompt[i], proj_w, kvx_w, mix_w, ffn_w, is_moe, alpha)
        hs, new_s = _layer(hs, past, p_sample[i], proj_w, kvx_w, mix_w, ffn_w, is_moe, alpha)
        rows_p.append(new_p)
        rows_s.append(new_s)

    def stack(rows, idx):
        return jnp.stack([r[idx] for r in rows], axis=0)

    return (hp, hs) + tuple(stack(rows_p, k) for k in range(5)) + tuple(stack(rows_s, k) for k in range(5))
```

```python
import functools

import jax
import jax.numpy as jnp
from jax import lax
from jax.experimental import pallas as pl
from jax.experimental.pallas import tpu as pltpu

CHUNK = 64
FOX_HEADS = 8
FOX_HEAD_DIM = 64
FOX_W = FOX_HEADS * FOX_HEAD_DIM
MLA_HEADS = 8
MLA_Q_RANK = 256
MLA_KV_RANK = 128
MLA_NOPE = 64
MLA_ROPE = 32
MLA_V = 64
MLA_W = MLA_HEADS * MLA_V
ROPE_THETA = 10000.0
N_EXPERTS = 8
LN_EPS = 1e-5
RMS_EPS = 1e-6
NEG_INF = -1e30
LOG2E = 1.4426950408889634

LANES = 128
SUBLANES = 8
PAIR_W = 2 * MLA_V
MLA_QK_W = 256
N_PAIRS = FOX_HEADS // 2
V7X_SCOPED_VMEM_BYTES = 60000 * 1024

BF16 = jnp.bfloat16
F32 = jnp.float32


def _dot(a, b):
    return jnp.dot(a, b, preferred_element_type=F32)


def _dot_nt(a, b):
    return lax.dot_general(a, b, (((1,), (1,)), ((), ())), preferred_element_type=F32)


def _sigmoid(x):
    return 1.0 / (1.0 + jnp.exp(-x))


def _log_sigmoid(x):
    return jnp.minimum(x, 0.0) - jnp.log1p(jnp.exp(-jnp.abs(x)))


def _rms_norm(x, g):
    return x * lax.rsqrt(jnp.mean(jnp.square(x), axis=-1, keepdims=True) + RMS_EPS) * g


def _layer_norm(x, g, b):
    mu = jnp.mean(x, axis=-1, keepdims=True)
    xc = x - mu
    var = jnp.mean(jnp.square(xc), axis=-1, keepdims=True)
    return xc * lax.rsqrt(var + LN_EPS) * g + b


def _resident(shape):
    nd = len(shape)
    return pl.BlockSpec(shape, lambda *_: (0,) * nd, pipeline_mode=pl.Buffered(1))


def _params(semantics, vmem_bytes):
    return pltpu.CompilerParams(dimension_semantics=semantics,
                                vmem_limit_bytes=min(int(vmem_bytes), V7X_SCOPED_VMEM_BYTES))


def _pick_tile(n, candidates):
    for c in candidates:
        if n % c == 0:
            return c
    return n


def _proj_kernel(x_ref, w1_ref, w2_ref, wfft_ref, wfvt_ref, bf_ref, bft_ref, gcq_ref, gckv_ref, wqa_ref, wqb_ref,
                 ctab_ref, stab_ref, cos_ref, sin_ref, *rest, expand_kv):
    if expand_kv:
        wk_ref, place_ref, wvt_ref = rest[:3]
        rest = rest[3:]
    fq_ref, fk_ref, fv_ref, fkb_ref, fvt_ref, logf_ref, logft_ref, ckv_ref, kr_ref, qp_ref = rest[:10]
    xb = x_ref[0].astype(BF16)
    z1 = _dot(xb, w1_ref[...])
    fq_ref[0] = z1[:, :FOX_W].astype(BF16)
    fk = z1[:, FOX_W:2 * FOX_W]
    fk_ref[0] = fk
    fkb_ref[0] = fk.astype(BF16)
    fv_ref[0] = z1[:, 2 * FOX_W:]
    fvt_ref[0] = _dot_nt(wfvt_ref[...], xb).astype(BF16)

    z2 = _dot(xb, w2_ref[...])
    cq = z2[:, :MLA_Q_RANK]
    o = MLA_Q_RANK
    ckv = z2[:, o:o + MLA_KV_RANK]
    o += MLA_KV_RANK
    kr = z2[:, o:o + MLA_ROPE]
    krr = z2[:, o + LANES:o + LANES + MLA_ROPE]
    ff = z2[:, o + 2 * LANES:o + 2 * LANES + FOX_HEADS]
    logf_ref[0] = _log_sigmoid(ff + bf_ref[...])
    fft = _dot_nt(wfft_ref[...], xb)
    logft_ref[0] = _log_sigmoid(fft[:FOX_HEADS] + bft_ref[...])
    ckv_n = _rms_norm(ckv, gckv_ref[...])
    ckv_ref[0] = ckv_n
    kr_rot = kr * cos_ref[...] + krr * sin_ref[...]
    kr_ref[0] = kr_rot
    cqn = _rms_norm(cq, gcq_ref[...]).astype(BF16)
    qp = _dot(cqn, wqa_ref[...]) * ctab_ref[...] + _dot(cqn, wqb_ref[...]) * stab_ref[...]
    qp_ref[0] = qp.astype(BF16)
    if expand_kv:
        kp_ref, vmt_ref = rest[10:]
        _expand_kv(ckv_n, kr_rot, wk_ref, place_ref, wvt_ref, kp_ref, vmt_ref)


def _expand_kv(ckv, kr, wk_ref, place_ref, wvt_ref, kp_ref, vmt_ref):
    cb = ckv.astype(BF16)
    kp_ref[0] = (_dot(cb, wk_ref[...]) + _dot(kr.astype(BF16), place_ref[...])).astype(BF16)
    vmt_ref[0] = _dot_nt(wvt_ref[...], cb).astype(BF16)


def _proj(x, w, tabs, kvx_w=None):
    b, s, d = x.shape
    tm = _pick_tile(s, (512, 256, 128))
    ns = s // tm
    w1, w2, wfft, wfvt, bf, bft, gcq, gckv, wqa, wqb = w
    ctab, stab, cos2, sin2 = tabs
    qw = N_PAIRS * MLA_QK_W

    def tok(width):
        return pl.BlockSpec((1, tm, width), lambda si, bi: (bi, si, 0))

    def tok_t(height):
        return pl.BlockSpec((1, height, tm), lambda si, bi: (bi, 0, si))

    def tab(width):
        return pl.BlockSpec((tm, width), lambda si, bi: (si, 0))

    in_specs = [tok(d), _resident(w1.shape), _resident(w2.shape), _resident(wfft.shape), _resident(wfvt.shape),
                _resident(bf.shape), _resident(bft.shape), _resident(gcq.shape), _resident(gckv.shape),
                _resident(wqa.shape), _resident(wqb.shape), tab(qw), tab(qw), tab(MLA_ROPE), tab(MLA_ROPE)]
    out_shape = (
        jax.ShapeDtypeStruct((b, s, FOX_W), BF16),
        jax.ShapeDtypeStruct((b, s, FOX_W), F32),
        jax.ShapeDtypeStruct((b, s, FOX_W), F32),
        jax.ShapeDtypeStruct((b, s, FOX_W), BF16),
        jax.ShapeDtypeStruct((b, FOX_W, s), BF16),
        jax.ShapeDtypeStruct((b, s, FOX_HEADS), F32),
        jax.ShapeDtypeStruct((b, FOX_HEADS, s), F32),
        jax.ShapeDtypeStruct((b, s, MLA_KV_RANK), F32),
        jax.ShapeDtypeStruct((b, s, MLA_ROPE), F32),
        jax.ShapeDtypeStruct((b, s, qw), BF16),
    )
    out_specs = (tok(FOX_W), tok(FOX_W), tok(FOX_W), tok(FOX_W), tok_t(FOX_W), tok(FOX_HEADS),
                 tok_t(FOX_HEADS), tok(MLA_KV_RANK), tok(MLA_ROPE), tok(qw))
    args = (x, w1, w2, wfft, wfvt, bf, bft, gcq, gckv, wqa, wqb, ctab, stab, cos2, sin2)
    if kvx_w is not None:
        in_specs += [_resident(a.shape) for a in kvx_w]
        args += tuple(kvx_w)
        out_shape += (jax.ShapeDtypeStruct((b, s, qw), BF16), jax.ShapeDtypeStruct((b, MLA_W, s), BF16))
        out_specs += (tok(qw), tok_t(MLA_W))
    return pl.pallas_call(
        functools.partial(_proj_kernel, expand_kv=kvx_w is not None), grid=(ns, b), in_specs=in_specs,
        out_specs=out_specs, out_shape=out_shape,
        compiler_params=_params(("arbitrary", "arbitrary"), 56 << 20), name="proj",
    )(*args)


def _cumsum_kernel(x_ref, aug_ref, *, ch):
    t = x_ref.shape[2]
    lane = lax.broadcasted_iota(jnp.int32, (1, LANES), 1)
    r = lax.broadcasted_iota(jnp.int32, (ch, ch), 0)
    c = lax.broadcasted_iota(jnp.int32, (ch, ch), 1)
    upper = jnp.where(r <= c, 1.0, 0.0).astype(BF16)
    carry = jnp.zeros((FOX_HEADS, 1), F32)
    zeros = jnp.zeros((FOX_HEADS, ch), F32)
    for ci in range(t // ch):
        xc = x_ref[0, :, ci * ch:(ci + 1) * ch]
        hi = xc.astype(BF16).astype(F32)
        r1 = xc - hi
        mid = r1.astype(BF16).astype(F32)
        lo = (r1 - mid).astype(BF16).astype(F32)
        pieces = jnp.concatenate([hi, mid, lo, zeros], axis=0).astype(BF16)
        pc = _dot(pieces, upper)
        cum = pc[0:8] + pc[8:16] + pc[16:24] + carry
        nb = cum * (-LOG2E)
        for hh in range(FOX_HEADS):
            rep = jnp.broadcast_to(nb[hh:hh + 1, :], (LANES, ch)).T
            a_hi = rep.astype(BF16).astype(F32)
            a_r = rep - a_hi
            a_mid = a_r.astype(BF16).astype(F32)
            a_lo = (a_r - a_mid).astype(BF16).astype(F32)
            aug = jnp.where(lane == 0, a_hi, jnp.where(lane == 1, a_mid, jnp.where(lane == 2, a_lo, 0.0)))
            aug_ref[0, hh, ci * ch:(ci + 1) * ch, :] = aug.astype(BF16)
        carry = cum[:, ch - 1:ch]


def _neg_cumsum(logft):
    b, h, t = logft.shape
    ch = 256 if t % 256 == 0 else LANES
    spec = pl.BlockSpec((1, h, t), lambda bi: (bi, 0, 0))
    return pl.pallas_call(
        functools.partial(_cumsum_kernel, ch=ch), grid=(b,), in_specs=[spec],
        out_specs=pl.BlockSpec((1, h, t, LANES), lambda bi: (bi, 0, 0, 0)),
        out_shape=jax.ShapeDtypeStruct((b, h, t, LANES), BF16),
        compiler_params=_params(("arbitrary",), 32 << 20), name="cumsum",
    )(logft)


def _kvx_kernel(ckv_ref, kr_ref, wk_ref, place_ref, wvt_ref, kp_ref, vmt_ref):
    _expand_kv(ckv_ref[0], kr_ref[0], wk_ref, place_ref, wvt_ref, kp_ref, vmt_ref)


def _kv_expand(ckv, kr, wk, place, wvt):
    b, t, _ = ckv.shape
    tm = _pick_tile(t, (512, 384, 256, 128))
    kw = N_PAIRS * MLA_QK_W

    def tok(width):
        return pl.BlockSpec((1, tm, width), lambda bi, ti: (bi, ti, 0))

    return pl.pallas_call(
        _kvx_kernel, grid=(b, t // tm),
        in_specs=[tok(MLA_KV_RANK), tok(MLA_ROPE), _resident(wk.shape), _resident(place.shape), _resident(wvt.shape)],
        out_specs=(tok(kw), pl.BlockSpec((1, MLA_W, tm), lambda bi, ti: (bi, 0, ti))),
        out_shape=(jax.ShapeDtypeStruct((b, t, kw), BF16), jax.ShapeDtypeStruct((b, MLA_W, t), BF16)),
        compiler_params=_params(("arbitrary", "arbitrary"), 32 << 20), name="kvexpand",
    )(ckv, kr, wk, place, wvt)


def _head_lane_mask(lane, hh, width):
    if width == LANES:
        return (lane // FOX_HEAD_DIM) == hh
    nope = (lane < 2 * MLA_NOPE) & ((lane // MLA_NOPE) == hh)
    rope = (lane >= 2 * MLA_NOPE) & (lane < 2 * MLA_NOPE + 2 * MLA_ROPE) & (
        ((lane - 2 * MLA_NOPE) // MLA_ROPE) == hh)
    return nope | rope


def _attn_t_kernel(*refs, tq, tk, width, q_off, t_valid, chunked, has_bias):
    if has_bias:
        q_ref, k_ref, vt_ref, nb_ref, o_ref, sa_ref, sb_ref = refs
    else:
        q_ref, k_ref, vt_ref, o_ref, sa_ref, sb_ref = refs
        nb_ref = None
    nq = q_ref.shape[1] // tq
    nkb = k_ref.shape[1] // tk

    items = []
    for qi in range(nq):
        q_min = q_off + qi * tq
        q_max = q_min + tq - 1
        if chunked:
            lim_min, lim_max = (q_min // CHUNK + 1) * CHUNK, (q_max // CHUNK + 1) * CHUNK
        else:
            lim_min, lim_max = q_min + 1, q_max + 1
        n_full = min(lim_min, t_valid) // tk
        n_vis = min(-(-min(lim_max, t_valid) // tk), nkb)
        items += [(qi, kb, kb >= n_full, kb == n_vis - 1) for kb in range(n_vis)]

    lane = lax.broadcasted_iota(jnp.int32, (1, width), 1)
    keep = [_head_lane_mask(lane, hh, width) for hh in range(2)]
    bufs = (sa_ref, sb_ref)

    ones3 = jnp.where(lax.broadcasted_iota(jnp.int32, (tq, LANES), 1) < 3, 1.0, 0.0).astype(BF16)
    ones_rows = jnp.ones((16, tk), BF16)

    def scores_into(s_ref, qi, kb):
        q = q_ref[0, qi * tq:(qi + 1) * tq, :]
        kblk = k_ref[0, kb * tk:(kb + 1) * tk, :]
        for hh in range(2):
            qh = jnp.where(keep[hh], q, jnp.zeros_like(q))
            if has_bias:
                kh = jnp.concatenate([kblk, nb_ref[0, hh, kb * tk:(kb + 1) * tk, :]], axis=1)
                s_ref[hh] = _dot_nt(kh, jnp.concatenate([qh, ones3], axis=1))
            else:
                s_ref[hh] = _dot_nt(kblk, qh)

    def consume(s_ref, qi, kb, stats, masked):
        if masked:
            qpos = q_off + qi * tq + lax.broadcasted_iota(jnp.int32, (1, tq), 1)
            if chunked:
                col_lim = lax.shift_left(lax.shift_right_logical(qpos, 6) + 1, 6)
            else:
                col_lim = qpos + 1
            col_lim = jnp.minimum(col_lim, t_valid)
            kpos = kb * tk + lax.broadcasted_iota(jnp.int32, (tk, 1), 0)
        out = []
        for hh in range(2):
            m, acc = stats[2 * hh:2 * hh + 2]
            s = s_ref[hh]
            if masked:
                s = jnp.where(kpos < col_lim, s, NEG_INF)
            m_new = jnp.maximum(m, jnp.max(s, axis=0, keepdims=True))
            p = jnp.exp2(s - m_new)
            alpha = jnp.exp2(m - m_new)
            vth = jnp.concatenate([vt_ref[0, hh * MLA_V:(hh + 1) * MLA_V, kb * tk:(kb + 1) * tk], ones_rows], axis=0)
            acc = alpha * acc + _dot(vth, p.astype(BF16))
            out += [m_new, acc]
        return tuple(out)

    scores_into(bufs[0], items[0][0], items[0][1])
    stats = None
    for i, (qi, kb, masked, last) in enumerate(items):
        if i + 1 < len(items):
            scores_into(bufs[(i + 1) % 2], items[i + 1][0], items[i + 1][1])
        if kb == 0:
            stats = (jnp.full((1, tq), NEG_INF, F32), jnp.zeros((MLA_V + 16, tq), F32)) * 2
        stats = consume(bufs[i % 2], qi, kb, stats, masked)
        if last:
            outs = [stats[2 * hh + 1][:MLA_V] * (1.0 / stats[2 * hh + 1][MLA_V:MLA_V + 1]) for hh in range(2)]
            o_ref[0, qi * tq:(qi + 1) * tq, :] = jnp.concatenate(outs, axis=0).T.astype(BF16)


def _attention_t(q, k, vt, nbrep, *, q_off, t_valid, chunked):
    assert CHUNK == 64
    b, sq, qw = q.shape
    width = qw // N_PAIRS
    tp = k.shape[1]
    tq = _pick_tile(sq, (512, 256, 128))
    tk = _pick_tile(tp, (512, 384, 256, 128))
    in_specs = [pl.BlockSpec((1, sq, width), lambda bi, pi: (bi, 0, pi)),
                pl.BlockSpec((1, tp, width), lambda bi, pi: (bi, 0, pi)),
                pl.BlockSpec((1, PAIR_W, tp), lambda bi, pi: (bi, pi, 0))]
    args = [q, k, vt]
    if nbrep is not None:
        in_specs.append(pl.BlockSpec((1, 2, tp, LANES), lambda bi, pi: (bi, pi, 0, 0)))
        args.append(nbrep)
    kern = functools.partial(_attn_t_kernel, tq=tq, tk=tk, width=width, q_off=q_off, t_valid=t_valid,
                             chunked=chunked, has_bias=nbrep is not None)
    return pl.pallas_call(
        kern, grid=(b, N_PAIRS), in_specs=in_specs,
        out_specs=pl.BlockSpec((1, sq, PAIR_W), lambda bi, pi: (bi, 0, pi)),
        out_shape=jax.ShapeDtypeStruct((b, sq, N_PAIRS * PAIR_W), BF16),
        scratch_shapes=[pltpu.VMEM((2, tk, tq), F32), pltpu.VMEM((2, tk, tq), F32)],
        compiler_params=_params(("arbitrary", "arbitrary"), 40 << 20),
        name="attn_t_mla" if chunked else "attn_t_fox",
    )(*args)


def _mix_kernel(x_ref, of_ref, om_ref, wga_ref, wgb_ref, wof_ref, wom_ref, wout_ref, g_ref, b_ref, h_ref, *, alpha):
    x = x_ref[...]
    xb = x.astype(BF16)
    a = _sigmoid(_dot(xb, wga_ref[...])) * _dot(of_ref[...], wof_ref[...])
    bm = _sigmoid(_dot(xb, wgb_ref[...])) * _dot(om_ref[...], wom_ref[...])
    m = _dot((a + bm).astype(BF16), wout_ref[...])
    h_ref[...] = _layer_norm(alpha * x + m, g_ref[...], b_ref[...])


def _mix(x, of, om, w, alpha):
    r, d = x.shape
    tm = _pick_tile(r, (512, 256, 128))

    def row(width):
        return pl.BlockSpec((tm, width), lambda i: (i, 0))

    return pl.pallas_call(
        functools.partial(_mix_kernel, alpha=alpha), grid=(r // tm,),
        in_specs=[row(d), row(FOX_W), row(MLA_W)] + [_resident(a.shape) for a in w],
        out_specs=row(d), out_shape=jax.ShapeDtypeStruct((r, d), F32),
        compiler_params=_params(("arbitrary",), 48 << 20), name="mix",
    )(x, of, om, *w)


def _ple(hb, p_ref, wpg_ref, wpp_ref):
    return _sigmoid(_dot(hb, wpg_ref[...])) * _dot(p_ref[...].astype(BF16), wpp_ref[...])


def _ffn_chunk(n_ff):
    return _pick_tile(n_ff, (1408, 1024, 512, 256, 128))


def _dense_ffn_kernel(h_ref, p_ref, wg_ref, wu_ref, wd_ref, wpg_ref, wpp_ref, g_ref, b_ref, y_ref, *, alpha, fc):
    h = h_ref[...]
    hb = h.astype(BF16)
    n_ff = wg_ref.shape[1]
    acc = jnp.zeros(h.shape, F32)
    for c in range(n_ff // fc):
        sl = slice(c * fc, (c + 1) * fc)
        gt = _dot(hb, wg_ref[:, sl])
        act = (gt * _sigmoid(gt) * _dot(hb, wu_ref[:, sl])).astype(BF16)
        acc = acc + _dot(act, wd_ref[sl, :])
    y = alpha * h + acc + _ple(hb, p_ref, wpg_ref, wpp_ref)
    y_ref[...] = _layer_norm(y, g_ref[...], b_ref[...])


def _dense_ffn(h, p, w, alpha):
    r, d = h.shape
    tm = _pick_tile(r, (512, 256, 128))
    wg, wu, wd, wpg, wpp, g, bb = w

    def row(width):
        return pl.BlockSpec((tm, width), lambda i: (i, 0))

    return pl.pallas_call(
        functools.partial(_dense_ffn_kernel, alpha=alpha, fc=_ffn_chunk(wg.shape[1])), grid=(r // tm,),
        in_specs=[row(d), row(p.shape[1])] + [_resident(a.shape) for a in w],
        out_specs=row(d), out_shape=jax.ShapeDtypeStruct((r, d), F32),
        compiler_params=_params(("arbitrary",), 56 << 20), name="ffn_dense",
    )(h, p, wg, wu, wd, wpg, wpp, g, bb)


MOE_ROW_TILE = 512
ROUTE_LANES = 6


def _route_kernel(h_ref, wr_ref, br_ref, route_ref, hp_ref, counts_ref, carry_ref):
    @pl.when(pl.program_id(0) == 0)
    def _init():
        carry_ref[...] = jnp.zeros(carry_ref.shape, F32)

    hb = h_ref[...].astype(BF16)
    tm, d = hb.shape
    lane = lax.broadcasted_iota(jnp.int32, (1, LANES), 1)
    logits = _dot(hb, wr_ref[...]) + br_ref[...]
    lg = jnp.where(lane < N_EXPERTS, logits, -jnp.inf)
    m1 = jnp.max(lg, axis=-1, keepdims=True)
    i1 = jnp.min(jnp.where(lg == m1, lane, LANES), axis=-1, keepdims=True)
    lg2 = jnp.where(lane == i1, -jnp.inf, lg)
    m2 = jnp.max(lg2, axis=-1, keepdims=True)
    i2 = jnp.min(jnp.where(lg2 == m2, lane, LANES), axis=-1, keepdims=True)
    e2 = jnp.exp(m2 - m1)
    den = 1.0 + e2
    hit1 = lane == i1
    hit2 = lane == i2
    onehot = jnp.where(hit1, 1.0, 0.0) + jnp.where(hit2, 1.0, 0.0)
    rr = lax.broadcasted_iota(jnp.int32, (tm, tm), 0)
    cc = lax.broadcasted_iota(jnp.int32, (tm, tm), 1)
    lower = jnp.where(cc < rr, 1.0, 0.0).astype(BF16)
    prefix = _dot(lower, onehot.astype(BF16)) + carry_ref[...]
    rank1 = jnp.sum(jnp.where(hit1, prefix, 0.0), axis=-1, keepdims=True)
    rank2 = jnp.sum(jnp.where(hit2, prefix, 0.0), axis=-1, keepdims=True)
    fields = (i1.astype(F32), i2.astype(F32), rank1, rank2, 1.0 / den, e2 / den)
    route = jnp.zeros((tm, LANES), F32)
    for j, f in enumerate(fields):
        route = jnp.where(lane == j, f, route)
    route_ref[...] = route
    carry = carry_ref[...] + jnp.sum(onehot, axis=0, keepdims=True)
    carry_ref[...] = carry
    counts_ref[...] = carry
    bits = lax.bitcast_convert_type(hb.astype(F32), jnp.uint32)
    hp_ref[...] = (bits[:, d // 2:] & jnp.uint32(0xFFFF0000)) | lax.shift_right_logical(bits[:, :d // 2], jnp.uint32(16))


def _route(h, wr, br):
    r, d = h.shape
    tm = _pick_tile(r, (512, 256, 128))
    return pl.pallas_call(
        _route_kernel, grid=(r // tm,),
        in_specs=[pl.BlockSpec((tm, d), lambda i: (i, 0)), _resident(wr.shape), _resident(br.shape)],
        out_specs=(pl.BlockSpec((tm, LANES), lambda i: (i, 0)), pl.BlockSpec((tm, d // 2), lambda i: (i, 0)),
                   pl.BlockSpec((1, LANES), lambda i: (0, 0))),
        out_shape=(jax.ShapeDtypeStruct((r, LANES), F32), jax.ShapeDtypeStruct((r, d // 2), jnp.uint32),
                   jax.ShapeDtypeStruct((1, LANES), F32)),
        scratch_shapes=[pltpu.VMEM((1, LANES), F32)],
        compiler_params=_params(("arbitrary",), 32 << 20), name="moe_route",
    )(h, wr, br)


ROW_DMA_UNROLL = 8


def _row_scatter_kernel(idx_ref, src_ref, init_ref, dst_ref, sem, *, tm):
    del init_ref

    def copy(t, d_row):
        return pltpu.make_async_copy(src_ref.at[pl.ds(t, 1)], dst_ref.at[pl.ds(d_row, 1)], sem)

    def issue(t, c):
        copy(t, idx_ref[0, 0, 2 * t]).start(priority=0)
        copy(t, idx_ref[0, 0, 2 * t + 1]).start(priority=1)
        return c

    lax.fori_loop(0, tm, issue, 0, unroll=ROW_DMA_UNROLL)
    for _ in range(2):
        pltpu.make_async_copy(src_ref, dst_ref.at[pl.ds(0, tm)], sem).wait()


def _row_scatter(idx3, src, init):
    steps, _, n = idx3.shape
    tm = n // 2
    any_spec = pl.BlockSpec(memory_space=pl.ANY)
    return pl.pallas_call(
        functools.partial(_row_scatter_kernel, tm=tm), grid=(steps,),
        in_specs=[pl.BlockSpec((1, 1, n), lambda i: (i, 0, 0), memory_space=pltpu.SMEM),
                  pl.BlockSpec((tm, src.shape[1]), lambda i: (i, 0)), any_spec],
        out_specs=any_spec, out_shape=jax.ShapeDtypeStruct(init.shape, init.dtype),
        scratch_shapes=[pltpu.SemaphoreType.DMA(())], input_output_aliases={2: 0},
        compiler_params=_params(("arbitrary",), 16 << 20), name="moe_scatter",
    )(idx3, src, init)


def _grouped_ffn_kernel(te_ref, tv_ref, x_ref, wg_ref, wu_ref, wd_ref, y_ref, *, fc):
    del te_ref
    valid = tv_ref[pl.program_id(0)] != 0

    @pl.when(valid)
    def _compute():
        pk = x_ref[...]
        lo = lax.bitcast_convert_type(lax.shift_left(pk, jnp.uint32(16)), F32)
        hi = lax.bitcast_convert_type(pk & jnp.uint32(0xFFFF0000), F32)
        xb = jnp.concatenate([lo, hi], axis=1).astype(BF16)
        n_ff = wg_ref.shape[2]
        tmx = xb.shape[0]
        acc = jnp.zeros((tmx, wd_ref.shape[2]), F32)
        for c in range(n_ff // fc):
            sl = slice(c * fc, (c + 1) * fc)
            gt = _dot(xb, wg_ref[0, :, sl])
            act = (gt * _sigmoid(gt) * _dot(xb, wu_ref[0, :, sl])).astype(BF16)
            acc = acc + _dot(act, wd_ref[0, sl, :])
        for j in range(SUBLANES):
            y_ref[pl.ds(j, tmx, stride=SUBLANES), :] = acc[:, j * LANES:(j + 1) * LANES]

    @pl.when(jnp.logical_not(valid))
    def _skip():
        y_ref[...] = jnp.zeros(y_ref.shape, F32)


def _grouped_ffn(tile_expert, tile_valid, xs, wg, wu, wd):
    rp, half = xs.shape
    n_e, d, n_ff = wg.shape
    tmx = MOE_ROW_TILE
    grid_spec = pltpu.PrefetchScalarGridSpec(
        num_scalar_prefetch=2, grid=(rp // tmx,),
        in_specs=[pl.BlockSpec((tmx, half), lambda i, te, tv: (i, 0)),
                  pl.BlockSpec((1, d, n_ff), lambda i, te, tv: (te[i], 0, 0)),
                  pl.BlockSpec((1, d, n_ff), lambda i, te, tv: (te[i], 0, 0)),
                  pl.BlockSpec((1, n_ff, d), lambda i, te, tv: (te[i], 0, 0))],
        out_specs=pl.BlockSpec((tmx * SUBLANES, LANES), lambda i, te, tv: (i, 0)))
    assert d == SUBLANES * LANES
    return pl.pallas_call(
        functools.partial(_grouped_ffn_kernel, fc=_ffn_chunk(n_ff)), grid_spec=grid_spec,
        out_shape=jax.ShapeDtypeStruct((rp * SUBLANES, LANES), F32),
        compiler_params=_params(("arbitrary",), V7X_SCOPED_VMEM_BYTES), name="moe_ffn",
    )(tile_expert, tile_valid, xs, wg, wu, wd)


def _moe_out_kernel(idx_ref, idx_next_ref, h_ref, p_ref, route_ref, ys_ref, wpg_ref, wpp_ref, g_ref, b_ref, y_ref,
                    buf_ref, sem, *, alpha):
    tm = h_ref.shape[0]
    step = pl.program_id(0)
    slot = lax.rem(step, 2)

    def copy(sl, k, t, s_row):
        src = ys_ref.at[pl.ds(pl.multiple_of(s_row * SUBLANES, SUBLANES), SUBLANES)]
        dst = buf_ref.at[sl, k, pl.ds(pl.multiple_of(t * SUBLANES, SUBLANES), SUBLANES)]
        return pltpu.make_async_copy(src, dst, sem.at[sl])

    def issue_block(rows_ref, sl):
        def issue(t, c):
            copy(sl, 0, t, rows_ref[0, 0, 2 * t]).start(priority=0)
            copy(sl, 1, t, rows_ref[0, 0, 2 * t + 1]).start(priority=1)
            return c

        lax.fori_loop(0, tm, issue, 0, unroll=ROW_DMA_UNROLL)

    @pl.when(step == 0)
    def _first():
        issue_block(idx_ref, 0)

    @pl.when(step + 1 < pl.num_programs(0))
    def _ahead():
        issue_block(idx_next_ref, 1 - slot)

    h = h_ref[...]
    hb = h.astype(BF16)
    ple = _ple(hb, p_ref, wpg_ref, wpp_ref)

    for k in range(2):
        pltpu.make_async_copy(ys_ref.at[pl.ds(0, tm * SUBLANES)], buf_ref.at[slot, k], sem.at[slot]).wait()

    def rows(k):
        view = buf_ref.at[slot, k]
        return jnp.concatenate([view[pl.ds(j, tm, stride=SUBLANES), :] for j in range(SUBLANES)], axis=1)

    rt = route_ref[...]
    moe = rt[:, 4:5] * rows(0) + rt[:, 5:6] * rows(1)
    y_ref[...] = _layer_norm(alpha * h + moe + ple, g_ref[...], b_ref[...])


def _moe_out(idx3, h, p, route, ys, w, alpha):
    r, d = h.shape
    steps, _, n = idx3.shape
    tm = n // 2

    def row(width):
        return pl.BlockSpec((tm, width), lambda i: (i, 0))

    return pl.pallas_call(
        functools.partial(_moe_out_kernel, alpha=alpha), grid=(steps,),
        in_specs=[pl.BlockSpec((1, 1, n), lambda i: (i, 0, 0), memory_space=pltpu.SMEM),
                  pl.BlockSpec((1, 1, n), lambda i: (jnp.minimum(i + 1, steps - 1), 0, 0), memory_space=pltpu.SMEM),
                  row(d), row(p.shape[1]), row(LANES), pl.BlockSpec(memory_space=pl.ANY)]
        + [_resident(a.shape) for a in w],
        out_specs=row(d), out_shape=jax.ShapeDtypeStruct((r, d), F32),
        scratch_shapes=[pltpu.VMEM((2, 2, tm * SUBLANES, LANES), F32), pltpu.SemaphoreType.DMA((2,))],
        compiler_params=_params(("arbitrary",), 48 << 20), name="moe_out",
    )(idx3, idx3, h, p, route, ys, *w)


def _moe_ffn(h, p, w, alpha):
    r, d = h.shape
    wr, br, wg, wu, wd, wpg, wpp, g, bb = w
    tmx = MOE_ROW_TILE
    route, hp, counts = _route(h, wr, br)
    cnt = counts[0, :N_EXPERTS].astype(jnp.int32)
    padded = (cnt + tmx - 1) // tmx * tmx
    ends = jnp.cumsum(padded)
    dest = jnp.take(ends - padded, route[:, 0:2].astype(jnp.int32)) + route[:, 2:4].astype(jnp.int32)
    n_tiles = -(-2 * r // tmx) + N_EXPERTS
    starts = jnp.arange(n_tiles, dtype=jnp.int32) * tmx
    tile_expert = jnp.minimum(jnp.sum(ends[None, :] <= starts[:, None], axis=1), N_EXPERTS - 1).astype(jnp.int32)
    tile_valid = (starts < ends[-1]).astype(jnp.int32)
    tm = _pick_tile(r, (512, 256, 128))
    dest3 = dest.reshape(r // tm, 1, 2 * tm)
    xs = _row_scatter(dest3, hp, jnp.zeros((n_tiles * tmx, d // 2), jnp.uint32))
    ys = _grouped_ffn(tile_expert, tile_valid, xs, wg, wu, wd)
    return _moe_out(dest3, h, p, route, ys, (wpg, wpp, g, bb), alpha)


def _pad_lanes(a, width=LANES):
    return jnp.pad(a, ((0, 0), (0, width - a.shape[1])))


def _rot_half_cols(a):
    half = a.shape[-1] // 2
    return jnp.concatenate([-a[..., half:], a[..., :half]], axis=-1)


def _mixer_weights(w_in, b_fox_f, g_cq, w_qb, g_ckv, w_kvb):
    d = w_in.shape[0]
    sizes = (FOX_W, FOX_W, FOX_W, FOX_HEADS, MLA_Q_RANK, MLA_KV_RANK, MLA_ROPE, d, d)
    cols, start = [], 0
    for n in sizes:
        cols.append(w_in[:, start:start + n])
        start += n
    wfq, wfk, wfv, wff, wcq, wckv, wkr, wga, wgb = cols
    w1 = jnp.concatenate([wfq * (FOX_HEAD_DIM ** -0.5 * LOG2E), wfk, wfv], axis=1).astype(BF16)
    w2 = jnp.concatenate([wcq, wckv, _pad_lanes(wkr), _pad_lanes(_rot_half_cols(wkr)), _pad_lanes(wff)],
                         axis=1).astype(BF16)
    wfft = jnp.pad(wff.T, ((0, 16 - FOX_HEADS), (0, 0))).astype(BF16)
    bf = b_fox_f.reshape(1, FOX_HEADS)
    bft = b_fox_f.reshape(FOX_HEADS, 1)

    qb = w_qb.reshape(MLA_Q_RANK, MLA_HEADS, MLA_NOPE + MLA_ROPE)
    q_nope, q_rope = qb[..., :MLA_NOPE], qb[..., MLA_NOPE:]
    q_rot = _rot_half_cols(q_rope)
    z_pad = jnp.zeros((MLA_Q_RANK, MLA_QK_W - 2 * MLA_NOPE - 2 * MLA_ROPE), F32)
    z_nope = jnp.zeros((MLA_Q_RANK, 2 * MLA_NOPE), F32)
    wqa = jnp.concatenate([jnp.concatenate([q_nope[:, 2 * j], q_nope[:, 2 * j + 1], q_rope[:, 2 * j],
                                            q_rope[:, 2 * j + 1], z_pad], axis=1) for j in range(N_PAIRS)], axis=1)
    wqb = jnp.concatenate([jnp.concatenate([z_nope, q_rot[:, 2 * j], q_rot[:, 2 * j + 1], z_pad], axis=1)
                           for j in range(N_PAIRS)], axis=1)

    kvb = w_kvb.reshape(MLA_KV_RANK, MLA_HEADS, MLA_NOPE + MLA_V)
    k_nope, v_up = kvb[..., :MLA_NOPE], kvb[..., MLA_NOPE:]
    zk = jnp.zeros((MLA_KV_RANK, MLA_QK_W - 2 * MLA_NOPE), F32)
    wk = jnp.concatenate([jnp.concatenate([k_nope[:, 2 * j], k_nope[:, 2 * j + 1], zk], axis=1)
                          for j in range(N_PAIRS)], axis=1).astype(BF16)
    eye = jnp.eye(MLA_ROPE, dtype=F32)
    place = jnp.concatenate([jnp.zeros((MLA_ROPE, 2 * MLA_NOPE), F32), eye, eye,
                             jnp.zeros((MLA_ROPE, MLA_QK_W - 2 * MLA_NOPE - 2 * MLA_ROPE), F32)], axis=1)
    place = jnp.tile(place, (1, N_PAIRS)).astype(BF16)
    wv = v_up.reshape(MLA_KV_RANK, MLA_W).astype(BF16)
    proj_w = (w1, w2, wfft, wfv.T.astype(BF16), bf, bft, g_cq.reshape(1, -1), g_ckv.reshape(1, -1),
              wqa.astype(BF16), wqb.astype(BF16))
    return proj_w, (wk, place, wv.T), (wga.astype(BF16), wgb.astype(BF16))


def _rope_tables(pos):
    half = MLA_ROPE // 2
    inv = ROPE_THETA ** (-jnp.arange(half, dtype=F32) * 2.0 / MLA_ROPE)
    ang = pos.astype(F32)[:, None] * inv[None, :]
    cos2 = jnp.concatenate([jnp.cos(ang)] * 2, axis=1)
    sin2 = jnp.concatenate([jnp.sin(ang)] * 2, axis=1)
    n = pos.shape[0]
    scale = (MLA_NOPE + MLA_ROPE) ** -0.5 * LOG2E
    pad = jnp.zeros((n, MLA_QK_W - 2 * MLA_NOPE - 2 * MLA_ROPE), F32)
    ctab = jnp.concatenate([jnp.full((n, 2 * MLA_NOPE), scale, F32), scale * cos2, scale * cos2, pad], axis=1)
    stab = jnp.concatenate([jnp.zeros((n, 2 * MLA_NOPE), F32), scale * sin2, scale * sin2, pad], axis=1)
    return jnp.tile(ctab, (1, N_PAIRS)), jnp.tile(stab, (1, N_PAIRS)), cos2, sin2


def _pad_time(a, tp, axis):
    pad = [(0, 0)] * a.ndim
    pad[axis] = (0, tp - a.shape[axis])
    return jnp.pad(a, pad)


def _layer(x, past, p, proj_w, kvx_w, mix_w, ffn_w, is_moe, alpha):
    b, s, d = x.shape
    n_past = 0 if past is None else past[0].shape[1]
    t = n_past + s
    tabs = _rope_tables(n_past + jnp.arange(s))
    if past is None:
        tp = t
        fq, fk, fv, fkb, fvt, logf, logft, ckv, kr, qp, k_mla, vt_mla = _proj(x, proj_w, tabs, kvx_w)
        k_fox, vt_fox, logft_all = fkb, fvt, logft
    else:
        fq, fk, fv, fkb, fvt, logf, logft, ckv, kr, qp = _proj(x, proj_w, tabs)
        tp = -(-t // LANES) * LANES
        pk, pv, plogf, pckv, pkr = past
        k_fox = _pad_time(jnp.concatenate([pk.reshape(b, n_past, FOX_W).astype(BF16), fkb], axis=1), tp, 1)
        pvt = jnp.swapaxes(pv.reshape(b, n_past, FOX_W), 1, 2).astype(BF16)
        vt_fox = _pad_time(jnp.concatenate([pvt, fvt], axis=2), tp, 2)
        logft_all = _pad_time(jnp.concatenate([jnp.swapaxes(plogf, 1, 2), logft], axis=2), tp, 2)
        ckv_all = _pad_time(jnp.concatenate([pckv, ckv], axis=1), tp, 1)
        kr_all = _pad_time(jnp.concatenate([pkr, kr], axis=1), tp, 1)
        k_mla, vt_mla = _kv_expand(ckv_all, kr_all, *kvx_w)
    nb_aug = _neg_cumsum(logft_all)
    sq = -(-s // LANES) * LANES
    fq_p, qp_p = _pad_time(fq, sq, 1), _pad_time(qp, sq, 1)
    o_fox = _attention_t(fq_p, k_fox, vt_fox, nb_aug, q_off=n_past, t_valid=t, chunked=False)[:, :s]
    o_mla = _attention_t(qp_p, k_mla, vt_mla, None, q_off=n_past, t_valid=t, chunked=True)[:, :s]
    r = b * s
    h = _mix(x.reshape(r, d), o_fox.reshape(r, FOX_W), o_mla.reshape(r, MLA_W), mix_w, alpha)
    ffn = _moe_ffn if is_moe else _dense_ffn
    y = ffn(h, p.reshape(r, -1), ffn_w, alpha).reshape(b, s, d)
    new_rows = (fk.reshape(b, s, FOX_HEADS, FOX_HEAD_DIM), fv.reshape(b, s, FOX_HEADS, FOX_HEAD_DIM), logf, ckv, kr)
    return y, new_rows


def kernel(x_prompt, x_sample, cache_fox_k, cache_fox_v, cache_fox_logf, cache_mla_ckv, cache_mla_krope,
           p_prompt, p_sample, w_in, b_fox_f, g_mla_cq, w_mla_qb, g_mla_ckv, w_mla_kvb, w_o_fox, w_o_mla,
           w_out, ln_mix_g, ln_mix_b, w_ffn_gate, w_ffn_up, w_ffn_down, w_router, b_router, w_moe_gate,
           w_moe_up, w_moe_down, w_ple_proj, w_ple_gate, ln_ffn_g, ln_ffn_b):
    depth = w_in.shape[0]
    alpha = (2 * depth) ** 0.25
    hp, hs = x_prompt, x_sample
    rows_p, rows_s = [], []
    for i in range(depth):
        proj_w, kvx_w, (wga, wgb) = _mixer_weights(w_in[i], b_fox_f[i], g_mla_cq[i], w_mla_qb[i], g_mla_ckv[i],
                                                   w_mla_kvb[i])
        mix_w = (wga, wgb, w_o_fox[i].astype(BF16), w_o_mla[i].astype(BF16), w_out[i].astype(BF16),
                 ln_mix_g[i].reshape(1, -1), ln_mix_b[i].reshape(1, -1))
        tail = (w_ple_gate[i].astype(BF16), w_ple_proj[i].astype(BF16),
                ln_ffn_g[i].reshape(1, -1), ln_ffn_b[i].reshape(1, -1))
        j = i // 2
        is_moe = i % 2 == 1
        if is_moe:
            ffn_w = (_pad_lanes(w_router[j]).astype(BF16), _pad_lanes(b_router[j].reshape(1, -1)),
                     w_moe_gate[j].astype(BF16), w_moe_up[j].astype(BF16), w_moe_down[j].astype(BF16)) + tail
        else:
            ffn_w = (w_ffn_gate[j].astype(BF16), w_ffn_up[j].astype(BF16), w_ffn_down[j].astype(BF16)) + tail
        past = (cache_fox_k[i], cache_fox_v[i], cache_fox_logf[i], cache_mla_ckv[i], cache_mla_krope[i])
        hp, new_p = _layer(hp, None, p_prompt[i], proj_w, kvx_w, mix_w, ffn_w, is_moe, alpha)
        hs, new_s = _layer(hs, past, p_sample[i], proj_w, kvx_w, mix_w, ffn_w, is_moe, alpha)
        rows_p.append(new_p)
        rows_s.append(new_s)

    def stack(rows, idx):
        return jnp.stack([r[idx] for r in rows], axis=0)

    return (hp, hs) + tuple(stack(rows_p, k) for k in range(5)) + tuple(stack(rows_s, k) for k in range(5))
```

```python
import functools

import jax
import jax.numpy as jnp
from jax import lax
from jax.experimental import pallas as pl
from jax.experimental.pallas import tpu as pltpu

CHUNK = 64
FOX_HEADS = 8
FOX_HEAD_DIM = 64
FOX_W = FOX_HEADS * FOX_HEAD_DIM
MLA_HEADS = 8
MLA_Q_RANK = 256
MLA_KV_RANK = 128
MLA_NOPE = 64
MLA_ROPE = 32
MLA_V = 64
MLA_W = MLA_HEADS * MLA_V
ROPE_THETA = 10000.0
N_EXPERTS = 8
LN_EPS = 1e-5
RMS_EPS = 1e-6
NEG_INF = -1e30
LOG2E = 1.4426950408889634

LANES = 128
SUBLANES = 8
PAIR_W = 2 * MLA_V
MLA_QK_W = 256
N_PAIRS = FOX_HEADS // 2
V7X_SCOPED_VMEM_BYTES = 60000 * 1024

BF16 = jnp.bfloat16
F32 = jnp.float32


def _dot(a, b):
    return jnp.dot(a, b, preferred_element_type=F32)


def _dot_nt(a, b):
    return lax.dot_general(a, b, (((1,), (1,)), ((), ())), preferred_element_type=F32)


def _sigmoid(x):
    return 1.0 / (1.0 + jnp.exp(-x))


def _log_sigmoid(x):
    return jnp.minimum(x, 0.0) - jnp.log1p(jnp.exp(-jnp.abs(x)))


def _rms_norm(x, g):
    return x * lax.rsqrt(jnp.mean(jnp.square(x), axis=-1, keepdims=True) + RMS_EPS) * g


def _layer_norm(x, g, b):
    mu = jnp.mean(x, axis=-1, keepdims=True)
    xc = x - mu
    var = jnp.mean(jnp.square(xc), axis=-1, keepdims=True)
    return xc * lax.rsqrt(var + LN_EPS) * g + b


def _resident(shape):
    nd = len(shape)
    return pl.BlockSpec(shape, lambda *_: (0,) * nd, pipeline_mode=pl.Buffered(1))


def _params(semantics, vmem_bytes):
    return pltpu.CompilerParams(dimension_semantics=semantics,
                                vmem_limit_bytes=min(int(vmem_bytes), V7X_SCOPED_VMEM_BYTES))


def _pick_tile(n, candidates):
    for c in candidates:
        if n % c == 0:
            return c
    return n


def _proj_kernel(x_ref, w1_ref, w2_ref, wfft_ref, wfvt_ref, bf_ref, bft_ref, gcq_ref, gckv_ref, wqa_ref, wqb_ref,
                 ctab_ref, stab_ref, cos_ref, sin_ref, *rest, expand_kv):
    if expand_kv:
        wk_ref, place_ref, wvt_ref = rest[:3]
        rest = rest[3:]
    fq_ref, fk_ref, fv_ref, fkb_ref, fvt_ref, logf_ref, logft_ref, ckv_ref, kr_ref, qp_ref = rest[:10]
    xb = x_ref[0].astype(BF16)
    z1 = _dot(xb, w1_ref[...])
    fq_ref[0] = z1[:, :FOX_W].astype(BF16)
    fk = z1[:, FOX_W:2 * FOX_W]
    fk_ref[0] = fk
    fkb_ref[0] = fk.astype(BF16)
    fv_ref[0] = z1[:, 2 * FOX_W:]
    fvt_ref[0] = _dot_nt(wfvt_ref[...], xb).astype(BF16)

    z2 = _dot(xb, w2_ref[...])
    cq = z2[:, :MLA_Q_RANK]
    o = MLA_Q_RANK
    ckv = z2[:, o:o + MLA_KV_RANK]
    o += MLA_KV_RANK
    kr = z2[:, o:o + MLA_ROPE]
    krr = z2[:, o + LANES:o + LANES + MLA_ROPE]
    ff = z2[:, o + 2 * LANES:o + 2 * LANES + FOX_HEADS]
    logf_ref[0] = _log_sigmoid(ff + bf_ref[...])
    fft = _dot_nt(wfft_ref[...], xb)
    logft_ref[0] = _log_sigmoid(fft[:FOX_HEADS] + bft_ref[...])
    ckv_n = _rms_norm(ckv, gckv_ref[...])
    ckv_ref[0] = ckv_n
    kr_rot = kr * cos_ref[...] + krr * sin_ref[...]
    kr_ref[0] = kr_rot
    cqn = _rms_norm(cq, gcq_ref[...]).astype(BF16)
    qp = _dot(cqn, wqa_ref[...]) * ctab_ref[...] + _dot(cqn, wqb_ref[...]) * stab_ref[...]
    qp_ref[0] = qp.astype(BF16)
    if expand_kv:
        kp_ref, vmt_ref = rest[10:]
        _expand_kv(ckv_n, kr_rot, wk_ref, place_ref, wvt_ref, kp_ref, vmt_ref)


def _expand_kv(ckv, kr, wk_ref, place_ref, wvt_ref, kp_ref, vmt_ref):
    cb = ckv.astype(BF16)
    kp_ref[0] = (_dot(cb, wk_ref[...]) + _dot(kr.astype(BF16), place_ref[...])).astype(BF16)
    vmt_ref[0] = _dot_nt(wvt_ref[...], cb).astype(BF16)


def _proj(x, w, tabs, kvx_w=None):
    b, s, d = x.shape
    tm = _pick_tile(s, (512, 256, 128))
    ns = s // tm
    w1, w2, wfft, wfvt, bf, bft, gcq, gckv, wqa, wqb = w
    ctab, stab, cos2, sin2 = tabs
    qw = N_PAIRS * MLA_QK_W

    def tok(width):
        return pl.BlockSpec((1, tm, width), lambda si, bi: (bi, si, 0))

    def tok_t(height):
        return pl.BlockSpec((1, height, tm), lambda si, bi: (bi, 0, si))

    def tab(width):
        return pl.BlockSpec((tm, width), lambda si, bi: (si, 0))

    in_specs = [tok(d), _resident(w1.shape), _resident(w2.shape), _resident(wfft.shape), _resident(wfvt.shape),
                _resident(bf.shape), _resident(bft.shape), _resident(gcq.shape), _resident(gckv.shape),
                _resident(wqa.shape), _resident(wqb.shape), tab(qw), tab(qw), tab(MLA_ROPE), tab(MLA_ROPE)]
    out_shape = (
        jax.ShapeDtypeStruct((b, s, FOX_W), BF16),
        jax.ShapeDtypeStruct((b, s, FOX_W), F32),
        jax.ShapeDtypeStruct((b, s, FOX_W), F32),
        jax.ShapeDtypeStruct((b, s, FOX_W), BF16),
        jax.ShapeDtypeStruct((b, FOX_W, s), BF16),
        jax.ShapeDtypeStruct((b, s, FOX_HEADS), F32),
        jax.ShapeDtypeStruct((b, FOX_HEADS, s), F32),
        jax.ShapeDtypeStruct((b, s, MLA_KV_RANK), F32),
        jax.ShapeDtypeStruct((b, s, MLA_ROPE), F32),
        jax.ShapeDtypeStruct((b, s, qw), BF16),
    )
    out_specs = (tok(FOX_W), tok(FOX_W), tok(FOX_W), tok(FOX_W), tok_t(FOX_W), tok(FOX_HEADS),
                 tok_t(FOX_HEADS), tok(MLA_KV_RANK), tok(MLA_ROPE), tok(qw))
    args = (x, w1, w2, wfft, wfvt, bf, bft, gcq, gckv, wqa, wqb, ctab, stab, cos2, sin2)
    if kvx_w is not None:
        in_specs += [_resident(a.shape) for a in kvx_w]
        args += tuple(kvx_w)
        out_shape += (jax.ShapeDtypeStruct((b, s, qw), BF16), jax.ShapeDtypeStruct((b, MLA_W, s), BF16))
        out_specs += (tok(qw), tok_t(MLA_W))
    return pl.pallas_call(
        functools.partial(_proj_kernel, expand_kv=kvx_w is not None), grid=(ns, b), in_specs=in_specs,
        out_specs=out_specs, out_shape=out_shape,
        compiler_params=_params(("arbitrary", "arbitrary"), 56 << 20), name="proj",
    )(*args)


def _cumsum_kernel(x_ref, aug_ref, *, ch):
    t = x_ref.shape[2]
    lane = lax.broadcasted_iota(jnp.int32, (1, LANES), 1)
    r = lax.broadcasted_iota(jnp.int32, (ch, ch), 0)
    c = lax.broadcasted_iota(jnp.int32, (ch, ch), 1)
    upper = jnp.where(r <= c, 1.0, 0.0).astype(BF16)
    carry = jnp.zeros((FOX_HEADS, 1), F32)
    zeros = jnp.zeros((FOX_HEADS, ch), F32)
    for ci in range(t // ch):
        xc = x_ref[0, :, ci * ch:(ci + 1) * ch]
        hi = xc.astype(BF16).astype(F32)
        r1 = xc - hi
        mid = r1.astype(BF16).astype(F32)
        lo = (r1 - mid).astype(BF16).astype(F32)
        pieces = jnp.concatenate([hi, mid, lo, zeros], axis=0).astype(BF16)
        pc = _dot(pieces, upper)
        cum = pc[0:8] + pc[8:16] + pc[16:24] + carry
        nb = cum * (-LOG2E)
        for hh in range(FOX_HEADS):
            rep = jnp.broadcast_to(nb[hh:hh + 1, :], (LANES, ch)).T
            a_hi = rep.astype(BF16).astype(F32)
            a_r = rep - a_hi
            a_mid = a_r.astype(BF16).astype(F32)
            a_lo = (a_r - a_mid).astype(BF16).astype(F32)
            aug = jnp.where(lane == 0, a_hi, jnp.where(lane == 1, a_mid, jnp.where(lane == 2, a_lo, 0.0)))
            aug_ref[0, hh, ci * ch:(ci + 1) * ch, :] = aug.astype(BF16)
        carry = cum[:, ch - 1:ch]


def _neg_cumsum(logft):
    b, h, t = logft.shape
    ch = 256 if t % 256 == 0 else LANES
    spec = pl.BlockSpec((1, h, t), lambda bi: (bi, 0, 0))
    return pl.pallas_call(
        functools.partial(_cumsum_kernel, ch=ch), grid=(b,), in_specs=[spec],
        out_specs=pl.BlockSpec((1, h, t, LANES), lambda bi: (bi, 0, 0, 0)),
        out_shape=jax.ShapeDtypeStruct((b, h, t, LANES), BF16),
        compiler_params=_params(("arbitrary",), 32 << 20), name="cumsum",
    )(logft)


def _kvx_kernel(ckv_ref, kr_ref, wk_ref, place_ref, wvt_ref, kp_ref, vmt_ref):
    _expand_kv(ckv_ref[0], kr_ref[0], wk_ref, place_ref, wvt_ref, kp_ref, vmt_ref)


def _kv_expand(ckv, kr, wk, place, wvt):
    b, t, _ = ckv.shape
    tm = _pick_tile(t, (512, 384, 256, 128))
    kw = N_PAIRS * MLA_QK_W

    def tok(width):
        return pl.BlockSpec((1, tm, width), lambda bi, ti: (bi, ti, 0))

    return pl.pallas_call(
        _kvx_kernel, grid=(b, t // tm),
        in_specs=[tok(MLA_KV_RANK), tok(MLA_ROPE), _resident(wk.shape), _resident(place.shape), _resident(wvt.shape)],
        out_specs=(tok(kw), pl.BlockSpec((1, MLA_W, tm), lambda bi, ti: (bi, 0, ti))),
        out_shape=(jax.ShapeDtypeStruct((b, t, kw), BF16), jax.ShapeDtypeStruct((b, MLA_W, t), BF16)),
        compiler_params=_params(("arbitrary", "arbitrary"), 32 << 20), name="kvexpand",
    )(ckv, kr, wk, place, wvt)


def _head_lane_mask(lane, hh, width):
    if width == LANES:
        return (lane // FOX_HEAD_DIM) == hh
    nope = (lane < 2 * MLA_NOPE) & ((lane // MLA_NOPE) == hh)
    rope = (lane >= 2 * MLA_NOPE) & (lane < 2 * MLA_NOPE + 2 * MLA_ROPE) & (
        ((lane - 2 * MLA_NOPE) // MLA_ROPE) == hh)
    return nope | rope


def _attn_t_kernel(*refs, tq, tk, width, q_off, t_valid, chunked, has_bias):
    if has_bias:
        q_ref, k_ref, vt_ref, nb_ref, o_ref, sa_ref, sb_ref = refs
    else:
        q_ref, k_ref, vt_ref, o_ref, sa_ref, sb_ref = refs
        nb_ref = None
    nq = q_ref.shape[1] // tq
    nkb = k_ref.shape[1] // tk

    items = []
    for qi in range(nq):
        q_min = q_off + qi * tq
        q_max = q_min + tq - 1
        if chunked:
            lim_min, lim_max = (q_min // CHUNK + 1) * CHUNK, (q_max // CHUNK + 1) * CHUNK
        else:
            lim_min, lim_max = q_min + 1, q_max + 1
        n_full = min(lim_min, t_valid) // tk
        n_vis = min(-(-min(lim_max, t_valid) // tk), nkb)
        items += [(qi, kb, kb >= n_full, kb == n_vis - 1) for kb in range(n_vis)]

    lane = lax.broadcasted_iota(jnp.int32, (1, width), 1)
    keep = [_head_lane_mask(lane, hh, width) for hh in range(2)]
    bufs = (sa_ref, sb_ref)

    ones3 = jnp.where(lax.broadcasted_iota(jnp.int32, (tq, LANES), 1) < 3, 1.0, 0.0).astype(BF16)
    ones_rows = jnp.ones((16, tk), BF16)

    def scores_into(s_ref, qi, kb):
        q = q_ref[0, qi * tq:(qi + 1) * tq, :]
        kblk = k_ref[0, kb * tk:(kb + 1) * tk, :]
        for hh in range(2):
            qh = jnp.where(keep[hh], q, jnp.zeros_like(q))
            if has_bias:
                kh = jnp.concatenate([kblk, nb_ref[0, hh, kb * tk:(kb + 1) * tk, :]], axis=1)
                s_ref[hh] = _dot_nt(kh, jnp.concatenate([qh, ones3], axis=1))
            else:
                s_ref[hh] = _dot_nt(kblk, qh)

    def consume(s_ref, qi, kb, stats, masked):
        if masked:
            qpos = q_off + qi * tq + lax.broadcasted_iota(jnp.int32, (1, tq), 1)
            if chunked:
                col_lim = lax.shift_left(lax.shift_right_logical(qpos, 6) + 1, 6)
            else:
                col_lim = qpos + 1
            col_lim = jnp.minimum(col_lim, t_valid)
            kpos = kb * tk + lax.broadcasted_iota(jnp.int32, (tk, 1), 0)
        out = []
        for hh in range(2):
            m, acc = stats[2 * hh:2 * hh + 2]
            s = s_ref[hh]
            if masked:
                s = jnp.where(kpos < col_lim, s, NEG_INF)
            m_new = jnp.maximum(m, jnp.max(s, axis=0, keepdims=True))
            p = jnp.exp2(s - m_new)
            alpha = jnp.exp2(m - m_new)
            vth = jnp.concatenate([vt_ref[0, hh * MLA_V:(hh + 1) * MLA_V, kb * tk:(kb + 1) * tk], ones_rows], axis=0)
            acc = alpha * acc + _dot(vth, p.astype(BF16))
            out += [m_new, acc]
        return tuple(out)

    scores_into(bufs[0], items[0][0], items[0][1])
    stats = None
    for i, (qi, kb, masked, last) in enumerate(items):
        if i + 1 < len(items):
            scores_into(bufs[(i + 1) % 2], items[i + 1][0], items[i + 1][1])
        if kb == 0:
            stats = (jnp.full((1, tq), NEG_INF, F32), jnp.zeros((MLA_V + 16, tq), F32)) * 2
        stats = consume(bufs[i % 2], qi, kb, stats, masked)
        if last:
            outs = [stats[2 * hh + 1][:MLA_V] * (1.0 / stats[2 * hh + 1][MLA_V:MLA_V + 1]) for hh in range(2)]
            o_ref[0, qi * tq:(qi + 1) * tq, :] = jnp.concatenate(outs, axis=0).T.astype(BF16)


def _attention_t(q, k, vt, nbrep, *, q_off, t_valid, chunked):
    assert CHUNK == 64
    b, sq, qw = q.shape
    width = qw // N_PAIRS
    tp = k.shape[1]
    tq = _pick_tile(sq, (512, 256, 128))
    tk = _pick_tile(tp, (512, 384, 256, 128))
    in_specs = [pl.BlockSpec((1, sq, width), lambda bi, pi: (bi, 0, pi)),
                pl.BlockSpec((1, tp, width), lambda bi, pi: (bi, 0, pi)),
                pl.BlockSpec((1, PAIR_W, tp), lambda bi, pi: (bi, pi, 0))]
    args = [q, k, vt]
    if nbrep is not None:
        in_specs.append(pl.BlockSpec((1, 2, tp, LANES), lambda bi, pi: (bi, pi, 0, 0)))
        args.append(nbrep)
    kern = functools.partial(_attn_t_kernel, tq=tq, tk=tk, width=width, q_off=q_off, t_valid=t_valid,
                             chunked=chunked, has_bias=nbrep is not None)
    return pl.pallas_call(
        kern, grid=(b, N_PAIRS), in_specs=in_specs,
        out_specs=pl.BlockSpec((1, sq, PAIR_W), lambda bi, pi: (bi, 0, pi)),
        out_shape=jax.ShapeDtypeStruct((b, sq, N_PAIRS * PAIR_W), BF16),
        scratch_shapes=[pltpu.VMEM((2, tk, tq), F32), pltpu.VMEM((2, tk, tq), F32)],
        compiler_params=_params(("arbitrary", "arbitrary"), 40 << 20),
        name="attn_t_mla" if chunked else "attn_t_fox",
    )(*args)


def _mix_kernel(x_ref, of_ref, om_ref, wga_ref, wgb_ref, wof_ref, wom_ref, wout_ref, g_ref, b_ref, h_ref, *, alpha):
    x = x_ref[...]
    xb = x.astype(BF16)
    a = _sigmoid(_dot(xb, wga_ref[...])) * _dot(of_ref[...], wof_ref[...])
    bm = _sigmoid(_dot(xb, wgb_ref[...])) * _dot(om_ref[...], wom_ref[...])
    m = _dot((a + bm).astype(BF16), wout_ref[...])
    h_ref[...] = _layer_norm(alpha * x + m, g_ref[...], b_ref[...])


def _mix(x, of, om, w, alpha):
    r, d = x.shape
    tm = _pick_tile(r, (512, 256, 128))

    def row(width):
        return pl.BlockSpec((tm, width), lambda i: (i, 0))

    return pl.pallas_call(
        functools.partial(_mix_kernel, alpha=alpha), grid=(r // tm,),
        in_specs=[row(d), row(FOX_W), row(MLA_W)] + [_resident(a.shape) for a in w],
        out_specs=row(d), out_shape=jax.ShapeDtypeStruct((r, d), F32),
        compiler_params=_params(("arbitrary",), 48 << 20), name="mix",
    )(x, of, om, *w)


def _ple(hb, p_ref, wpg_ref, wpp_ref):
    return _sigmoid(_dot(hb, wpg_ref[...])) * _dot(p_ref[...].astype(BF16), wpp_ref[...])


def _ffn_chunk(n_ff):
    return _pick_tile(n_ff, (1408, 1024, 512, 256, 128))


def _dense_ffn_kernel(h_ref, p_ref, wg_ref, wu_ref, wd_ref, wpg_ref, wpp_ref, g_ref, b_ref, y_ref, *, alpha, fc):
    h = h_ref[...]
    hb = h.astype(BF16)
    n_ff = wg_ref.shape[1]
    acc = jnp.zeros(h.shape, F32)
    for c in range(n_ff // fc):
        sl = slice(c * fc, (c + 1) * fc)
        gt = _dot(hb, wg_ref[:, sl])
        act = (gt * _sigmoid(gt) * _dot(hb, wu_ref[:, sl])).astype(BF16)
        acc = acc + _dot(act, wd_ref[sl, :])
    y = alpha * h + acc + _ple(hb, p_ref, wpg_ref, wpp_ref)
    y_ref[...] = _layer_norm(y, g_ref[...], b_ref[...])


def _dense_ffn(h, p, w, alpha):
    r, d = h.shape
    tm = _pick_tile(r, (512, 256, 128))
    wg, wu, wd, wpg, wpp, g, bb = w

    def row(width):
        return pl.BlockSpec((tm, width), lambda i: (i, 0))

    return pl.pallas_call(
        functools.partial(_dense_ffn_kernel, alpha=alpha, fc=_ffn_chunk(wg.shape[1])), grid=(r // tm,),
        in_specs=[row(d), row(p.shape[1])] + [_resident(a.shape) for a in w],
        out_specs=row(d), out_shape=jax.ShapeDtypeStruct((r, d), F32),
        compiler_params=_params(("arbitrary",), 56 << 20), name="ffn_dense",
    )(h, p, wg, wu, wd, wpg, wpp, g, bb)


MOE_ROW_TILE = 512
ROUTE_LANES = 6


def _route_kernel(h_ref, wr_ref, br_ref, route_ref, counts_ref, carry_ref):
    @pl.when(pl.program_id(0) == 0)
    def _init():
        carry_ref[...] = jnp.zeros(carry_ref.shape, F32)

    hb = h_ref[...].astype(BF16)
    tm, d = hb.shape
    lane = lax.broadcasted_iota(jnp.int32, (1, LANES), 1)
    logits = _dot(hb, wr_ref[...]) + br_ref[...]
    lg = jnp.where(lane < N_EXPERTS, logits, -jnp.inf)
    m1 = jnp.max(lg, axis=-1, keepdims=True)
    i1 = jnp.min(jnp.where(lg == m1, lane, LANES), axis=-1, keepdims=True)
    lg2 = jnp.where(lane == i1, -jnp.inf, lg)
    m2 = jnp.max(lg2, axis=-1, keepdims=True)
    i2 = jnp.min(jnp.where(lg2 == m2, lane, LANES), axis=-1, keepdims=True)
    e2 = jnp.exp(m2 - m1)
    den = 1.0 + e2
    hit1 = lane == i1
    hit2 = lane == i2
    onehot = jnp.where(hit1, 1.0, 0.0) + jnp.where(hit2, 1.0, 0.0)
    rr = lax.broadcasted_iota(jnp.int32, (tm, tm), 0)
    cc = lax.broadcasted_iota(jnp.int32, (tm, tm), 1)
    lower = jnp.where(cc < rr, 1.0, 0.0).astype(BF16)
    prefix = _dot(lower, onehot.astype(BF16)) + carry_ref[...]
    rank1 = jnp.sum(jnp.where(hit1, prefix, 0.0), axis=-1, keepdims=True)
    rank2 = jnp.sum(jnp.where(hit2, prefix, 0.0), axis=-1, keepdims=True)
    fields = (i1.astype(F32), i2.astype(F32), rank1, rank2, 1.0 / den, e2 / den)
    route = jnp.zeros((tm, LANES), F32)
    for j, f in enumerate(fields):
        route = jnp.where(lane == j, f, route)
    route_ref[...] = route
    carry = carry_ref[...] + jnp.sum(onehot, axis=0, keepdims=True)
    carry_ref[...] = carry
    counts_ref[...] = carry


def _route(h, wr, br):
    r, d = h.shape
    tm = _pick_tile(r, (512, 256, 128))
    return pl.pallas_call(
        _route_kernel, grid=(r // tm,),
        in_specs=[pl.BlockSpec((tm, d), lambda i: (i, 0)), _resident(wr.shape), _resident(br.shape)],
        out_specs=(pl.BlockSpec((tm, LANES), lambda i: (i, 0)), pl.BlockSpec((1, LANES), lambda i: (0, 0))),
        out_shape=(jax.ShapeDtypeStruct((r, LANES), F32), jax.ShapeDtypeStruct((1, LANES), F32)),
        scratch_shapes=[pltpu.VMEM((1, LANES), F32)],
        compiler_params=_params(("arbitrary",), 32 << 20), name="moe_route",
    )(h, wr, br)


ROW_DMA_UNROLL = 8


def _row_scatter_kernel(idx_ref, src_ref, init_ref, dst_ref, stage_ref, sem, *, tm):
    del init_ref
    for j in range(SUBLANES):
        stage_ref[pl.ds(j, tm, stride=SUBLANES), :] = src_ref[:, j * LANES:(j + 1) * LANES]

    def copy(t, d_row):
        src = stage_ref.at[pl.ds(pl.multiple_of(t * SUBLANES, SUBLANES), SUBLANES)]
        return pltpu.make_async_copy(src, dst_ref.at[pl.ds(pl.multiple_of(d_row * SUBLANES, SUBLANES), SUBLANES)], sem)

    def issue(t, c):
        copy(t, idx_ref[0, 0, 2 * t]).start(priority=0)
        copy(t, idx_ref[0, 0, 2 * t + 1]).start(priority=1)
        return c

    lax.fori_loop(0, tm, issue, 0, unroll=ROW_DMA_UNROLL)
    for _ in range(2):
        pltpu.make_async_copy(stage_ref, dst_ref.at[pl.ds(0, tm * SUBLANES)], sem).wait()


def _row_scatter(idx3, src, init):
    steps, _, n = idx3.shape
    tm = n // 2
    assert src.shape[1] == SUBLANES * LANES
    any_spec = pl.BlockSpec(memory_space=pl.ANY)
    return pl.pallas_call(
        functools.partial(_row_scatter_kernel, tm=tm), grid=(steps,),
        in_specs=[pl.BlockSpec((1, 1, n), lambda i: (i, 0, 0), memory_space=pltpu.SMEM),
                  pl.BlockSpec((tm, src.shape[1]), lambda i: (i, 0)), any_spec],
        out_specs=any_spec, out_shape=jax.ShapeDtypeStruct(init.shape, init.dtype),
        scratch_shapes=[pltpu.VMEM((tm * SUBLANES, LANES), F32), pltpu.SemaphoreType.DMA(())],
        input_output_aliases={2: 0},
        compiler_params=_params(("arbitrary",), 16 << 20), name="moe_scatter",
    )(idx3, src, init)


def _grouped_ffn_kernel(te_ref, tv_ref, x_ref, wg_ref, wu_ref, wd_ref, y_ref, *, fc):
    del te_ref
    valid = tv_ref[pl.program_id(0)] != 0

    @pl.when(valid)
    def _compute():
        tmx = x_ref.shape[0] // SUBLANES
        xb = jnp.concatenate([x_ref[pl.ds(j, tmx, stride=SUBLANES), :] for j in range(SUBLANES)],
                             axis=1).astype(BF16)
        n_ff = wg_ref.shape[2]
        acc = jnp.zeros((tmx, wd_ref.shape[2]), F32)
        for c in range(n_ff // fc):
            sl = slice(c * fc, (c + 1) * fc)
            gt = _dot(xb, wg_ref[0, :, sl])
            act = (gt * _sigmoid(gt) * _dot(xb, wu_ref[0, :, sl])).astype(BF16)
            acc = acc + _dot(act, wd_ref[0, sl, :])
        for j in range(SUBLANES):
            y_ref[pl.ds(j, tmx, stride=SUBLANES), :] = acc[:, j * LANES:(j + 1) * LANES]

    @pl.when(jnp.logical_not(valid))
    def _skip():
        y_ref[...] = jnp.zeros(y_ref.shape, F32)


def _grouped_ffn(tile_expert, tile_valid, xs, wg, wu, wd):
    n_e, d, n_ff = wg.shape
    tmx = MOE_ROW_TILE
    rp = xs.shape[0] // SUBLANES
    grid_spec = pltpu.PrefetchScalarGridSpec(
        num_scalar_prefetch=2, grid=(rp // tmx,),
        in_specs=[pl.BlockSpec((tmx * SUBLANES, LANES), lambda i, te, tv: (i, 0)),
                  pl.BlockSpec((1, d, n_ff), lambda i, te, tv: (te[i], 0, 0)),
                  pl.BlockSpec((1, d, n_ff), lambda i, te, tv: (te[i], 0, 0)),
                  pl.BlockSpec((1, n_ff, d), lambda i, te, tv: (te[i], 0, 0))],
        out_specs=pl.BlockSpec((tmx * SUBLANES, LANES), lambda i, te, tv: (i, 0)))
    assert d == SUBLANES * LANES
    return pl.pallas_call(
        functools.partial(_grouped_ffn_kernel, fc=_ffn_chunk(n_ff)), grid_spec=grid_spec,
        out_shape=jax.ShapeDtypeStruct((rp * SUBLANES, LANES), F32),
        compiler_params=_params(("arbitrary",), V7X_SCOPED_VMEM_BYTES), name="moe_ffn",
    )(tile_expert, tile_valid, xs, wg, wu, wd)


def _moe_out_kernel(idx_ref, idx_next_ref, h_ref, p_ref, route_ref, ys_ref, wpg_ref, wpp_ref, g_ref, b_ref, y_ref,
                    buf_ref, sem, *, alpha):
    tm = h_ref.shape[0]
    step = pl.program_id(0)
    slot = lax.rem(step, 2)

    def copy(sl, k, t, s_row):
        src = ys_ref.at[pl.ds(pl.multiple_of(s_row * SUBLANES, SUBLANES), SUBLANES)]
        dst = buf_ref.at[sl, k, pl.ds(pl.multiple_of(t * SUBLANES, SUBLANES), SUBLANES)]
        return pltpu.make_async_copy(src, dst, sem.at[sl])

    def issue_block(rows_ref, sl):
        def issue(t, c):
            copy(sl, 0, t, rows_ref[0, 0, 2 * t]).start(priority=0)
            copy(sl, 1, t, rows_ref[0, 0, 2 * t + 1]).start(priority=1)
            return c

        lax.fori_loop(0, tm, issue, 0, unroll=ROW_DMA_UNROLL)

    @pl.when(step == 0)
    def _first():
        issue_block(idx_ref, 0)

    @pl.when(step + 1 < pl.num_programs(0))
    def _ahead():
        issue_block(idx_next_ref, 1 - slot)

    h = h_ref[...]
    hb = h.astype(BF16)
    ple = _ple(hb, p_ref, wpg_ref, wpp_ref)

    for k in range(2):
        pltpu.make_async_copy(ys_ref.at[pl.ds(0, tm * SUBLANES)], buf_ref.at[slot, k], sem.at[slot]).wait()

    def rows(k):
        view = buf_ref.at[slot, k]
        return jnp.concatenate([view[pl.ds(j, tm, stride=SUBLANES), :] for j in range(SUBLANES)], axis=1)

    rt = route_ref[...]
    moe = rt[:, 4:5] * rows(0) + rt[:, 5:6] * rows(1)
    y_ref[...] = _layer_norm(alpha * h + moe + ple, g_ref[...], b_ref[...])


def _moe_out(idx3, h, p, route, ys, w, alpha):
    r, d = h.shape
    steps, _, n = idx3.shape
    tm = n // 2

    def row(width):
        return pl.BlockSpec((tm, width), lambda i: (i, 0))

    return pl.pallas_call(
        functools.partial(_moe_out_kernel, alpha=alpha), grid=(steps,),
        in_specs=[pl.BlockSpec((1, 1, n), lambda i: (i, 0, 0), memory_space=pltpu.SMEM),
                  pl.BlockSpec((1, 1, n), lambda i: (jnp.minimum(i + 1, steps - 1), 0, 0), memory_space=pltpu.SMEM),
                  row(d), row(p.shape[1]), row(LANES), pl.BlockSpec(memory_space=pl.ANY)]
        + [_resident(a.shape) for a in w],
        out_specs=row(d), out_shape=jax.ShapeDtypeStruct((r, d), F32),
        scratch_shapes=[pltpu.VMEM((2, 2, tm * SUBLANES, LANES), F32), pltpu.SemaphoreType.DMA((2,))],
        compiler_params=_params(("arbitrary",), 48 << 20), name="moe_out",
    )(idx3, idx3, h, p, route, ys, *w)


def _moe_ffn(h, p, w, alpha):
    r, d = h.shape
    wr, br, wg, wu, wd, wpg, wpp, g, bb = w
    tmx = MOE_ROW_TILE
    route, counts = _route(h, wr, br)
    cnt = counts[0, :N_EXPERTS].astype(jnp.int32)
    padded = (cnt + tmx - 1) // tmx * tmx
    ends = jnp.cumsum(padded)
    dest = jnp.take(ends - padded, route[:, 0:2].astype(jnp.int32)) + route[:, 2:4].astype(jnp.int32)
    n_tiles = -(-2 * r // tmx) + N_EXPERTS
    starts = jnp.arange(n_tiles, dtype=jnp.int32) * tmx
    tile_expert = jnp.minimum(jnp.sum(ends[None, :] <= starts[:, None], axis=1), N_EXPERTS - 1).astype(jnp.int32)
    tile_valid = (starts < ends[-1]).astype(jnp.int32)
    tm = _pick_tile(r, (512, 256, 128))
    dest3 = dest.reshape(r // tm, 1, 2 * tm)
    xs = _row_scatter(dest3, h, jnp.zeros((n_tiles * tmx * SUBLANES, LANES), F32))
    ys = _grouped_ffn(tile_expert, tile_valid, xs, wg, wu, wd)
    return _moe_out(dest3, h, p, route, ys, (wpg, wpp, g, bb), alpha)


def _pad_lanes(a, width=LANES):
    return jnp.pad(a, ((0, 0), (0, width - a.shape[1])))


def _rot_half_cols(a):
    half = a.shape[-1] // 2
    return jnp.concatenate([-a[..., half:], a[..., :half]], axis=-1)


def _mixer_weights(w_in, b_fox_f, g_cq, w_qb, g_ckv, w_kvb):
    d = w_in.shape[0]
    sizes = (FOX_W, FOX_W, FOX_W, FOX_HEADS, MLA_Q_RANK, MLA_KV_RANK, MLA_ROPE, d, d)
    cols, start = [], 0
    for n in sizes:
        cols.append(w_in[:, start:start + n])
        start += n
    wfq, wfk, wfv, wff, wcq, wckv, wkr, wga, wgb = cols
    w1 = jnp.concatenate([wfq * (FOX_HEAD_DIM ** -0.5 * LOG2E), wfk, wfv], axis=1).astype(BF16)
    w2 = jnp.concatenate([wcq, wckv, _pad_lanes(wkr), _pad_lanes(_rot_half_cols(wkr)), _pad_lanes(wff)],
                         axis=1).astype(BF16)
    wfft = jnp.pad(wff.T, ((0, 16 - FOX_HEADS), (0, 0))).astype(BF16)
    bf = b_fox_f.reshape(1, FOX_HEADS)
    bft = b_fox_f.reshape(FOX_HEADS, 1)

    qb = w_qb.reshape(MLA_Q_RANK, MLA_HEADS, MLA_NOPE + MLA_ROPE)
    q_nope, q_rope = qb[..., :MLA_NOPE], qb[..., MLA_NOPE:]
    q_rot = _rot_half_cols(q_rope)
    z_pad = jnp.zeros((MLA_Q_RANK, MLA_QK_W - 2 * MLA_NOPE - 2 * MLA_ROPE), F32)
    z_nope = jnp.zeros((MLA_Q_RANK, 2 * MLA_NOPE), F32)
    wqa = jnp.concatenate([jnp.concatenate([q_nope[:, 2 * j], q_nope[:, 2 * j + 1], q_rope[:, 2 * j],
                                            q_rope[:, 2 * j + 1], z_pad], axis=1) for j in range(N_PAIRS)], axis=1)
    wqb = jnp.concatenate([jnp.concatenate([z_nope, q_rot[:, 2 * j], q_rot[:, 2 * j + 1], z_pad], axis=1)
                           for j in range(N_PAIRS)], axis=1)

    kvb = w_kvb.reshape(MLA_KV_RANK, MLA_HEADS, MLA_NOPE + MLA_V)
    k_nope, v_up = kvb[..., :MLA_NOPE], kvb[..., MLA_NOPE:]
    zk = jnp.zeros((MLA_KV_RANK, MLA_QK_W - 2 * MLA_NOPE), F32)
    wk = jnp.concatenate([jnp.concatenate([k_nope[:, 2 * j], k_nope[:, 2 * j + 1], zk], axis=1)
                          for j in range(N_PAIRS)], axis=1).astype(BF16)
    eye = jnp.eye(MLA_ROPE, dtype=F32)
    place = jnp.concatenate([jnp.zeros((MLA_ROPE, 2 * MLA_NOPE), F32), eye, eye,
                             jnp.zeros((MLA_ROPE, MLA_QK_W - 2 * MLA_NOPE - 2 * MLA_ROPE), F32)], axis=1)
    place = jnp.tile(place, (1, N_PAIRS)).astype(BF16)
    wv = v_up.reshape(MLA_KV_RANK, MLA_W).astype(BF16)
    proj_w = (w1, w2, wfft, wfv.T.astype(BF16), bf, bft, g_cq.reshape(1, -1), g_ckv.reshape(1, -1),
              wqa.astype(BF16), wqb.astype(BF16))
    return proj_w, (wk, place, wv.T), (wga.astype(BF16), wgb.astype(BF16))


def _rope_tables(pos):
    half = MLA_ROPE // 2
    inv = ROPE_THETA ** (-jnp.arange(half, dtype=F32) * 2.0 / MLA_ROPE)
    ang = pos.astype(F32)[:, None] * inv[None, :]
    cos2 = jnp.concatenate([jnp.cos(ang)] * 2, axis=1)
    sin2 = jnp.concatenate([jnp.sin(ang)] * 2, axis=1)
    n = pos.shape[0]
    scale = (MLA_NOPE + MLA_ROPE) ** -0.5 * LOG2E
    pad = jnp.zeros((n, MLA_QK_W - 2 * MLA_NOPE - 2 * MLA_ROPE), F32)
    ctab = jnp.concatenate([jnp.full((n, 2 * MLA_NOPE), scale, F32), scale * cos2, scale * cos2, pad], axis=1)
    stab = jnp.concatenate([jnp.zeros((n, 2 * MLA_NOPE), F32), scale * sin2, scale * sin2, pad], axis=1)
    return jnp.tile(ctab, (1, N_PAIRS)), jnp.tile(stab, (1, N_PAIRS)), cos2, sin2


def _pad_time(a, tp, axis):
    pad = [(0, 0)] * a.ndim
    pad[axis] = (0, tp - a.shape[axis])
    return jnp.pad(a, pad)


def _layer(x, past, p, proj_w, kvx_w, mix_w, ffn_w, is_moe, alpha):
    b, s, d = x.shape
    n_past = 0 if past is None else past[0].shape[1]
    t = n_past + s
    tabs = _rope_tables(n_past + jnp.arange(s))
    if past is None:
        tp = t
        fq, fk, fv, fkb, fvt, logf, logft, ckv, kr, qp, k_mla, vt_mla = _proj(x, proj_w, tabs, kvx_w)
        k_fox, vt_fox, logft_all = fkb, fvt, logft
    else:
        fq, fk, fv, fkb, fvt, logf, logft, ckv, kr, qp = _proj(x, proj_w, tabs)
        tp = -(-t // LANES) * LANES
        pk, pv, plogf, pckv, pkr = past
        k_fox = _pad_time(jnp.concatenate([pk.reshape(b, n_past, FOX_W).astype(BF16), fkb], axis=1), tp, 1)
        pvt = jnp.swapaxes(pv.reshape(b, n_past, FOX_W), 1, 2).astype(BF16)
        vt_fox = _pad_time(jnp.concatenate([pvt, fvt], axis=2), tp, 2)
        logft_all = _pad_time(jnp.concatenate([jnp.swapaxes(plogf, 1, 2), logft], axis=2), tp, 2)
        ckv_all = _pad_time(jnp.concatenate([pckv, ckv], axis=1), tp, 1)
        kr_all = _pad_time(jnp.concatenate([pkr, kr], axis=1), tp, 1)
        k_mla, vt_mla = _kv_expand(ckv_all, kr_all, *kvx_w)
    nb_aug = _neg_cumsum(logft_all)
    sq = -(-s // LANES) * LANES
    fq_p, qp_p = _pad_time(fq, sq, 1), _pad_time(qp, sq, 1)
    o_fox = _attention_t(fq_p, k_fox, vt_fox, nb_aug, q_off=n_past, t_valid=t, chunked=False)[:, :s]
    o_mla = _attention_t(qp_p, k_mla, vt_mla, None, q_off=n_past, t_valid=t, chunked=True)[:, :s]
    r = b * s
    h = _mix(x.reshape(r, d), o_fox.reshape(r, FOX_W), o_mla.reshape(r, MLA_W), mix_w, alpha)
    ffn = _moe_ffn if is_moe else _dense_ffn
    y = ffn(h, p.reshape(r, -1), ffn_w, alpha).reshape(b, s, d)
    new_rows = (fk.reshape(b, s, FOX_HEADS, FOX_HEAD_DIM), fv.reshape(b, s, FOX_HEADS, FOX_HEAD_DIM), logf, ckv, kr)
    return y, new_rows


def kernel(x_prompt, x_sample, cache_fox_k, cache_fox_v, cache_fox_logf, cache_mla_ckv, cache_mla_krope,
           p_prompt, p_sample, w_in, b_fox_f, g_mla_cq, w_mla_qb, g_mla_ckv, w_mla_kvb, w_o_fox, w_o_mla,
           w_out, ln_mix_g, ln_mix_b, w_ffn_gate, w_ffn_up, w_ffn_down, w_router, b_router, w_moe_gate,
           w_moe_up, w_moe_down, w_ple_proj, w_ple_gate, ln_ffn_g, ln_ffn_b):
    depth = w_in.shape[0]
    alpha = (2 * depth) ** 0.25
    hp, hs = x_prompt, x_sample
    rows_p, rows_s = [], []
    for i in range(depth):
        proj_w, kvx_w, (wga, wgb) = _mixer_weights(w_in[i], b_fox_f[i], g_mla_cq[i], w_mla_qb[i], g_mla_ckv[i],
                                                   w_mla_kvb[i])
        mix_w = (wga, wgb, w_o_fox[i].astype(BF16), w_o_mla[i].astype(BF16), w_out[i].astype(BF16),
                 ln_mix_g[i].reshape(1, -1), ln_mix_b[i].reshape(1, -1))
        tail = (w_ple_gate[i].astype(BF16), w_ple_proj[i].astype(BF16),
                ln_ffn_g[i].reshape(1, -1), ln_ffn_b[i].reshape(1, -1))
        j = i // 2
        is_moe = i % 2 == 1
        if is_moe:
            ffn_w = (_pad_lanes(w_router[j]).astype(BF16), _pad_lanes(b_router[j].reshape(1, -1)),
                     w_moe_gate[j].astype(BF16), w_moe_up[j].astype(BF16), w_moe_down[j].astype(BF16)) + tail
        else:
            ffn_w = (w_ffn_gate[j].astype(BF16), w_ffn_up[j].astype(BF16), w_ffn_down[j].astype(BF16)) + tail
        past = (cache_fox_k[i], cache_fox_v[i], cache_fox_logf[i], cache_mla_ckv[i], cache_mla_krope[i])
        hp, new_p = _layer(hp, None, p_prompt[i], proj_w, kvx_w, mix_w, ffn_w, is_moe, alpha)
        hs, new_s = _layer(hs, past, p_sample[i], proj_w, kvx_w, mix_w, ffn_w, is_moe, alpha)
        rows_p.append(new_p)
        rows_s.append(new_s)

    def stack(rows, idx):
        return jnp.stack([r[idx] for r in rows], axis=0)

    return (hp, hs) + tuple(stack(rows_p, k) for k in range(5)) + tuple(stack(rows_s, k) for k in range(5))
```

```python
import functools

import jax
import jax.numpy as jnp
from jax import lax
from jax.experimental import pallas as pl
from jax.experimental.pallas import tpu as pltpu

CHUNK = 64
FOX_HEADS = 8
FOX_HEAD_DIM = 64
FOX_W = FOX_HEADS * FOX_HEAD_DIM
MLA_HEADS = 8
MLA_Q_RANK = 256
MLA_KV_RANK = 128
MLA_NOPE = 64
MLA_ROPE = 32
MLA_V = 64
MLA_W = MLA_HEADS * MLA_V
ROPE_THETA = 10000.0
N_EXPERTS = 8
LN_EPS = 1e-5
RMS_EPS = 1e-6
NEG_INF = -1e30
LOG2E = 1.4426950408889634

LANES = 128
SUBLANES = 8
PAIR_W = 2 * MLA_V
MLA_QK_W = 256
N_PAIRS = FOX_HEADS // 2
V7X_SCOPED_VMEM_BYTES = 60000 * 1024

BF16 = jnp.bfloat16
F32 = jnp.float32


def _dot(a, b):
    return jnp.dot(a, b, preferred_element_type=F32)


def _dot_nt(a, b):
    return lax.dot_general(a, b, (((1,), (1,)), ((), ())), preferred_element_type=F32)


def _sigmoid(x):
    return 1.0 / (1.0 + jnp.exp(-x))


def _log_sigmoid(x):
    return jnp.minimum(x, 0.0) - jnp.log1p(jnp.exp(-jnp.abs(x)))


def _rms_norm(x, g):
    return x * lax.rsqrt(jnp.mean(jnp.square(x), axis=-1, keepdims=True) + RMS_EPS) * g


def _layer_norm(x, g, b):
    mu = jnp.mean(x, axis=-1, keepdims=True)
    xc = x - mu
    var = jnp.mean(jnp.square(xc), axis=-1, keepdims=True)
    return xc * lax.rsqrt(var + LN_EPS) * g + b


def _resident(shape):
    nd = len(shape)
    return pl.BlockSpec(shape, lambda *_: (0,) * nd, pipeline_mode=pl.Buffered(1))


def _params(semantics, vmem_bytes):
    return pltpu.CompilerParams(dimension_semantics=semantics,
                                vmem_limit_bytes=min(int(vmem_bytes), V7X_SCOPED_VMEM_BYTES))


def _pick_tile(n, candidates):
    for c in candidates:
        if n % c == 0:
            return c
    return n


def _proj_kernel(x_ref, w1_ref, w2_ref, wfft_ref, wfvt_ref, bf_ref, bft_ref, gcq_ref, gckv_ref, wqa_ref, wqb_ref,
                 ctab_ref, stab_ref, cos_ref, sin_ref, *rest, expand_kv):
    if expand_kv:
        wk_ref, place_ref, wvt_ref = rest[:3]
        rest = rest[3:]
    fq_ref, fk_ref, fv_ref, fkb_ref, fvt_ref, logf_ref, logft_ref, ckv_ref, kr_ref, qp_ref = rest[:10]
    xb = x_ref[0].astype(BF16)
    z1 = _dot(xb, w1_ref[...])
    fq_ref[0] = z1[:, :FOX_W].astype(BF16)
    fk = z1[:, FOX_W:2 * FOX_W]
    fk_ref[0] = fk
    fkb_ref[0] = fk.astype(BF16)
    fv_ref[0] = z1[:, 2 * FOX_W:]
    fvt_ref[0] = _dot_nt(wfvt_ref[...], xb).astype(BF16)

    z2 = _dot(xb, w2_ref[...])
    cq = z2[:, :MLA_Q_RANK]
    o = MLA_Q_RANK
    ckv = z2[:, o:o + MLA_KV_RANK]
    o += MLA_KV_RANK
    kr = z2[:, o:o + MLA_ROPE]
    krr = z2[:, o + LANES:o + LANES + MLA_ROPE]
    ff = z2[:, o + 2 * LANES:o + 2 * LANES + FOX_HEADS]
    logf_ref[0] = _log_sigmoid(ff + bf_ref[...])
    fft = _dot_nt(wfft_ref[...], xb)
    logft_ref[0] = _log_sigmoid(fft[:FOX_HEADS] + bft_ref[...])
    ckv_n = _rms_norm(ckv, gckv_ref[...])
    ckv_ref[0] = ckv_n
    kr_rot = kr * cos_ref[...] + krr * sin_ref[...]
    kr_ref[0] = kr_rot
    cqn = _rms_norm(cq, gcq_ref[...]).astype(BF16)
    qp = _dot(cqn, wqa_ref[...]) * ctab_ref[...] + _dot(cqn, wqb_ref[...]) * stab_ref[...]
    qp_ref[0] = qp.astype(BF16)
    if expand_kv:
        kp_ref, vmt_ref = rest[10:]
        _expand_kv(ckv_n, kr_rot, wk_ref, place_ref, wvt_ref, kp_ref, vmt_ref)


def _expand_kv(ckv, kr, wk_ref, place_ref, wvt_ref, kp_ref, vmt_ref):
    cb = ckv.astype(BF16)
    kp_ref[0] = (_dot(cb, wk_ref[...]) + _dot(kr.astype(BF16), place_ref[...])).astype(BF16)
    vmt_ref[0] = _dot_nt(wvt_ref[...], cb).astype(BF16)


def _proj(x, w, tabs, kvx_w=None):
    b, s, d = x.shape
    tm = _pick_tile(s, (512, 256, 128))
    ns = s // tm
    w1, w2, wfft, wfvt, bf, bft, gcq, gckv, wqa, wqb = w
    ctab, stab, cos2, sin2 = tabs
    qw = N_PAIRS * MLA_QK_W

    def tok(width):
        return pl.BlockSpec((1, tm, width), lambda si, bi: (bi, si, 0))

    def tok_t(height):
        return pl.BlockSpec((1, height, tm), lambda si, bi: (bi, 0, si))

    def tab(width):
        return pl.BlockSpec((tm, width), lambda si, bi: (si, 0))

    in_specs = [tok(d), _resident(w1.shape), _resident(w2.shape), _resident(wfft.shape), _resident(wfvt.shape),
                _resident(bf.shape), _resident(bft.shape), _resident(gcq.shape), _resident(gckv.shape),
                _resident(wqa.shape), _resident(wqb.shape), tab(qw), tab(qw), tab(MLA_ROPE), tab(MLA_ROPE)]
    out_shape = (
        jax.ShapeDtypeStruct((b, s, FOX_W), BF16),
        jax.ShapeDtypeStruct((b, s, FOX_W), F32),
        jax.ShapeDtypeStruct((b, s, FOX_W), F32),
        jax.ShapeDtypeStruct((b, s, FOX_W), BF16),
        jax.ShapeDtypeStruct((b, FOX_W, s), BF16),
        jax.ShapeDtypeStruct((b, s, FOX_HEADS), F32),
        jax.ShapeDtypeStruct((b, FOX_HEADS, s), F32),
        jax.ShapeDtypeStruct((b, s, MLA_KV_RANK), F32),
        jax.ShapeDtypeStruct((b, s, MLA_ROPE), F32),
        jax.ShapeDtypeStruct((b, s, qw), BF16),
    )
    out_specs = (tok(FOX_W), tok(FOX_W), tok(FOX_W), tok(FOX_W), tok_t(FOX_W), tok(FOX_HEADS),
                 tok_t(FOX_HEADS), tok(MLA_KV_RANK), tok(MLA_ROPE), tok(qw))
    args = (x, w1, w2, wfft, wfvt, bf, bft, gcq, gckv, wqa, wqb, ctab, stab, cos2, sin2)
    if kvx_w is not None:
        in_specs += [_resident(a.shape) for a in kvx_w]
        args += tuple(kvx_w)
        out_shape += (jax.ShapeDtypeStruct((b, s, qw), BF16), jax.ShapeDtypeStruct((b, MLA_W, s), BF16))
        out_specs += (tok(qw), tok_t(MLA_W))
    return pl.pallas_call(
        functools.partial(_proj_kernel, expand_kv=kvx_w is not None), grid=(ns, b), in_specs=in_specs,
        out_specs=out_specs, out_shape=out_shape,
        compiler_params=_params(("arbitrary", "arbitrary"), 56 << 20), name="proj",
    )(*args)


def _cumsum_kernel(x_ref, aug_ref, *, ch):
    t = x_ref.shape[2]
    lane = lax.broadcasted_iota(jnp.int32, (1, LANES), 1)
    r = lax.broadcasted_iota(jnp.int32, (ch, ch), 0)
    c = lax.broadcasted_iota(jnp.int32, (ch, ch), 1)
    upper = jnp.where(r <= c, 1.0, 0.0).astype(BF16)
    carry = jnp.zeros((FOX_HEADS, 1), F32)
    zeros = jnp.zeros((FOX_HEADS, ch), F32)
    for ci in range(t // ch):
        xc = x_ref[0, :, ci * ch:(ci + 1) * ch]
        hi = xc.astype(BF16).astype(F32)
        r1 = xc - hi
        mid = r1.astype(BF16).astype(F32)
        lo = (r1 - mid).astype(BF16).astype(F32)
        pieces = jnp.concatenate([hi, mid, lo, zeros], axis=0).astype(BF16)
        pc = _dot(pieces, upper)
        cum = pc[0:8] + pc[8:16] + pc[16:24] + carry
        nb = cum * (-LOG2E)
        for hh in range(FOX_HEADS):
            rep = jnp.broadcast_to(nb[hh:hh + 1, :], (LANES, ch)).T
            a_hi = rep.astype(BF16).astype(F32)
            a_r = rep - a_hi
            a_mid = a_r.astype(BF16).astype(F32)
            a_lo = (a_r - a_mid).astype(BF16).astype(F32)
            aug = jnp.where(lane == 0, a_hi, jnp.where(lane == 1, a_mid, jnp.where(lane == 2, a_lo, 0.0)))
            aug_ref[0, hh, ci * ch:(ci + 1) * ch, :] = aug.astype(BF16)
        carry = cum[:, ch - 1:ch]


def _neg_cumsum(logft):
    b, h, t = logft.shape
    ch = 256 if t % 256 == 0 else LANES
    spec = pl.BlockSpec((1, h, t), lambda bi: (bi, 0, 0))
    return pl.pallas_call(
        functools.partial(_cumsum_kernel, ch=ch), grid=(b,), in_specs=[spec],
        out_specs=pl.BlockSpec((1, h, t, LANES), lambda bi: (bi, 0, 0, 0)),
        out_shape=jax.ShapeDtypeStruct((b, h, t, LANES), BF16),
        compiler_params=_params(("arbitrary",), 32 << 20), name="cumsum",
    )(logft)


def _kvx_kernel(ckv_ref, kr_ref, wk_ref, place_ref, wvt_ref, kp_ref, vmt_ref):
    _expand_kv(ckv_ref[0], kr_ref[0], wk_ref, place_ref, wvt_ref, kp_ref, vmt_ref)


def _kv_expand(ckv, kr, wk, place, wvt):
    b, t, _ = ckv.shape
    tm = _pick_tile(t, (512, 384, 256, 128))
    kw = N_PAIRS * MLA_QK_W

    def tok(width):
        return pl.BlockSpec((1, tm, width), lambda bi, ti: (bi, ti, 0))

    return pl.pallas_call(
        _kvx_kernel, grid=(b, t // tm),
        in_specs=[tok(MLA_KV_RANK), tok(MLA_ROPE), _resident(wk.shape), _resident(place.shape), _resident(wvt.shape)],
        out_specs=(tok(kw), pl.BlockSpec((1, MLA_W, tm), lambda bi, ti: (bi, 0, ti))),
        out_shape=(jax.ShapeDtypeStruct((b, t, kw), BF16), jax.ShapeDtypeStruct((b, MLA_W, t), BF16)),
        compiler_params=_params(("arbitrary", "arbitrary"), 32 << 20), name="kvexpand",
    )(ckv, kr, wk, place, wvt)


def _head_lane_mask(lane, hh, width):
    if width == LANES:
        return (lane // FOX_HEAD_DIM) == hh
    nope = (lane < 2 * MLA_NOPE) & ((lane // MLA_NOPE) == hh)
    rope = (lane >= 2 * MLA_NOPE) & (lane < 2 * MLA_NOPE + 2 * MLA_ROPE) & (
        ((lane - 2 * MLA_NOPE) // MLA_ROPE) == hh)
    return nope | rope


def _attn_t_kernel(*refs, tq, tk, width, q_off, t_valid, chunked, has_bias):
    if has_bias:
        q_ref, k_ref, vt_ref, nb_ref, o_ref, sa_ref, sb_ref = refs
    else:
        q_ref, k_ref, vt_ref, o_ref, sa_ref, sb_ref = refs
        nb_ref = None
    nq = q_ref.shape[1] // tq
    nkb = k_ref.shape[1] // tk

    items = []
    for qi in range(nq):
        q_min = q_off + qi * tq
        q_max = q_min + tq - 1
        if chunked:
            lim_min, lim_max = (q_min // CHUNK + 1) * CHUNK, (q_max // CHUNK + 1) * CHUNK
        else:
            lim_min, lim_max = q_min + 1, q_max + 1
        n_full = min(lim_min, t_valid) // tk
        n_vis = min(-(-min(lim_max, t_valid) // tk), nkb)
        items += [(qi, kb, kb >= n_full, kb == n_vis - 1) for kb in range(n_vis)]

    lane = lax.broadcasted_iota(jnp.int32, (1, width), 1)
    keep = [_head_lane_mask(lane, hh, width) for hh in range(2)]
    bufs = (sa_ref, sb_ref)

    ones3 = jnp.where(lax.broadcasted_iota(jnp.int32, (tq, LANES), 1) < 3, 1.0, 0.0).astype(BF16)
    ones_rows = jnp.ones((16, tk), BF16)

    def scores_into(s_ref, qi, kb):
        q = q_ref[0, qi * tq:(qi + 1) * tq, :]
        kblk = k_ref[0, kb * tk:(kb + 1) * tk, :]
        for hh in range(2):
            qh = jnp.where(keep[hh], q, jnp.zeros_like(q))
            if has_bias:
                kh = jnp.concatenate([kblk, nb_ref[0, hh, kb * tk:(kb + 1) * tk, :]], axis=1)
                s_ref[hh] = _dot_nt(kh, jnp.concatenate([qh, ones3], axis=1))
            else:
                s_ref[hh] = _dot_nt(kblk, qh)

    def consume(s_ref, qi, kb, stats, masked):
        if masked:
            qpos = q_off + qi * tq + lax.broadcasted_iota(jnp.int32, (1, tq), 1)
            if chunked:
                col_lim = lax.shift_left(lax.shift_right_logical(qpos, 6) + 1, 6)
            else:
                col_lim = qpos + 1
            col_lim = jnp.minimum(col_lim, t_valid)
            kpos = kb * tk + lax.broadcasted_iota(jnp.int32, (tk, 1), 0)
        out = []
        for hh in range(2):
            m, acc = stats[2 * hh:2 * hh + 2]
            s = s_ref[hh]
            if masked:
                s = jnp.where(kpos < col_lim, s, NEG_INF)
            m_new = jnp.maximum(m, jnp.max(s, axis=0, keepdims=True))
            p = jnp.exp2(s - m_new)
            alpha = jnp.exp2(m - m_new)
            vth = jnp.concatenate([vt_ref[0, hh * MLA_V:(hh + 1) * MLA_V, kb * tk:(kb + 1) * tk], ones_rows], axis=0)
            acc = alpha * acc + _dot(vth, p.astype(BF16))
            out += [m_new, acc]
        return tuple(out)

    scores_into(bufs[0], items[0][0], items[0][1])
    stats = None
    for i, (qi, kb, masked, last) in enumerate(items):
        if i + 1 < len(items):
            scores_into(bufs[(i + 1) % 2], items[i + 1][0], items[i + 1][1])
        if kb == 0:
            stats = (jnp.full((1, tq), NEG_INF, F32), jnp.zeros((MLA_V + 16, tq), F32)) * 2
        stats = consume(bufs[i % 2], qi, kb, stats, masked)
        if last:
            outs = [stats[2 * hh + 1][:MLA_V] * (1.0 / stats[2 * hh + 1][MLA_V:MLA_V + 1]) for hh in range(2)]
            o_ref[0, qi * tq:(qi + 1) * tq, :] = jnp.concatenate(outs, axis=0).T.astype(BF16)


def _attention_t(q, k, vt, nbrep, *, q_off, t_valid, chunked):
    assert CHUNK == 64
    b, sq, qw = q.shape
    width = qw // N_PAIRS
    tp = k.shape[1]
    tq = _pick_tile(sq, (512, 256, 128))
    tk = _pick_tile(tp, (512, 384, 256, 128))
    in_specs = [pl.BlockSpec((1, sq, width), lambda bi, pi: (bi, 0, pi)),
                pl.BlockSpec((1, tp, width), lambda bi, pi: (bi, 0, pi)),
                pl.BlockSpec((1, PAIR_W, tp), lambda bi, pi: (bi, pi, 0))]
    args = [q, k, vt]
    if nbrep is not None:
        in_specs.append(pl.BlockSpec((1, 2, tp, LANES), lambda bi, pi: (bi, pi, 0, 0)))
        args.append(nbrep)
    kern = functools.partial(_attn_t_kernel, tq=tq, tk=tk, width=width, q_off=q_off, t_valid=t_valid,
                             chunked=chunked, has_bias=nbrep is not None)
    return pl.pallas_call(
        kern, grid=(b, N_PAIRS), in_specs=in_specs,
        out_specs=pl.BlockSpec((1, sq, PAIR_W), lambda bi, pi: (bi, 0, pi)),
        out_shape=jax.ShapeDtypeStruct((b, sq, N_PAIRS * PAIR_W), BF16),
        scratch_shapes=[pltpu.VMEM((2, tk, tq), F32), pltpu.VMEM((2, tk, tq), F32)],
        compiler_params=_params(("arbitrary", "arbitrary"), 40 << 20),
        name="attn_t_mla" if chunked else "attn_t_fox",
    )(*args)


def _mix_rows(x_ref, of_ref, om_ref, wga_ref, wgb_ref, wof_ref, wom_ref, wout_ref, g_ref, b_ref, alpha):
    x = x_ref[...]
    xb = x.astype(BF16)
    a = _sigmoid(_dot(xb, wga_ref[...])) * _dot(of_ref[...], wof_ref[...])
    bm = _sigmoid(_dot(xb, wgb_ref[...])) * _dot(om_ref[...], wom_ref[...])
    m = _dot((a + bm).astype(BF16), wout_ref[...])
    return _layer_norm(alpha * x + m, g_ref[...], b_ref[...])


def _mix_kernel(*refs, alpha):
    refs[-1][...] = _mix_rows(*refs[:-1], alpha)


def _mix(x, of, om, w, alpha):
    r, d = x.shape
    tm = _pick_tile(r, (512, 256, 128))

    def row(width):
        return pl.BlockSpec((tm, width), lambda i: (i, 0))

    return pl.pallas_call(
        functools.partial(_mix_kernel, alpha=alpha), grid=(r // tm,),
        in_specs=[row(d), row(FOX_W), row(MLA_W)] + [_resident(a.shape) for a in w],
        out_specs=row(d), out_shape=jax.ShapeDtypeStruct((r, d), F32),
        compiler_params=_params(("arbitrary",), 48 << 20), name="mix",
    )(x, of, om, *w)


def _ple(hb, p_ref, wpg_ref, wpp_ref):
    return _sigmoid(_dot(hb, wpg_ref[...])) * _dot(p_ref[...].astype(BF16), wpp_ref[...])


def _ffn_chunk(n_ff):
    return _pick_tile(n_ff, (1408, 1024, 512, 256, 128))


def _dense_rows(h, p_ref, wg_ref, wu_ref, wd_ref, wpg_ref, wpp_ref, g_ref, b_ref, alpha, fc):
    hb = h.astype(BF16)
    n_ff = wg_ref.shape[1]
    acc = jnp.zeros(h.shape, F32)
    for c in range(n_ff // fc):
        sl = slice(c * fc, (c + 1) * fc)
        gt = _dot(hb, wg_ref[:, sl])
        act = (gt * _sigmoid(gt) * _dot(hb, wu_ref[:, sl])).astype(BF16)
        acc = acc + _dot(act, wd_ref[sl, :])
    y = alpha * h + acc + _ple(hb, p_ref, wpg_ref, wpp_ref)
    return _layer_norm(y, g_ref[...], b_ref[...])


N_MIX_REFS = 10


def _mix_dense_kernel(*refs, alpha, fc):
    h = _mix_rows(*refs[:N_MIX_REFS], alpha)
    refs[-1][...] = _dense_rows(h, *refs[N_MIX_REFS:-1], alpha, fc)


def _mix_dense(x, of, om, p, mix_w, ffn_w, alpha):
    r, d = x.shape
    tm = _pick_tile(r, (512, 256, 128))

    def row(width):
        return pl.BlockSpec((tm, width), lambda i: (i, 0))

    assert 3 + len(mix_w) == N_MIX_REFS
    return pl.pallas_call(
        functools.partial(_mix_dense_kernel, alpha=alpha, fc=_ffn_chunk(ffn_w[0].shape[1])), grid=(r // tm,),
        in_specs=[row(d), row(FOX_W), row(MLA_W)] + [_resident(a.shape) for a in mix_w] + [row(p.shape[1])]
        + [_resident(a.shape) for a in ffn_w],
        out_specs=row(d), out_shape=jax.ShapeDtypeStruct((r, d), F32),
        compiler_params=_params(("arbitrary",), V7X_SCOPED_VMEM_BYTES), name="mix_ffn_dense",
    )(x, of, om, *mix_w, p, *ffn_w)


MOE_ROW_TILE = 512
ROUTE_LANES = 6


def _route_kernel(h_ref, wr_ref, br_ref, route_ref, counts_ref, carry_ref):
    @pl.when(pl.program_id(0) == 0)
    def _init():
        carry_ref[...] = jnp.zeros(carry_ref.shape, F32)

    hb = h_ref[...].astype(BF16)
    tm, d = hb.shape
    lane = lax.broadcasted_iota(jnp.int32, (1, LANES), 1)
    logits = _dot(hb, wr_ref[...]) + br_ref[...]
    lg = jnp.where(lane < N_EXPERTS, logits, -jnp.inf)
    m1 = jnp.max(lg, axis=-1, keepdims=True)
    i1 = jnp.min(jnp.where(lg == m1, lane, LANES), axis=-1, keepdims=True)
    lg2 = jnp.where(lane == i1, -jnp.inf, lg)
    m2 = jnp.max(lg2, axis=-1, keepdims=True)
    i2 = jnp.min(jnp.where(lg2 == m2, lane, LANES), axis=-1, keepdims=True)
    e2 = jnp.exp(m2 - m1)
    den = 1.0 + e2
    hit1 = lane == i1
    hit2 = lane == i2
    onehot = jnp.where(hit1, 1.0, 0.0) + jnp.where(hit2, 1.0, 0.0)
    rr = lax.broadcasted_iota(jnp.int32, (tm, tm), 0)
    cc = lax.broadcasted_iota(jnp.int32, (tm, tm), 1)
    lower = jnp.where(cc < rr, 1.0, 0.0).astype(BF16)
    prefix = _dot(lower, onehot.astype(BF16)) + carry_ref[...]
    rank1 = jnp.sum(jnp.where(hit1, prefix, 0.0), axis=-1, keepdims=True)
    rank2 = jnp.sum(jnp.where(hit2, prefix, 0.0), axis=-1, keepdims=True)
    fields = (i1.astype(F32), i2.astype(F32), rank1, rank2, 1.0 / den, e2 / den)
    route = jnp.zeros((tm, LANES), F32)
    for j, f in enumerate(fields):
        route = jnp.where(lane == j, f, route)
    route_ref[...] = route
    carry = carry_ref[...] + jnp.sum(onehot, axis=0, keepdims=True)
    carry_ref[...] = carry
    counts_ref[...] = carry


def _route(h, wr, br):
    r, d = h.shape
    tm = _pick_tile(r, (512, 256, 128))
    return pl.pallas_call(
        _route_kernel, grid=(r // tm,),
        in_specs=[pl.BlockSpec((tm, d), lambda i: (i, 0)), _resident(wr.shape), _resident(br.shape)],
        out_specs=(pl.BlockSpec((tm, LANES), lambda i: (i, 0)), pl.BlockSpec((1, LANES), lambda i: (0, 0))),
        out_shape=(jax.ShapeDtypeStruct((r, LANES), F32), jax.ShapeDtypeStruct((1, LANES), F32)),
        scratch_shapes=[pltpu.VMEM((1, LANES), F32)],
        compiler_params=_params(("arbitrary",), 32 << 20), name="moe_route",
    )(h, wr, br)


ROW_DMA_UNROLL = 8


def _row_scatter_kernel(idx_ref, src_ref, init_ref, dst_ref, stage_ref, sem, *, tm):
    del init_ref
    for j in range(SUBLANES):
        stage_ref[pl.ds(j, tm, stride=SUBLANES), :] = src_ref[:, j * LANES:(j + 1) * LANES]

    def copy(t, d_row):
        src = stage_ref.at[pl.ds(pl.multiple_of(t * SUBLANES, SUBLANES), SUBLANES)]
        return pltpu.make_async_copy(src, dst_ref.at[pl.ds(pl.multiple_of(d_row * SUBLANES, SUBLANES), SUBLANES)], sem)

    def issue(t, c):
        copy(t, idx_ref[0, 0, 2 * t]).start(priority=0)
        copy(t, idx_ref[0, 0, 2 * t + 1]).start(priority=1)
        return c

    lax.fori_loop(0, tm, issue, 0, unroll=ROW_DMA_UNROLL)
    for _ in range(2):
        pltpu.make_async_copy(stage_ref, dst_ref.at[pl.ds(0, tm * SUBLANES)], sem).wait()


def _row_scatter(idx3, src, init):
    steps, _, n = idx3.shape
    tm = n // 2
    assert src.shape[1] == SUBLANES * LANES
    any_spec = pl.BlockSpec(memory_space=pl.ANY)
    return pl.pallas_call(
        functools.partial(_row_scatter_kernel, tm=tm), grid=(steps,),
        in_specs=[pl.BlockSpec((1, 1, n), lambda i: (i, 0, 0), memory_space=pltpu.SMEM),
                  pl.BlockSpec((tm, src.shape[1]), lambda i: (i, 0)), any_spec],
        out_specs=any_spec, out_shape=jax.ShapeDtypeStruct(init.shape, init.dtype),
        scratch_shapes=[pltpu.VMEM((tm * SUBLANES, LANES), F32), pltpu.SemaphoreType.DMA(())],
        input_output_aliases={2: 0},
        compiler_params=_params(("arbitrary",), 16 << 20), name="moe_scatter",
    )(idx3, src, init)


def _grouped_ffn_kernel(te_ref, tv_ref, x_ref, wg_ref, wu_ref, wd_ref, y_ref, *, fc):
    del te_ref
    valid = tv_ref[pl.program_id(0)] != 0

    @pl.when(valid)
    def _compute():
        tmx = x_ref.shape[0] // SUBLANES
        xb = jnp.concatenate([x_ref[pl.ds(j, tmx, stride=SUBLANES), :] for j in range(SUBLANES)],
                             axis=1).astype(BF16)
        n_ff = wg_ref.shape[2]
        acc = jnp.zeros((tmx, wd_ref.shape[2]), F32)
        for c in range(n_ff // fc):
            sl = slice(c * fc, (c + 1) * fc)
            gt = _dot(xb, wg_ref[0, :, sl])
            act = (gt * _sigmoid(gt) * _dot(xb, wu_ref[0, :, sl])).astype(BF16)
            acc = acc + _dot(act, wd_ref[0, sl, :])
        for j in range(SUBLANES):
            y_ref[pl.ds(j, tmx, stride=SUBLANES), :] = acc[:, j * LANES:(j + 1) * LANES]

    @pl.when(jnp.logical_not(valid))
    def _skip():
        y_ref[...] = jnp.zeros(y_ref.shape, F32)


def _grouped_ffn(tile_expert, tile_valid, xs, wg, wu, wd):
    n_e, d, n_ff = wg.shape
    tmx = MOE_ROW_TILE
    rp = xs.shape[0] // SUBLANES
    grid_spec = pltpu.PrefetchScalarGridSpec(
        num_scalar_prefetch=2, grid=(rp // tmx,),
        in_specs=[pl.BlockSpec((tmx * SUBLANES, LANES), lambda i, te, tv: (i, 0)),
                  pl.BlockSpec((1, d, n_ff), lambda i, te, tv: (te[i], 0, 0)),
                  pl.BlockSpec((1, d, n_ff), lambda i, te, tv: (te[i], 0, 0)),
                  pl.BlockSpec((1, n_ff, d), lambda i, te, tv: (te[i], 0, 0))],
        out_specs=pl.BlockSpec((tmx * SUBLANES, LANES), lambda i, te, tv: (i, 0)))
    assert d == SUBLANES * LANES
    return pl.pallas_call(
        functools.partial(_grouped_ffn_kernel, fc=_ffn_chunk(n_ff)), grid_spec=grid_spec,
        out_shape=jax.ShapeDtypeStruct((rp * SUBLANES, LANES), F32),
        compiler_params=_params(("arbitrary",), V7X_SCOPED_VMEM_BYTES), name="moe_ffn",
    )(tile_expert, tile_valid, xs, wg, wu, wd)


def _moe_out_kernel(idx_ref, idx_next_ref, h_ref, p_ref, route_ref, ys_ref, wpg_ref, wpp_ref, g_ref, b_ref, y_ref,
                    buf_ref, sem, *, alpha):
    tm = h_ref.shape[0]
    step = pl.program_id(0)
    slot = lax.rem(step, 2)

    def copy(sl, k, t, s_row):
        src = ys_ref.at[pl.ds(pl.multiple_of(s_row * SUBLANES, SUBLANES), SUBLANES)]
        dst = buf_ref.at[sl, k, pl.ds(pl.multiple_of(t * SUBLANES, SUBLANES), SUBLANES)]
        return pltpu.make_async_copy(src, dst, sem.at[sl])

    def issue_block(rows_ref, sl):
        def issue(t, c):
            copy(sl, 0, t, rows_ref[0, 0, 2 * t]).start(priority=0)
            copy(sl, 1, t, rows_ref[0, 0, 2 * t + 1]).start(priority=1)
            return c

        lax.fori_loop(0, tm, issue, 0, unroll=ROW_DMA_UNROLL)

    @pl.when(step == 0)
    def _first():
        issue_block(idx_ref, 0)

    @pl.when(step + 1 < pl.num_programs(0))
    def _ahead():
        issue_block(idx_next_ref, 1 - slot)

    h = h_ref[...]
    hb = h.astype(BF16)
    ple = _ple(hb, p_ref, wpg_ref, wpp_ref)

    for k in range(2):
        pltpu.make_async_copy(ys_ref.at[pl.ds(0, tm * SUBLANES)], buf_ref.at[slot, k], sem.at[slot]).wait()

    def rows(k):
        view = buf_ref.at[slot, k]
        return jnp.concatenate([view[pl.ds(j, tm, stride=SUBLANES), :] for j in range(SUBLANES)], axis=1)

    rt = route_ref[...]
    moe = rt[:, 4:5] * rows(0) + rt[:, 5:6] * rows(1)
    y_ref[...] = _layer_norm(alpha * h + moe + ple, g_ref[...], b_ref[...])


def _moe_out(idx3, h, p, route, ys, w, alpha):
    r, d = h.shape
    steps, _, n = idx3.shape
    tm = n // 2

    def row(width):
        return pl.BlockSpec((tm, width), lambda i: (i, 0))

    return pl.pallas_call(
        functools.partial(_moe_out_kernel, alpha=alpha), grid=(steps,),
        in_specs=[pl.BlockSpec((1, 1, n), lambda i: (i, 0, 0), memory_space=pltpu.SMEM),
                  pl.BlockSpec((1, 1, n), lambda i: (jnp.minimum(i + 1, steps - 1), 0, 0), memory_space=pltpu.SMEM),
                  row(d), row(p.shape[1]), row(LANES), pl.BlockSpec(memory_space=pl.ANY)]
        + [_resident(a.shape) for a in w],
        out_specs=row(d), out_shape=jax.ShapeDtypeStruct((r, d), F32),
        scratch_shapes=[pltpu.VMEM((2, 2, tm * SUBLANES, LANES), F32), pltpu.SemaphoreType.DMA((2,))],
        compiler_params=_params(("arbitrary",), 48 << 20), name="moe_out",
    )(idx3, idx3, h, p, route, ys, *w)


def _moe_ffn(h, p, w, alpha):
    r, d = h.shape
    wr, br, wg, wu, wd, wpg, wpp, g, bb = w
    tmx = MOE_ROW_TILE
    route, counts = _route(h, wr, br)
    cnt = counts[0, :N_EXPERTS].astype(jnp.int32)
    padded = (cnt + tmx - 1) // tmx * tmx
    ends = jnp.cumsum(padded)
    dest = jnp.take(ends - padded, route[:, 0:2].astype(jnp.int32)) + route[:, 2:4].astype(jnp.int32)
    n_tiles = -(-2 * r // tmx) + N_EXPERTS
    starts = jnp.arange(n_tiles, dtype=jnp.int32) * tmx
    tile_expert = jnp.minimum(jnp.sum(ends[None, :] <= starts[:, None], axis=1), N_EXPERTS - 1).astype(jnp.int32)
    tile_valid = (starts < ends[-1]).astype(jnp.int32)
    tm = _pick_tile(r, (512, 256, 128))
    dest3 = dest.reshape(r // tm, 1, 2 * tm)
    xs = _row_scatter(dest3, h, jnp.zeros((n_tiles * tmx * SUBLANES, LANES), F32))
    ys = _grouped_ffn(tile_expert, tile_valid, xs, wg, wu, wd)
    return _moe_out(dest3, h, p, route, ys, (wpg, wpp, g, bb), alpha)


def _pad_lanes(a, width=LANES):
    return jnp.pad(a, ((0, 0), (0, width - a.shape[1])))


def _rot_half_cols(a):
    half = a.shape[-1] // 2
    return jnp.concatenate([-a[..., half:], a[..., :half]], axis=-1)


def _mixer_weights(w_in, b_fox_f, g_cq, w_qb, g_ckv, w_kvb):
    d = w_in.shape[0]
    sizes = (FOX_W, FOX_W, FOX_W, FOX_HEADS, MLA_Q_RANK, MLA_KV_RANK, MLA_ROPE, d, d)
    cols, start = [], 0
    for n in sizes:
        cols.append(w_in[:, start:start + n])
        start += n
    wfq, wfk, wfv, wff, wcq, wckv, wkr, wga, wgb = cols
    w1 = jnp.concatenate([wfq * (FOX_HEAD_DIM ** -0.5 * LOG2E), wfk, wfv], axis=1).astype(BF16)
    w2 = jnp.concatenate([wcq, wckv, _pad_lanes(wkr), _pad_lanes(_rot_half_cols(wkr)), _pad_lanes(wff)],
                         axis=1).astype(BF16)
    wfft = jnp.pad(wff.T, ((0, 16 - FOX_HEADS), (0, 0))).astype(BF16)
    bf = b_fox_f.reshape(1, FOX_HEADS)
    bft = b_fox_f.reshape(FOX_HEADS, 1)

    qb = w_qb.reshape(MLA_Q_RANK, MLA_HEADS, MLA_NOPE + MLA_ROPE)
    q_nope, q_rope = qb[..., :MLA_NOPE], qb[..., MLA_NOPE:]
    q_rot = _rot_half_cols(q_rope)
    z_pad = jnp.zeros((MLA_Q_RANK, MLA_QK_W - 2 * MLA_NOPE - 2 * MLA_ROPE), F32)
    z_nope = jnp.zeros((MLA_Q_RANK, 2 * MLA_NOPE), F32)
    wqa = jnp.concatenate([jnp.concatenate([q_nope[:, 2 * j], q_nope[:, 2 * j + 1], q_rope[:, 2 * j],
                                            q_rope[:, 2 * j + 1], z_pad], axis=1) for j in range(N_PAIRS)], axis=1)
    wqb = jnp.concatenate([jnp.concatenate([z_nope, q_rot[:, 2 * j], q_rot[:, 2 * j + 1], z_pad], axis=1)
                           for j in range(N_PAIRS)], axis=1)

    kvb = w_kvb.reshape(MLA_KV_RANK, MLA_HEADS, MLA_NOPE + MLA_V)
    k_nope, v_up = kvb[..., :MLA_NOPE], kvb[..., MLA_NOPE:]
    zk = jnp.zeros((MLA_KV_RANK, MLA_QK_W - 2 * MLA_NOPE), F32)
    wk = jnp.concatenate([jnp.concatenate([k_nope[:, 2 * j], k_nope[:, 2 * j + 1], zk], axis=1)
                          for j in range(N_PAIRS)], axis=1).astype(BF16)
    eye = jnp.eye(MLA_ROPE, dtype=F32)
    place = jnp.concatenate([jnp.zeros((MLA_ROPE, 2 * MLA_NOPE), F32), eye, eye,
                             jnp.zeros((MLA_ROPE, MLA_QK_W - 2 * MLA_NOPE - 2 * MLA_ROPE), F32)], axis=1)
    place = jnp.tile(place, (1, N_PAIRS)).astype(BF16)
    wv = v_up.reshape(MLA_KV_RANK, MLA_W).astype(BF16)
    proj_w = (w1, w2, wfft, wfv.T.astype(BF16), bf, bft, g_cq.reshape(1, -1), g_ckv.reshape(1, -1),
              wqa.astype(BF16), wqb.astype(BF16))
    return proj_w, (wk, place, wv.T), (wga.astype(BF16), wgb.astype(BF16))


def _rope_tables(pos):
    half = MLA_ROPE // 2
    inv = ROPE_THETA ** (-jnp.arange(half, dtype=F32) * 2.0 / MLA_ROPE)
    ang = pos.astype(F32)[:, None] * inv[None, :]
    cos2 = jnp.concatenate([jnp.cos(ang)] * 2, axis=1)
    sin2 = jnp.concatenate([jnp.sin(ang)] * 2, axis=1)
    n = pos.shape[0]
    scale = (MLA_NOPE + MLA_ROPE) ** -0.5 * LOG2E
    pad = jnp.zeros((n, MLA_QK_W - 2 * MLA_NOPE - 2 * MLA_ROPE), F32)
    ctab = jnp.concatenate([jnp.full((n, 2 * MLA_NOPE), scale, F32), scale * cos2, scale * cos2, pad], axis=1)
    stab = jnp.concatenate([jnp.zeros((n, 2 * MLA_NOPE), F32), scale * sin2, scale * sin2, pad], axis=1)
    return jnp.tile(ctab, (1, N_PAIRS)), jnp.tile(stab, (1, N_PAIRS)), cos2, sin2


def _pad_time(a, tp, axis):
    pad = [(0, 0)] * a.ndim
    pad[axis] = (0, tp - a.shape[axis])
    return jnp.pad(a, pad)


def _layer(x, past, p, proj_w, kvx_w, mix_w, ffn_w, is_moe, alpha):
    b, s, d = x.shape
    n_past = 0 if past is None else past[0].shape[1]
    t = n_past + s
    tabs = _rope_tables(n_past + jnp.arange(s))
    if past is None:
        tp = t
        fq, fk, fv, fkb, fvt, logf, logft, ckv, kr, qp, k_mla, vt_mla = _proj(x, proj_w, tabs, kvx_w)
        k_fox, vt_fox, logft_all = fkb, fvt, logft
    else:
        fq, fk, fv, fkb, fvt, logf, logft, ckv, kr, qp = _proj(x, proj_w, tabs)
        tp = -(-t // LANES) * LANES
        pk, pv, plogf, pckv, pkr = past
        k_fox = _pad_time(jnp.concatenate([pk.reshape(b, n_past, FOX_W).astype(BF16), fkb], axis=1), tp, 1)
        pvt = jnp.swapaxes(pv.reshape(b, n_past, FOX_W), 1, 2).astype(BF16)
        vt_fox = _pad_time(jnp.concatenate([pvt, fvt], axis=2), tp, 2)
        logft_all = _pad_time(jnp.concatenate([jnp.swapaxes(plogf, 1, 2), logft], axis=2), tp, 2)
        ckv_all = _pad_time(jnp.concatenate([pckv, ckv], axis=1), tp, 1)
        kr_all = _pad_time(jnp.concatenate([pkr, kr], axis=1), tp, 1)
        k_mla, vt_mla = _kv_expand(ckv_all, kr_all, *kvx_w)
    nb_aug = _neg_cumsum(logft_all)
    sq = -(-s // LANES) * LANES
    fq_p, qp_p = _pad_time(fq, sq, 1), _pad_time(qp, sq, 1)
    o_fox = _attention_t(fq_p, k_fox, vt_fox, nb_aug, q_off=n_past, t_valid=t, chunked=False)[:, :s]
    o_mla = _attention_t(qp_p, k_mla, vt_mla, None, q_off=n_past, t_valid=t, chunked=True)[:, :s]
    r = b * s
    rows = (x.reshape(r, d), o_fox.reshape(r, FOX_W), o_mla.reshape(r, MLA_W))
    if is_moe:
        y = _moe_ffn(_mix(*rows, mix_w, alpha), p.reshape(r, -1), ffn_w, alpha)
    else:
        y = _mix_dense(*rows, p.reshape(r, -1), mix_w, ffn_w, alpha)
    y = y.reshape(b, s, d)
    new_rows = (fk.reshape(b, s, FOX_HEADS, FOX_HEAD_DIM), fv.reshape(b, s, FOX_HEADS, FOX_HEAD_DIM), logf, ckv, kr)
    return y, new_rows


def kernel(x_prompt, x_sample, cache_fox_k, cache_fox_v, cache_fox_logf, cache_mla_ckv, cache_mla_krope,
           p_prompt, p_sample, w_in, b_fox_f, g_mla_cq, w_mla_qb, g_mla_ckv, w_mla_kvb, w_o_fox, w_o_mla,
           w_out, ln_mix_g, ln_mix_b, w_ffn_gate, w_ffn_up, w_ffn_down, w_router, b_router, w_moe_gate,
           w_moe_up, w_moe_down, w_ple_proj, w_ple_gate, ln_ffn_g, ln_ffn_b):
    depth = w_in.shape[0]
    alpha = (2 * depth) ** 0.25
    hp, hs = x_prompt, x_sample
    rows_p, rows_s = [], []
    for i in range(depth):
        proj_w, kvx_w, (wga, wgb) = _mixer_weights(w_in[i], b_fox_f[i], g_mla_cq[i], w_mla_qb[i], g_mla_ckv[i],
                                                   w_mla_kvb[i])
        mix_w = (wga, wgb, w_o_fox[i].astype(BF16), w_o_mla[i].astype(BF16), w_out[i].astype(BF16),
                 ln_mix_g[i].reshape(1, -1), ln_mix_b[i].reshape(1, -1))
        tail = (w_ple_gate[i].astype(BF16), w_ple_proj[i].astype(BF16),
                ln_ffn_g[i].reshape(1, -1), ln_ffn_b[i].reshape(1, -1))
        j = i // 2
        is_moe = i % 2 == 1
        if is_moe:
            ffn_w = (_pad_lanes(w_router[j]).astype(BF16), _pad_lanes(b_router[j].reshape(1, -1)),
                     w_moe_gate[j].astype(BF16), w_moe_up[j].astype(BF16), w_moe_down[j].astype(BF16)) + tail
        else:
            ffn_w = (w_ffn_gate[j].astype(BF16), w_ffn_up[j].astype(BF16), w_ffn_down[j].astype(BF16)) + tail
        past = (cache_fox_k[i], cache_fox_v[i], cache_fox_logf[i], cache_mla_ckv[i], cache_mla_krope[i])
        hp, new_p = _layer(hp, None, p_prompt[i], proj_w, kvx_w, mix_w, ffn_w, is_moe, alpha)
        hs, new_s = _layer(hs, past, p_sample[i], proj_w, kvx_w, mix_w, ffn_w, is_moe, alpha)
        rows_p.append(new_p)
        rows_s.append(new_s)

    def stack(rows, idx):
        return jnp.stack([r[idx] for r in rows], axis=0)

    return (hp, hs) + tuple(stack(rows_p, k) for k in range(5)) + tuple(stack(rows_s, k) for k in range(5))
```
